```python
import math
import jax, jax.numpy as jnp
from jax import lax
import numpy as np

D_MODEL = 1024
BATCH = 16
SEQ = 256
DEPTH = 2
DEC_BATCH = 2
DEC_SEQ = 4096
PAST_LEN = 256

GRID_W = 64
H_A = 4
DK_A = 64
DV_A = 128
RET_CHUNK = 128
H_B = 4
DK_B = 64
DV_B = 128
GLA_CHUNK = 16
GLA_RANK = 16
GLA_TAU = 16.0
H_C = 4
DH_C = 64
Q_BLOCK = 128
D_FF = 2816
CONV_W = 3
ROPE_BASE = 10000.0
N_BRANCH = 3
BRANCH_W = 512
ALPHA = (2 * DEPTH) ** 0.25
BETA = (8 * DEPTH) ** -0.25
EPS = 1e-5
SPLIT_SIZES = (H_A * DK_A, H_A * DK_A, H_A * DV_A, H_A * DV_A,
               H_B * DK_B, H_B * DK_B, H_B * DV_B, H_B * DV_B,
               2 * H_C * DH_C, 2 * H_C * DH_C, 2 * H_C * DH_C,
               D_MODEL, D_MODEL, D_MODEL)
D_IN = sum(SPLIT_SIZES)

kernel_name = 'hybrid_ret_gla_diffattn_prefix_dit_step'


def _normalize(x):
    xf = x.astype(jnp.float32)
    mu = xf.mean(-1, keepdims=True)
    var = jnp.square(xf - mu).mean(-1, keepdims=True)
    return (xf - mu) * lax.rsqrt(var + EPS)


def layer_norm(x, g, b):
    return (_normalize(x) * g + b).astype(x.dtype)


def rms_norm(x, g):
    xf = x.astype(jnp.float32)
    return (xf * lax.rsqrt(jnp.mean(xf * xf, -1, keepdims=True) + EPS) * g).astype(x.dtype)


def heads(x, g):
    B, T, _ = x.shape
    return x.reshape(B, T, g, -1).transpose(0, 2, 1, 3)


def merge_heads(x):
    B, G, T, d = x.shape
    return x.transpose(0, 2, 1, 3).reshape(B, T, G * d)


def rev(x):
    return jnp.flip(x, axis=2)


def grid_positions(T):
    rows = T // GRID_W
    row = jnp.repeat(jnp.arange(rows, dtype=jnp.float32), GRID_W)
    col = (jnp.arange(T) % GRID_W).astype(jnp.float32)
    return row, col


def rope_1d(x, p):
    half = x.shape[-1] // 2
    inv = ROPE_BASE ** (-jnp.arange(half, dtype=jnp.float32) / half)
    ang = p[:, None] * inv[None, :]
    cos, sin = jnp.cos(ang), jnp.sin(ang)
    x1, x2 = x[..., :half], x[..., half:]
    return jnp.concatenate([x1 * cos - x2 * sin, x2 * cos + x1 * sin], axis=-1)


def rope_2d(x, pos):
    row, col = pos
    d = x.shape[-1] // 2
    xf = x.astype(jnp.float32)
    return jnp.concatenate([rope_1d(xf[..., :d], row), rope_1d(xf[..., d:], col)], axis=-1).astype(x.dtype)


def chunk_state_scan(q_dec, k_dec, v, chunk_decay, s0):
    delta = jnp.einsum('bgnck,bgncv->bgnkv', k_dec, v)

    def step(s, inp):
        dec, d = inp
        return dec[..., None] * s + d, s

    s_final, s_before = lax.scan(step, s0.astype(jnp.float32),
                                 (jnp.moveaxis(chunk_decay, 2, 0), jnp.moveaxis(delta, 2, 0)))
    s_before = jnp.moveaxis(s_before, 0, 2)
    inter = jnp.einsum('bgnck,bgnkv->bgncv', q_dec, s_before)
    return inter, s_final


def retention_dir(q, k, v, log_gamma, s0):
    B, H, T, _ = q.shape
    C = RET_CHUNK
    n = T // C
    qc = q.reshape(B, H, n, C, -1)
    kc = k.reshape(B, H, n, C, -1)
    vc = v.reshape(B, H, n, C, -1)
    idx = jnp.arange(C, dtype=jnp.float32)
    lg = log_gamma[:, None]
    b = (idx + 1.0)[None, :] * lg
    diff = idx[:, None] - idx[None, :]
    dmat = jnp.where(diff >= 0, jnp.exp(jnp.maximum(diff, 0.0)[None] * lg[:, :, None]), 0.0)
    scores = jnp.einsum('bhnik,bhnjk->bhnij', qc, kc) * dmat[None, :, None]
    intra = jnp.einsum('bhnij,bhnjv->bhniv', scores, vc)
    q_dec = qc * jnp.exp(b)[None, :, None, :, None]
    k_dec = kc * jnp.exp(C * lg - b)[None, :, None, :, None]
    chunk_decay = jnp.broadcast_to(jnp.exp(C * lg)[None, :, None, :], (B, H, n, qc.shape[-1]))
    inter, s_final = chunk_state_scan(q_dec, k_dec, vc, chunk_decay, s0)
    return (intra + inter).reshape(B, H, T, -1), s_final


def gla_dir(q, k, v, log_a, s0):
    B, H, T, _ = q.shape
    C = GLA_CHUNK
    n = T // C
    qc = q.reshape(B, H, n, C, -1)
    kc = k.reshape(B, H, n, C, -1)
    vc = v.reshape(B, H, n, C, -1)
    b = jnp.cumsum(log_a.reshape(B, H, n, C, -1), axis=3)
    b_last = b[:, :, :, -1:, :]
    causal = jnp.tril(jnp.ones((C, C), dtype=bool))
    pair = jnp.where(causal[:, :, None], b[:, :, :, :, None, :] - b[:, :, :, None, :, :], -jnp.inf)
    attn = jnp.einsum('bhntk,bhnsk,bhntsk->bhnts', qc, kc, jnp.exp(pair))
    intra = jnp.einsum('bhnts,bhnsv->bhntv', attn, vc)
    q_dec = qc * jnp.exp(b)
    k_dec = kc * jnp.exp(b_last - b)
    inter, s_final = chunk_state_scan(q_dec, k_dec, vc, jnp.exp(b_last[:, :, :, 0]), s0)
    return (intra + inter).reshape(B, H, T, -1), s_final


def diff_attention(q, k, v, lam, lam_init, subln_g):
    B, G, Tq, dh = q.shape
    nb = Tq // Q_BLOCK
    qb = jnp.moveaxis(q.reshape(B, G, nb, Q_BLOCK, dh), 2, 0)
    scale = dh ** -0.5

    def block(qi):
        s = jnp.einsum('bgqd,bgkd->bgqk', qi, k).astype(jnp.float32) * scale
        p = jax.nn.softmax(s, axis=-1).reshape(B, H_C, 2, Q_BLOCK, -1)
        a = (p[:, :, 0] - lam * p[:, :, 1]).astype(v.dtype)
        return jnp.einsum('bhqk,bhkv->bhqv', a, v)

    o = lax.map(block, qb)
    o = jnp.moveaxis(o, 0, 2).reshape(B, H_C, Tq, -1)
    o = rms_norm(o, subln_g) * (1.0 - lam_init)
    return merge_heads(o)


def token_mixer(h, l, P, pos, ctx):
    B, T, _ = h.shape
    f32 = jnp.float32
    dt = h.dtype
    points = np.cumsum(SPLIT_SIZES)[:-1].tolist()
    (a_q, a_k, a_v, a_g, b_q, b_k, b_v, b_r, c_q, c_k, c_v,
     m_a, m_b, m_c) = jnp.split(h @ P['w_in'][l], points, axis=-1)

    q = heads(a_q, H_A).astype(f32)
    k = heads(a_k, H_A).astype(f32) * DK_A ** -0.5
    if pos is not None:
        q, k = rope_2d(q, pos), rope_2d(k, pos)
    v = heads(a_v, H_A).astype(f32)
    s0 = jnp.zeros((B, 2, H_A, DK_A, DV_A), f32) if ctx is None else ctx['ret'].astype(f32)
    lg = jax.nn.log_sigmoid(P['ret_decay'][l].astype(f32))
    ya_f, sa_f = retention_dir(q, k, v, lg[0], s0[:, 0])
    ya_b, sa_b = retention_dir(rev(q), rev(k), rev(v), lg[1], s0[:, 1])
    y_a = merge_heads(_normalize(ya_f + rev(ya_b))) * jax.nn.silu(a_g.astype(f32))
    ret_state = jnp.stack([sa_f, sa_b], axis=1)

    q = heads(b_q, H_B).astype(f32) * DK_B ** -0.5
    k = heads(b_k, H_B).astype(f32)
    v = heads(b_v, H_B).astype(f32)

    def log_gate(e):
        z = (h @ P['gla_wa1'][l, e]) @ P['gla_wa2'][l, e] + P['gla_ba'][l, e]
        return heads(jax.nn.log_sigmoid(z.astype(f32)) / GLA_TAU, H_B)

    s0 = jnp.zeros((B, 2, H_B, DK_B, DV_B), f32) if ctx is None else ctx['gla'].astype(f32)
    yb_f, sb_f = gla_dir(q, k, v, log_gate(0), s0[:, 0])
    yb_b, sb_b = gla_dir(rev(q), rev(k), rev(v), rev(log_gate(1)), s0[:, 1])
    y_b = merge_heads(rms_norm(yb_f + rev(yb_b), P['gla_norm_g'][l])) * jax.nn.silu(b_r.astype(f32))
    gla_state = jnp.stack([sb_f, sb_b], axis=1)

    cq = heads(c_q, 2 * H_C)
    ck = heads(c_k, 2 * H_C)
    cv = heads(c_v, H_C)
    if pos is not None:
        cq, ck = rope_2d(cq, pos), rope_2d(ck, pos)
    if ctx is None:
        k_all, v_all = ck, cv
    else:
        k_all = jnp.concatenate([ctx['dk'].astype(ck.dtype), ck], axis=2)
        v_all = jnp.concatenate([ctx['dv'].astype(cv.dtype), cv], axis=2)
    lam_init = 0.8 - 0.6 * math.exp(-0.3 * l)
    lp = P['diff_lam'][l].astype(f32)
    lam = jnp.exp(jnp.sum(lp[0] * lp[1])) - jnp.exp(jnp.sum(lp[2] * lp[3])) + lam_init
    y_c = diff_attention(cq, k_all, v_all, lam, lam_init, P['diff_subln_g'][l])

    wb = P['w_branch'][l]
    merged = (jax.nn.sigmoid(m_a) * (y_a.astype(dt) @ wb[0])
              + jax.nn.sigmoid(m_b) * (y_b.astype(dt) @ wb[1])
              + jax.nn.sigmoid(m_c) * (y_c.astype(dt) @ wb[2]))
    out = merged @ P['w_out'][l]
    return out, (ck, cv, ret_state.astype(dt), gla_state.astype(dt))


def conv_ffn(h, l, P):
    a, b = jnp.split(h @ P['ffn_w_up'][l], 2, axis=-1)
    w = P['ffn_conv_w'][l]
    ap = jnp.pad(a, ((0, 0), (1, 1), (0, 0)))
    a = ap[:, :-2] * w[0] + ap[:, 1:-1] * w[1] + ap[:, 2:] * w[2] + P['ffn_conv_b'][l]
    return (jax.nn.gelu(a) * b) @ P['ffn_w_down'][l]


def trunk_layer(x, cvec, l, P, pos, ctx):
    mod = jax.nn.silu(cvec) @ P['ada_w'][l] + P['ada_b'][l]
    sh1, sc1, g1, sh2, sc2, g2 = jnp.split(mod, 6, axis=-1)
    mix, ctx_out = token_mixer(x * (1.0 + sc1) + sh1, l, P, pos, ctx)
    x = layer_norm(ALPHA * x + g1 * mix, P['ln_g'][l, 0], P['ln_b'][l, 0])
    ff = conv_ffn(x * (1.0 + sc2) + sh2, l, P)
    x = layer_norm(ALPHA * x + g2 * ff, P['ln_g'][l, 1], P['ln_b'][l, 1])
    return x, ctx_out


def setup_inputs(seed: int = 0) -> dict:
    key = jax.random.key(seed)
    ks = jax.random.split(key, 26)
    f32 = jnp.float32

    def nrm(k, shape, s):
        return s * jax.random.normal(k, shape, f32)

    gam = 1.0 - 2.0 ** (-5.0 - jnp.arange(H_A, dtype=f32))
    return {
        'x_prompt': nrm(ks[0], (BATCH, SEQ, D_MODEL), 1.0),
        'x_sample': nrm(ks[1], (DEC_BATCH, DEC_SEQ, D_MODEL), 1.0),
        'cache_diff_k': nrm(ks[2], (DEC_BATCH, DEPTH, 2 * H_C, PAST_LEN, DH_C), 1.0),
        'cache_diff_v': nrm(ks[3], (DEC_BATCH, DEPTH, H_C, PAST_LEN, 2 * DH_C), 1.0),
        'state_ret': nrm(ks[4], (DEC_BATCH, DEPTH, 2, H_A, DK_A, DV_A), 0.5),
        'state_gla': nrm(ks[5], (DEC_BATCH, DEPTH, 2, H_B, DK_B, DV_B), 0.5),
        'c': nrm(ks[6], (DEC_BATCH, D_MODEL), 1.0),
        'c_ctx': nrm(ks[7], (D_MODEL,), 1.0),
        'ada_w': nrm(ks[8], (DEPTH, D_MODEL, 6 * D_MODEL), 0.5 * D_MODEL ** -0.5),
        'ada_b': nrm(ks[9], (DEPTH, 6 * D_MODEL), 0.02),
        'w_in': nrm(ks[10], (DEPTH, D_MODEL, D_IN), D_MODEL ** -0.5),
        'ret_decay': jnp.log(gam / (1.0 - gam))[None, None, :] + nrm(ks[11], (DEPTH, 2, H_A), 0.1),
        'gla_wa1': nrm(ks[12], (DEPTH, 2, D_MODEL, GLA_RANK), D_MODEL ** -0.5),
        'gla_wa2': nrm(ks[13], (DEPTH, 2, GLA_RANK, H_B * DK_B), GLA_RANK ** -0.5),
        'gla_ba': nrm(ks[14], (DEPTH, 2, H_B * DK_B), 0.1),
        'gla_norm_g': 1.0 + nrm(ks[15], (DEPTH, DV_B), 0.02),
        'diff_lam': nrm(ks[16], (DEPTH, 4, DH_C), 0.1),
        'diff_subln_g': 1.0 + nrm(ks[17], (DEPTH, 2 * DH_C), 0.02),
        'w_branch': nrm(ks[18], (DEPTH, N_BRANCH, BRANCH_W, D_MODEL), BETA * BRANCH_W ** -0.5),
        'w_out': nrm(ks[19], (DEPTH, D_MODEL, D_MODEL), BETA * D_MODEL ** -0.5),
        'ln_g': 1.0 + nrm(ks[20], (DEPTH, 2, D_MODEL), 0.02),
        'ln_b': nrm(ks[21], (DEPTH, 2, D_MODEL), 0.02),
        'ffn_w_up': nrm(ks[22], (DEPTH, D_MODEL, 2 * D_FF), BETA * D_MODEL ** -0.5),
        'ffn_conv_w': nrm(ks[23], (DEPTH, CONV_W, D_FF), CONV_W ** -0.5),
        'ffn_conv_b': nrm(ks[24], (DEPTH, D_FF), 0.02),
        'ffn_w_down': nrm(ks[25], (DEPTH, D_FF, D_MODEL), BETA * D_FF ** -0.5),
    }


def reference(x_prompt, x_sample, cache_diff_k, cache_diff_v, state_ret, state_gla, c, c_ctx,
              ada_w, ada_b, w_in, ret_decay, gla_wa1, gla_wa2, gla_ba, gla_norm_g, diff_lam,
              diff_subln_g, w_branch, w_out, ln_g, ln_b, ffn_w_up, ffn_conv_w, ffn_conv_b, ffn_w_down):
    P = {'ada_w': ada_w, 'ada_b': ada_b, 'w_in': w_in, 'ret_decay': ret_decay,
         'gla_wa1': gla_wa1, 'gla_wa2': gla_wa2, 'gla_ba': gla_ba, 'gla_norm_g': gla_norm_g,
         'diff_lam': diff_lam, 'diff_subln_g': diff_subln_g, 'w_branch': w_branch, 'w_out': w_out,
         'ln_g': ln_g, 'ln_b': ln_b, 'ffn_w_up': ffn_w_up, 'ffn_conv_w': ffn_conv_w,
         'ffn_conv_b': ffn_conv_b, 'ffn_w_down': ffn_w_down}

    h = x_prompt
    cc = c_ctx[None, None, :]
    dks, dvs, rets, glas = [], [], [], []
    for l in range(DEPTH):
        h, (dk_l, dv_l, ret_l, gla_l) = trunk_layer(h, cc, l, P, None, None)
        dks.append(dk_l)
        dvs.append(dv_l)
        rets.append(ret_l)
        glas.append(gla_l)
    y_prompt = h
    new_diff_k = jnp.stack(dks, axis=1)
    new_diff_v = jnp.stack(dvs, axis=1)
    new_state_ret = jnp.stack(rets, axis=1)
    new_state_gla = jnp.stack(glas, axis=1)

    pos = grid_positions(x_sample.shape[1])
    z = x_sample
    cs = c[:, None, :]
    for l in range(DEPTH):
        ctx = {'dk': cache_diff_k[:, l], 'dv': cache_diff_v[:, l],
               'ret': state_ret[:, l], 'gla': state_gla[:, l]}
        z, _ = trunk_layer(z, cs, l, P, pos, ctx)
    y_sample = z

    return (y_prompt, y_sample, new_diff_k, new_diff_v, new_state_ret, new_state_gla)
```

```python
import functools
import math

import numpy as np
import jax
import jax.numpy as jnp
from jax import lax
from jax.experimental import pallas as pl
from jax.experimental.pallas import tpu as pltpu

F32 = jnp.float32
BF16 = jnp.bfloat16

D_MODEL = 1024
DEPTH = 2
GRID_W = 64
N_HEAD = 4
DK = 64
DV = 128
GLA_RANK = 16
GLA_TAU = 16.0
D_FF = 2816
ROPE_BASE = 10000.0
ALPHA = (2 * DEPTH) ** 0.25
EPS = 1e-5
D_IN = 7680

LANES = 128
SUBLANES = 8
VMEM_LIMIT = 56 * 1024 * 1024

N_GATE_COLS = 3 * D_MODEL
CB_M = 0
CB_AQ, CB_AK, CB_AV, CB_AG = 24, 26, 28, 32
CB_BQ, CB_BK, CB_BV, CB_BR = 36, 38, 40, 44
CB_CQ, CB_CK, CB_CV = 48, 52, 56
ROPE_BLOCKS = (24, 25, 26, 27, 48, 49, 50, 51, 52, 53, 54, 55)
SCALED_BLOCKS = (26, 27, 36, 37, 48, 49, 50, 51)
QK_SCALE = DK ** -0.5

TOKEN_BLOCK = 512
RET_CHUNK = 256
GLA_BLOCK = 128
Q_BLOCK = 256
FF_CHUNK = 256


def _cparams(sem):
    return pltpu.CompilerParams(dimension_semantics=sem, vmem_limit_bytes=VMEM_LIMIT)


def _resident(shape):
    zeros = (0,) * len(shape)
    return pl.BlockSpec(shape, lambda *_: zeros, pipeline_mode=pl.Buffered(1))


def _mod_spec(row0, span, k):
    return pl.BlockSpec((None, None, 1, D_MODEL),
                        lambda i: (row0 + (i * TOKEN_BLOCK) // span, k, 0, 0))


def _sigmoid(x):
    return 1.0 / (1.0 + jnp.exp(-x))


def _log_sigmoid(x):
    return jnp.minimum(x, 0.0) - jnp.log(1.0 + jnp.exp(-jnp.abs(x)))


def _dot(a, b):
    return jnp.dot(a, b, preferred_element_type=F32)


def _dot_nt(a, b):
    return lax.dot_general(a, b, (((1,), (1,)), ((), ())), preferred_element_type=F32)


def _dot_tn(a, b):
    return lax.dot_general(a, b, (((0,), (0,)), ((), ())), preferred_element_type=F32)


def _ada_kernel(c_ref, w_ref, b_ref, o_ref):
    c = c_ref[...]
    s = (c * _sigmoid(c)).astype(BF16)
    o_ref[...] = _dot(s, w_ref[...].astype(BF16)) + b_ref[...]


def _ada_call(cvec, ada_w, ada_b):
    nt = 1536
    return pl.pallas_call(
        _ada_kernel,
        grid=(DEPTH, 6 * D_MODEL // nt),
        in_specs=[pl.BlockSpec((SUBLANES, D_MODEL), lambda l, j: (0, 0)),
                  pl.BlockSpec((None, D_MODEL, nt), lambda l, j: (l, 0, j)),
                  pl.BlockSpec((None, 1, nt), lambda l, j: (l, 0, j))],
        out_specs=pl.BlockSpec((None, SUBLANES, nt), lambda l, j: (l, 0, j)),
        out_shape=jax.ShapeDtypeStruct((DEPTH, SUBLANES, 6 * D_MODEL), F32),
        compiler_params=_cparams(("arbitrary", "arbitrary")),
    )(cvec, ada_w, ada_b.reshape(DEPTH, 1, 6 * D_MODEL))


def _proj_kernel(*refs, rope, emit_kv):
    x_ref, sh_ref, sc_ref, w_ref, wa1_ref, wa2_ref, ba_ref = refs[:7]
    pos = 7
    if rope:
        cos_ref, sa_ref, sb_ref = refs[pos:pos + 3]
        pos += 3
    proj_ref, la_ref = refs[pos:pos + 2]
    pos += 2
    if emit_kv:
        ck_ref, cv_ref = refs[pos:pos + 2]

    h = (x_ref[...] * (1.0 + sc_ref[...]) + sh_ref[...]).astype(BF16)
    tile = 512
    per = tile // LANES
    for j in range(D_IN // tile):
        acc = _dot(h, w_ref[:, j * tile:(j + 1) * tile])
        for i in range(per):
            blk = j * per + i
            y = acc[:, i * LANES:(i + 1) * LANES]
            if emit_kv and CB_CK <= blk < CB_CK + 4:
                ck_ref[:, (blk - CB_CK) * LANES:(blk - CB_CK + 1) * LANES] = y
            if emit_kv and CB_CV <= blk < CB_CV + 4:
                cv_ref[:, (blk - CB_CV) * LANES:(blk - CB_CV + 1) * LANES] = y
            if rope and blk in ROPE_BLOCKS:
                y = (y * cos_ref[...] + pltpu.roll(y, LANES - 16, 1) * sa_ref[...]
                     + pltpu.roll(y, 16, 1) * sb_ref[...])
            if blk in SCALED_BLOCKS:
                y = y * QK_SCALE
            proj_ref[:, blk * LANES:(blk + 1) * LANES] = y.astype(BF16)

    r = _dot(h, wa1_ref[...]).astype(BF16)
    z = _dot(r, wa2_ref[...]) + ba_ref[...]
    la_ref[...] = _log_sigmoid(z) * (1.0 / GLA_TAU)


def _proj_call(x, mod5, row0, span, seq_len, w_in, wa1, wa2, ba, tables, emit_kv):
    n = x.shape[0]
    tb = TOKEN_BLOCK
    bps = max(seq_len // tb, 1)
    rope = tables is not None

    in_specs = [pl.BlockSpec((tb, D_MODEL), lambda i: (i, 0)),
                _mod_spec(row0, span, 0), _mod_spec(row0, span, 1),
                _resident((D_MODEL, D_IN)), _resident((D_MODEL, LANES)),
                _resident((LANES, 4 * LANES)), _resident((1, 4 * LANES))]
    args = [x, mod5, mod5, w_in, wa1, wa2, ba]
    if rope:
        in_specs += [pl.BlockSpec((tb, LANES), lambda i: (i % bps, 0))] * 3
        args += list(tables)
    out_specs = [pl.BlockSpec((tb, D_IN), lambda i: (i, 0)),
                 pl.BlockSpec((tb, 4 * LANES), lambda i: (i, 0))]
    out_shape = [jax.ShapeDtypeStruct((n, D_IN), BF16), jax.ShapeDtypeStruct((n, 4 * LANES), F32)]
    if emit_kv:
        out_specs += [pl.BlockSpec((tb, 4 * LANES), lambda i: (i, 0))] * 2
        out_shape += [jax.ShapeDtypeStruct((n, 4 * LANES), F32)] * 2
    return pl.pallas_call(
        functools.partial(_proj_kernel, rope=rope, emit_kv=emit_kv),
        grid=(n // tb,), in_specs=in_specs, out_specs=out_specs, out_shape=out_shape,
        compiler_params=_cparams(("arbitrary",)),
    )(*args)


def _ret_kernel(*refs, seq, chunk, has_ctx):
    dec_ref, q_ref, k_ref, v_ref, g_ref = refs[:5]
    pos = 5
    if has_ctx:
        s0_ref = refs[pos]
        pos += 1
    y_ref, st_ref, of_ref, ob_ref, s_ref, dm_ref, eq_ref, ek_ref, gc_ref = refs[pos:]
    p = pl.program_id(1)
    nc = seq // chunk
    c_f = float(chunk)
    lane = lax.broadcasted_iota(jnp.int32, (1, LANES), 1)
    hmask = [lane < DK, lane >= DK]
    rowp = lax.broadcasted_iota(jnp.int32, (chunk, 1), 0).astype(F32)
    colp = lax.broadcasted_iota(jnp.int32, (1, chunk), 1).astype(F32)
    diff = rowp - colp

    for d in range(2):
        for hh in range(2):
            ci = 2 * d + hh
            raw = dec_ref[d, 2 * p + hh]
            lg_l = _log_sigmoid(jnp.full((1, LANES), raw, F32))
            lg_c = _log_sigmoid(jnp.full((1, chunk), raw, F32))
            if d == 0:
                dm_ref[ci] = jnp.where(diff >= 0, jnp.exp(jnp.maximum(diff, 0.0) * lg_c), 0.0)
                eq_ref[ci] = jnp.exp((rowp + 1.0) * lg_l)
                ek_ref[ci] = jnp.exp((c_f - 1.0 - rowp) * lg_l)
            else:
                dm_ref[ci] = jnp.where(diff <= 0, jnp.exp(jnp.maximum(-diff, 0.0) * lg_c), 0.0)
                eq_ref[ci] = jnp.exp((c_f - rowp) * lg_l)
                ek_ref[ci] = jnp.exp(rowp * lg_l)
            gc_ref[ci] = jnp.exp(c_f * lg_l)
            if has_ctx:
                s0 = s0_ref[d, hh]
                zero = jnp.zeros((DK, DV), F32)
                s_ref[ci] = jnp.concatenate([s0, zero] if hh == 0 else [zero, s0], axis=0)
            else:
                s_ref[ci] = jnp.zeros((LANES, DV), F32)

    def step(n, carry):
        for d in range(2):
            c = n if d == 0 else nc - 1 - n
            r0 = pl.multiple_of(c * chunk, chunk)
            qc = q_ref[pl.ds(r0, chunk), :]
            kc = k_ref[pl.ds(r0, chunk), :]
            o_ref = of_ref if d == 0 else ob_ref
            for hh in range(2):
                ci = 2 * d + hh
                qh = jnp.where(hmask[hh], qc, jnp.zeros_like(qc))
                kh = jnp.where(hmask[hh], kc, jnp.zeros_like(kc))
                vh = v_ref[pl.ds(r0, chunk), hh * DV:(hh + 1) * DV]
                sc = _dot_nt(qh, kh) * dm_ref[ci]
                intra = _dot(sc.astype(BF16), vh)
                sb = s_ref[ci]
                qd = (qh.astype(F32) * eq_ref[ci]).astype(BF16)
                inter = _dot(qd, sb.astype(BF16))
                o_ref[hh, pl.ds(r0, chunk), :] = intra + inter
                kd = (kh.astype(F32) * ek_ref[ci]).astype(BF16)
                s_ref[ci] = gc_ref[ci] * sb + _dot_tn(kd, vh)
        return carry

    lax.fori_loop(0, nc, step, 0)

    for d in range(2):
        for hh in range(2):
            st_ref[d, hh] = s_ref[2 * d + hh][hh * DK:(hh + 1) * DK, :]

    def finish(n, carry):
        r0 = pl.multiple_of(n * chunk, chunk)
        for hh in range(2):
            o = of_ref[hh, pl.ds(r0, chunk), :] + ob_ref[hh, pl.ds(r0, chunk), :]
            mu = jnp.mean(o, axis=-1, keepdims=True)
            oc = o - mu
            var = jnp.mean(oc * oc, axis=-1, keepdims=True)
            g = g_ref[pl.ds(r0, chunk), hh * DV:(hh + 1) * DV].astype(F32)
            y = oc * lax.rsqrt(var + EPS) * (g * _sigmoid(g))
            y_ref[pl.ds(r0, chunk), hh * DV:(hh + 1) * DV] = y.astype(BF16)
        return carry

    lax.fori_loop(0, nc, finish, 0)


def _ret_call(proj3, decay, s0):
    b, t, _ = proj3.shape
    chunk = min(RET_CHUNK, t)
    has_ctx = s0 is not None
    st_spec = pl.BlockSpec((None, 2, 2, DK, DV), lambda i, p: (i, 0, p, 0, 0))
    in_specs = [pl.BlockSpec(memory_space=pltpu.SMEM),
                pl.BlockSpec((None, t, LANES), lambda i, p: (i, 0, CB_AQ + p)),
                pl.BlockSpec((None, t, LANES), lambda i, p: (i, 0, CB_AK + p)),
                pl.BlockSpec((None, t, 2 * DV), lambda i, p: (i, 0, CB_AV // 2 + p)),
                pl.BlockSpec((None, t, 2 * DV), lambda i, p: (i, 0, CB_AG // 2 + p))]
    args = [decay, proj3, proj3, proj3, proj3]
    if has_ctx:
        in_specs.append(st_spec)
        args.append(s0)
    return pl.pallas_call(
        functools.partial(_ret_kernel, seq=t, chunk=chunk, has_ctx=has_ctx),
        grid=(b, 2), in_specs=in_specs,
        out_specs=[pl.BlockSpec((None, t, 2 * DV), lambda i, p: (i, 0, p)), st_spec],
        out_shape=[jax.ShapeDtypeStruct((b, t, N_HEAD * DV), BF16),
                   jax.ShapeDtypeStruct((b, 2, N_HEAD, DK, DV), F32)],
        scratch_shapes=[pltpu.VMEM((2, t, DV), F32), pltpu.VMEM((2, t, DV), F32),
                        pltpu.VMEM((4, LANES, DV), F32), pltpu.VMEM((4, chunk, chunk), F32),
                        pltpu.VMEM((4, chunk, LANES), F32), pltpu.VMEM((4, chunk, LANES), F32),
                        pltpu.VMEM((4, 1, LANES), F32)],
        compiler_params=_cparams(("arbitrary", "arbitrary")),
    )(*args)


def _gla_block(q, k, x, v_pair, st_list, hmask, blk, reverse):
    row = lax.broadcasted_iota(jnp.int32, (blk, 1), 0)
    xr = (lax.broadcasted_iota(jnp.int32, (blk, blk), 0)
          ^ lax.broadcasted_iota(jnp.int32, (blk, blk), 1))
    w = x
    tot = x
    qb = q.astype(BF16)
    kb = k.astype(BF16)
    amat = []
    for hh in range(2):
        qh = jnp.where(hmask[hh], qb, jnp.zeros_like(qb))
        amat.append(jnp.where(xr == 0, _dot_nt(qh, kb), 0.0))
    h = 1
    while h < blk:
        up = (row & h) != 0
        qside = jnp.logical_not(up) if reverse else up
        f = jnp.exp(-jnp.where(qside, w, tot - w))
        ql = jnp.where(qside, q * f, 0.0).astype(BF16)
        kl = jnp.where(qside, 0.0, k * f).astype(BF16)
        for hh in range(2):
            qh = jnp.where(hmask[hh], ql, jnp.zeros_like(ql))
            amat[hh] = amat[hh] + jnp.where(xr < 2 * h, _dot_nt(qh, kl), 0.0)
        partner = jnp.where(up, pltpu.roll(tot, h, 0), pltpu.roll(tot, blk - h, 0))
        w = w + jnp.where(qside, partner, 0.0)
        tot = tot + partner
        h *= 2
    qd = (q * jnp.exp(-w)).astype(BF16)
    kd = (k * jnp.exp(-(tot - w))).astype(BF16)
    dec = jnp.exp(-tot[0:1, :])
    outs, new_st = [], []
    for hh in range(2):
        vh = v_pair[:, hh * DV:(hh + 1) * DV]
        st = st_list[hh]
        qh = jnp.where(hmask[hh], qd, jnp.zeros_like(qd))
        kh = jnp.where(hmask[hh], kd, jnp.zeros_like(kd))
        o = _dot(amat[hh].astype(BF16), vh) + _dot_nt(qh, st.astype(BF16))
        outs.append(o)
        new_st.append(dec * st + _dot_tn(vh, kh))
    return outs, new_st


def _gla_kernel(*refs, seq, blk, has_ctx):
    q_ref, k_ref, v_ref, r_ref, laf_ref, lab_ref, ng_ref = refs[:7]
    pos = 7
    if has_ctx:
        s0_ref = refs[pos]
        pos += 1
    y_ref, st_ref, of_ref, ob_ref, s_ref = refs[pos:]
    nb = seq // blk
    lane = lax.broadcasted_iota(jnp.int32, (1, LANES), 1)
    hmask = [lane < DK, lane >= DK]

    for d in range(2):
        for hh in range(2):
            if has_ctx:
                zero = jnp.zeros((DK, DV), F32)
                s0 = s0_ref[d, hh]
                full = jnp.concatenate([s0, zero] if hh == 0 else [zero, s0], axis=0)
                s_ref[2 * d + hh] = full.T
            else:
                s_ref[2 * d + hh] = jnp.zeros((DV, LANES), F32)

    def step(n, carry):
        for d in range(2):
            c = n if d == 0 else nb - 1 - n
            r0 = pl.multiple_of(c * blk, blk)
            q = q_ref[pl.ds(r0, blk), :].astype(F32)
            k = k_ref[pl.ds(r0, blk), :].astype(F32)
            la_ref = laf_ref if d == 0 else lab_ref
            x = -la_ref[pl.ds(r0, blk), :]
            vp = v_ref[pl.ds(r0, blk), :]
            sts = [s_ref[2 * d], s_ref[2 * d + 1]]
            outs, new_st = _gla_block(q, k, x, vp, sts, hmask, blk, d == 1)
            o_ref = of_ref if d == 0 else ob_ref
            for hh in range(2):
                o_ref[hh, pl.ds(r0, blk), :] = outs[hh]
                s_ref[2 * d + hh] = new_st[hh]
        return carry

    lax.fori_loop(0, nb, step, 0)

    for d in range(2):
        for hh in range(2):
            st_ref[d, hh] = s_ref[2 * d + hh][...].T[hh * DK:(hh + 1) * DK, :]

    fin = min(seq, 512)

    def finish(n, carry):
        r0 = pl.multiple_of(n * fin, fin)
        for hh in range(2):
            o = of_ref[hh, pl.ds(r0, fin), :] + ob_ref[hh, pl.ds(r0, fin), :]
            ms = jnp.mean(o * o, axis=-1, keepdims=True)
            g = r_ref[pl.ds(r0, fin), hh * DV:(hh + 1) * DV].astype(F32)
            y = o * lax.rsqrt(ms + EPS) * ng_ref[...] * (g * _sigmoid(g))
            y_ref[pl.ds(r0, fin), hh * DV:(hh + 1) * DV] = y.astype(BF16)
        return carry

    lax.fori_loop(0, seq // fin, finish, 0)


def _gla_call(proj3, la3, norm_g, s0):
    b, t, _ = proj3.shape
    blk = GLA_BLOCK
    has_ctx = s0 is not None
    st_spec = pl.BlockSpec((None, 2, 2, DK, DV), lambda i, p: (i, 0, p, 0, 0))
    in_specs = [pl.BlockSpec((None, t, LANES), lambda i, p: (i, 0, CB_BQ + p)),
                pl.BlockSpec((None, t, LANES), lambda i, p: (i, 0, CB_BK + p)),
                pl.BlockSpec((None, t, 2 * DV), lambda i, p: (i, 0, CB_BV // 2 + p)),
                pl.BlockSpec((None, t, 2 * DV), lambda i, p: (i, 0, CB_BR // 2 + p)),
                pl.BlockSpec((None, t, LANES), lambda i, p: (i, 0, p)),
                pl.BlockSpec((None, t, LANES), lambda i, p: (i, 0, 2 + p)),
                pl.BlockSpec((1, DV), lambda i, p: (0, 0))]
    args = [proj3, proj3, proj3, proj3, la3, la3, norm_g]
    if has_ctx:
        in_specs.append(st_spec)
        args.append(s0)
    return pl.pallas_call(
        functools.partial(_gla_kernel, seq=t, blk=blk, has_ctx=has_ctx),
        grid=(b, 2), in_specs=in_specs,
        out_specs=[pl.BlockSpec((None, t, 2 * DV), lambda i, p: (i, 0, p)), st_spec],
        out_shape=[jax.ShapeDtypeStruct((b, t, N_HEAD * DV), BF16),
                   jax.ShapeDtypeStruct((b, 2, N_HEAD, DK, DV), F32)],
        scratch_shapes=[pltpu.VMEM((2, t, DV), F32), pltpu.VMEM((2, t, DV), F32),
                        pltpu.VMEM((4, DV, LANES), F32)],
        compiler_params=_cparams(("arbitrary", "arbitrary")),
    )(*args)


def _diff_kernel(lam_ref, q_ref, k_ref, v_ref, sg_ref, y_ref, *, lam_init):
    lp = lam_ref[...]
    lam = (jnp.exp(jnp.sum(lp[0:1] * lp[1:2], axis=-1, keepdims=True))
           - jnp.exp(jnp.sum(lp[2:3] * lp[3:4], axis=-1, keepdims=True)) + lam_init)
    lane = lax.broadcasted_iota(jnp.int32, (1, LANES), 1)
    q = q_ref[...]
    k = k_ref[...]
    v = v_ref[...]
    parts = []
    for i in range(2):
        qi = jnp.where((lane < DK) if i == 0 else (lane >= DK), q, jnp.zeros_like(q))
        s = _dot_nt(qi, k)
        m = jnp.max(s, axis=-1, keepdims=True)
        e = jnp.exp(s - m)
        den = jnp.sum(e, axis=-1, keepdims=True)
        parts.append(_dot(e.astype(BF16), v) / den)
    o = parts[0] - lam * parts[1]
    ms = jnp.mean(o * o, axis=-1, keepdims=True)
    y = o * lax.rsqrt(ms + EPS) * sg_ref[...] * (1.0 - lam_init)
    y_ref[...] = y.astype(BF16)


def _diff_call(q3, k3, v3, kcb, vcb, lam_p, subln_g, lam_init):
    b, tq = q3.shape[0], q3.shape[1]
    tk = k3.shape[1]
    qb = min(Q_BLOCK, tq)
    return pl.pallas_call(
        functools.partial(_diff_kernel, lam_init=lam_init),
        grid=(b, N_HEAD, tq // qb),
        in_specs=[pl.BlockSpec((4, DK), lambda i, h, j: (0, 0)),
                  pl.BlockSpec((None, qb, LANES), lambda i, h, j: (i, j, CB_CQ + h)),
                  pl.BlockSpec((None, tk, LANES), lambda i, h, j: (i, 0, kcb + h)),
                  pl.BlockSpec((None, tk, DV), lambda i, h, j: (i, 0, vcb + h)),
                  pl.BlockSpec((1, DV), lambda i, h, j: (0, 0))],
        out_specs=pl.BlockSpec((None, qb, DV), lambda i, h, j: (i, j, h)),
        out_shape=jax.ShapeDtypeStruct((b, tq, N_HEAD * DV), BF16),
        compiler_params=_cparams(("arbitrary", "arbitrary", "arbitrary")),
    )(lam_p, q3, k3, v3, subln_g)


def _layer_norm(z, g, b):
    mu = jnp.mean(z, axis=-1, keepdims=True)
    zc = z - mu
    var = jnp.mean(zc * zc, axis=-1, keepdims=True)
    return zc * lax.rsqrt(var + EPS) * g + b


def _merge_kernel(x_ref, ya_ref, yb_ref, yc_ref, m_ref, g1_ref, wb_ref, wo_ref, lg_ref, lb_ref, o_ref):
    merged = None
    for i, y_ref in enumerate((ya_ref, yb_ref, yc_ref)):
        gate = _sigmoid(m_ref[:, i * D_MODEL:(i + 1) * D_MODEL].astype(F32))
        term = gate * _dot(y_ref[...], wb_ref[i])
        merged = term if merged is None else merged + term
    out = _dot(merged.astype(BF16), wo_ref[...])
    z = ALPHA * x_ref[...] + g1_ref[...] * out
    o_ref[...] = _layer_norm(z, lg_ref[...], lb_ref[...])


def _merge_call(x, ya, yb, yc, proj, mod5, row0, span, w_branch, w_out, ln_g, ln_b):
    n = x.shape[0]
    tb = TOKEN_BLOCK
    bw = N_HEAD * DV
    tok = lambda w: pl.BlockSpec((tb, w), lambda i: (i, 0))
    return pl.pallas_call(
        _merge_kernel,
        grid=(n // tb,),
        in_specs=[tok(D_MODEL), tok(bw), tok(bw), tok(bw),
                  pl.BlockSpec((tb, N_GATE_COLS), lambda i: (i, 0)),
                  _mod_spec(row0, span, 2),
                  _resident((3, bw, D_MODEL)), _resident((D_MODEL, D_MODEL)),
                  _resident((1, D_MODEL)), _resident((1, D_MODEL))],
        out_specs=tok(D_MODEL),
        out_shape=jax.ShapeDtypeStruct((n, D_MODEL), F32),
        compiler_params=_cparams(("arbitrary",)),
    )(x, ya, yb, yc, proj, mod5, w_branch, w_out, ln_g, ln_b)


def _gelu_tanh(x):
    return 0.5 * x * (1.0 + jnp.tanh(0.7978845608028654 * (x + 0.044715 * (x * x * x))))


def _ffn_kernel(xp_ref, x_ref, xn_ref, sh_ref, sc_ref, g2_ref, wu_ref, cw_ref, cb_ref, wd_ref,
                lg_ref, lb_ref, o_ref, acc_ref, *, seq_len):
    tb = x_ref.shape[0]
    halo = SUBLANES
    i = pl.program_id(0)
    xm = x_ref[...]
    xin = jnp.concatenate([xp_ref[...], xm, xn_ref[...]], axis=0)
    h = (xin * (1.0 + sc_ref[...]) + sh_ref[...]).astype(BF16)
    hm = h[halo:halo + tb]
    posn = (i * tb + lax.broadcasted_iota(jnp.int32, (tb, 1), 0)) % seq_len
    has_prev = posn != 0
    has_next = posn != seq_len - 1
    for c in range(D_FF // FF_CHUNK):
        lo, hi = c * FF_CHUNK, (c + 1) * FF_CHUNK
        a = _dot(h, wu_ref[:, lo:hi])
        b = _dot(hm, wu_ref[:, D_FF + lo:D_FF + hi])
        rows = tb + 2 * halo
        a_prev = jnp.where(has_prev, pltpu.roll(a, 1, 0)[halo:halo + tb], 0.0)
        a_next = jnp.where(has_next, pltpu.roll(a, rows - 1, 0)[halo:halo + tb], 0.0)
        cv = (a_prev * cw_ref[0:1, lo:hi] + a[halo:halo + tb] * cw_ref[1:2, lo:hi]
              + a_next * cw_ref[2:3, lo:hi] + cb_ref[:, lo:hi])
        act = (_gelu_tanh(cv) * b).astype(BF16)
        part = _dot(act, wd_ref[lo:hi, :])
        if c == 0:
            acc_ref[...] = part
        else:
            acc_ref[...] += part
    z = ALPHA * xm + g2_ref[...] * acc_ref[...]
    o_ref[...] = _layer_norm(z, lg_ref[...], lb_ref[...])


def _ffn_call(x, mod5, row0, span, seq_len, w_up, conv_w, conv_b, w_down, ln_g, ln_b):
    n = x.shape[0]
    tb = TOKEN_BLOCK
    hb = tb // SUBLANES
    last = n // SUBLANES - 1
    modspec = functools.partial(_mod_spec, row0, span)

    return pl.pallas_call(
        functools.partial(_ffn_kernel, seq_len=seq_len),
        grid=(n // tb,),
        in_specs=[pl.BlockSpec((SUBLANES, D_MODEL), lambda i: (jnp.maximum(i * hb - 1, 0), 0)),
                  pl.BlockSpec((tb, D_MODEL), lambda i: (i, 0)),
                  pl.BlockSpec((SUBLANES, D_MODEL), lambda i: (jnp.minimum((i + 1) * hb, last), 0)),
                  modspec(3), modspec(4), modspec(5),
                  _resident((D_MODEL, 2 * D_FF)), _resident((3, D_FF)), _resident((1, D_FF)),
                  _resident((D_FF, D_MODEL)), _resident((1, D_MODEL)), _resident((1, D_MODEL))],
        out_specs=pl.BlockSpec((tb, D_MODEL), lambda i: (i, 0)),
        out_shape=jax.ShapeDtypeStruct((n, D_MODEL), F32),
        scratch_shapes=[pltpu.VMEM((tb, D_MODEL), F32)],
        compiler_params=_cparams(("arbitrary",)),
    )(x, x, x, mod5, mod5, mod5, w_up, conv_w, conv_b, w_down, ln_g, ln_b)


def _rope_tables(seq):
    half = DK // 4
    t = np.arange(seq)
    inv = ROPE_BASE ** (-np.arange(half, dtype=np.float64) / half)
    ang_row = (t // GRID_W)[:, None] * inv[None, :]
    ang_col = (t % GRID_W)[:, None] * inv[None, :]
    ang = np.concatenate([ang_row, ang_row, ang_col, ang_col], axis=1)
    first = (np.arange(DK) % (2 * half)) < half
    cos = np.cos(ang)
    sa = np.where(first[None, :], -np.sin(ang), 0.0)
    sb = np.where(first[None, :], 0.0, np.sin(ang))
    tile = lambda a: jnp.asarray(np.concatenate([a, a], axis=1), F32)
    return tile(cos), tile(sa), tile(sb)


def _trunk(x, mod5, row0, seq_len, batch, l, wts, tables, ctx):
    n = x.shape[0]
    emit_kv = ctx is None
    span = n if ctx is None else seq_len
    outs = _proj_call(x, mod5, row0, span, seq_len, wts['w_in'][l], wts['wa1'][l], wts['wa2'][l],
                      wts['ba'][l], tables, emit_kv)
    proj, la = outs[0], outs[1]
    proj3 = proj.reshape(batch, seq_len, D_IN)
    la3 = la.reshape(batch, seq_len, 4 * LANES)
    lam_init = 0.8 - 0.6 * math.exp(-0.3 * l)
    if ctx is None:
        ya, st_ret = _ret_call(proj3, wts['ret_decay'][l], None)
        yb, st_gla = _gla_call(proj3, la3, wts['gla_norm_g'][l], None)
        yc = _diff_call(proj3, proj3, proj3, CB_CK, CB_CV, wts['diff_lam'][l],
                        wts['diff_subln_g'][l], lam_init)
        extra = (outs[2], outs[3], st_ret, st_gla)
    else:
        ya, _ = _ret_call(proj3, wts['ret_decay'][l], ctx['ret'])
        yb, _ = _gla_call(proj3, la3, wts['gla_norm_g'][l], ctx['gla'])
        k_all = jnp.concatenate([ctx['dk'], proj3[:, :, CB_CK * LANES:(CB_CK + 4) * LANES]], axis=1)
        v_all = jnp.concatenate([ctx['dv'], proj3[:, :, CB_CV * LANES:(CB_CV + 4) * LANES]], axis=1)
        yc = _diff_call(proj3, k_all, v_all, 0, 0, wts['diff_lam'][l], wts['diff_subln_g'][l], lam_init)
        extra = None
    bw = N_HEAD * DV
    x1 = _merge_call(x, ya.reshape(n, bw), yb.reshape(n, bw), yc.reshape(n, bw), proj, mod5, row0,
                     span, wts['w_branch'][l], wts['w_out'][l], wts['ln_g'][l, 0:1], wts['ln_b'][l, 0:1])
    x2 = _ffn_call(x1, mod5, row0, span, seq_len, wts['w_up'][l], wts['conv_w'][l], wts['conv_b'][l],
                   wts['w_down'][l], wts['ln_g'][l, 1:2], wts['ln_b'][l, 1:2])
    return x2, extra


def kernel(x_prompt, x_sample, cache_diff_k, cache_diff_v, state_ret, state_gla, c, c_ctx,
           ada_w, ada_b, w_in, ret_decay, gla_wa1, gla_wa2, gla_ba, gla_norm_g, diff_lam,
           diff_subln_g, w_branch, w_out, ln_g, ln_b, ffn_w_up, ffn_conv_w, ffn_conv_b, ffn_w_down):
    bp, tp, _ = x_prompt.shape
    bs, ts, _ = x_sample.shape
    past = cache_diff_k.shape[3]

    cvec = jnp.concatenate([c_ctx[None, :], c, jnp.zeros((SUBLANES - 1 - bs, D_MODEL), F32)], axis=0)
    mod = _ada_call(cvec, ada_w, ada_b).reshape(DEPTH, SUBLANES, 6, 1, D_MODEL)

    wa1 = jnp.concatenate([gla_wa1[:, 0], gla_wa1[:, 1]], axis=-1)
    wa1 = jnp.pad(wa1, ((0, 0), (0, 0), (0, LANES - 2 * GLA_RANK))).astype(BF16)
    wa2 = jnp.zeros((DEPTH, LANES, 4 * LANES), F32)
    wa2 = wa2.at[:, 0:GLA_RANK, 0:2 * LANES].set(gla_wa2[:, 0])
    wa2 = wa2.at[:, GLA_RANK:2 * GLA_RANK, 2 * LANES:].set(gla_wa2[:, 1]).astype(BF16)
    gate0 = D_IN - N_GATE_COLS
    w_in_perm = jnp.concatenate([w_in[:, :, gate0:], w_in[:, :, :gate0]], axis=-1)
    wts = {
        'w_in': w_in_perm.astype(BF16), 'wa1': wa1, 'wa2': wa2,
        'ba': gla_ba.reshape(DEPTH, 1, 4 * LANES), 'ret_decay': ret_decay,
        'gla_norm_g': gla_norm_g.reshape(DEPTH, 1, DV), 'diff_lam': diff_lam,
        'diff_subln_g': diff_subln_g.reshape(DEPTH, 1, DV),
        'w_branch': w_branch.astype(BF16), 'w_out': w_out.astype(BF16),
        'ln_g': ln_g, 'ln_b': ln_b, 'w_up': ffn_w_up.astype(BF16), 'conv_w': ffn_conv_w,
        'conv_b': ffn_conv_b.reshape(DEPTH, 1, D_FF), 'w_down': ffn_w_down.astype(BF16),
    }

    h = x_prompt.reshape(bp * tp, D_MODEL)
    dks, dvs, rets, glas = [], [], [], []
    for l in range(DEPTH):
        h, (ck, cv, st_ret, st_gla) = _trunk(h, mod[l], 0, tp, bp, l, wts, None, None)
        dks.append(ck.reshape(bp, tp, 2 * N_HEAD, DK).transpose(0, 2, 1, 3))
        dvs.append(cv.reshape(bp, tp, N_HEAD, DV).transpose(0, 2, 1, 3))
        rets.append(st_ret)
        glas.append(st_gla)
    y_prompt = h.reshape(bp, tp, D_MODEL)

    tables = _rope_tables(ts)
    z = x_sample.reshape(bs * ts, D_MODEL)
    for l in range(DEPTH):
        ctx = {
            'dk': cache_diff_k[:, l].transpose(0, 2, 1, 3).reshape(bs, past, 2 * N_HEAD * DK).astype(BF16),
            'dv': cache_diff_v[:, l].transpose(0, 2, 1, 3).reshape(bs, past, N_HEAD * DV).astype(BF16),
            'ret': state_ret[:, l], 'gla': state_gla[:, l],
        }
        z, _ = _trunk(z, mod[l], 1, ts, bs, l, wts, tables, ctx)
    y_sample = z.reshape(bs, ts, D_MODEL)

    return (y_prompt, y_sample, jnp.stack(dks, axis=1), jnp.stack(dvs, axis=1),
            jnp.stack(rets, axis=1), jnp.stack(glas, axis=1))
```

```python
import functools
import math

import numpy as np
import jax
import jax.numpy as jnp
from jax import lax
from jax.experimental import pallas as pl
from jax.experimental.pallas import tpu as pltpu

F32 = jnp.float32
BF16 = jnp.bfloat16

D_MODEL = 1024
DEPTH = 2
GRID_W = 64
N_HEAD = 4
DK = 64
DV = 128
GLA_RANK = 16
GLA_TAU = 16.0
D_FF = 2816
ROPE_BASE = 10000.0
ALPHA = (2 * DEPTH) ** 0.25
EPS = 1e-5
D_IN = 7680

LANES = 128
SUBLANES = 8
VMEM_LIMIT = 56 * 1024 * 1024

N_GATE_COLS = 3 * D_MODEL
CB_M = 0
CB_AQ, CB_AK, CB_AV, CB_AG = 24, 26, 28, 32
CB_BQ, CB_BK, CB_BV, CB_BR = 36, 38, 40, 44
CB_CQ, CB_CK, CB_CV = 48, 52, 56
ROPE_BLOCKS = (24, 25, 26, 27, 48, 49, 50, 51, 52, 53, 54, 55)
SCALED_BLOCKS = (26, 27, 36, 37)
SOFTMAX_Q_BLOCKS = (48, 49, 50, 51)
QK_SCALE = DK ** -0.5
LOG2E = math.log2(math.e)

TOKEN_BLOCK = 512
RET_CHUNK = 256
GLA_BLOCK = 128
Q_BLOCK = 256
DIFF_KEY_TILE = 512
FF_CHUNK = 256


def _cparams(sem):
    return pltpu.CompilerParams(dimension_semantics=sem, vmem_limit_bytes=VMEM_LIMIT)


def _resident(shape):
    zeros = (0,) * len(shape)
    return pl.BlockSpec(shape, lambda *_: zeros, pipeline_mode=pl.Buffered(1))


def _mod_spec(row0, span, k):
    return pl.BlockSpec((None, None, 1, D_MODEL),
                        lambda i: (row0 + (i * TOKEN_BLOCK) // span, k, 0, 0))


def _sigmoid(x):
    return 1.0 / (1.0 + jnp.exp(-x))


def _log_sigmoid(x):
    return jnp.minimum(x, 0.0) - jnp.log(1.0 + jnp.exp(-jnp.abs(x)))


def _dot(a, b):
    return jnp.dot(a, b, preferred_element_type=F32)


def _dot_nt(a, b):
    return lax.dot_general(a, b, (((1,), (1,)), ((), ())), preferred_element_type=F32)


def _dot_tn(a, b):
    return lax.dot_general(a, b, (((0,), (0,)), ((), ())), preferred_element_type=F32)


def _ada_kernel(c_ref, w_ref, b_ref, o_ref):
    c = c_ref[...]
    s = (c * _sigmoid(c)).astype(BF16)
    o_ref[...] = _dot(s, w_ref[...].astype(BF16)) + b_ref[...]


def _ada_call(cvec, ada_w, ada_b):
    nt = 1536
    return pl.pallas_call(
        _ada_kernel,
        grid=(DEPTH, 6 * D_MODEL // nt),
        in_specs=[pl.BlockSpec((SUBLANES, D_MODEL), lambda l, j: (0, 0)),
                  pl.BlockSpec((None, D_MODEL, nt), lambda l, j: (l, 0, j)),
                  pl.BlockSpec((None, 1, nt), lambda l, j: (l, 0, j))],
        out_specs=pl.BlockSpec((None, SUBLANES, nt), lambda l, j: (l, 0, j)),
        out_shape=jax.ShapeDtypeStruct((DEPTH, SUBLANES, 6 * D_MODEL), F32),
        compiler_params=_cparams(("arbitrary", "arbitrary")),
    )(cvec, ada_w, ada_b.reshape(DEPTH, 1, 6 * D_MODEL))


def _proj_kernel(*refs, rope, emit_kv):
    x_ref, sh_ref, sc_ref, w_ref, wa1_ref, wa2_ref, ba_ref = refs[:7]
    pos = 7
    if rope:
        cos_ref, sa_ref, sb_ref = refs[pos:pos + 3]
        pos += 3
    proj_ref, la_ref = refs[pos:pos + 2]
    pos += 2
    if emit_kv:
        ck_ref, cv_ref = refs[pos:pos + 2]

    h = (x_ref[...] * (1.0 + sc_ref[...]) + sh_ref[...]).astype(BF16)
    tile = 512
    per = tile // LANES
    for j in range(D_IN // tile):
        acc = _dot(h, w_ref[:, j * tile:(j + 1) * tile])
        for i in range(per):
            blk = j * per + i
            y = acc[:, i * LANES:(i + 1) * LANES]
            if emit_kv and CB_CK <= blk < CB_CK + 4:
                ck_ref[:, (blk - CB_CK) * LANES:(blk - CB_CK + 1) * LANES] = y
            if emit_kv and CB_CV <= blk < CB_CV + 4:
                cv_ref[:, (blk - CB_CV) * LANES:(blk - CB_CV + 1) * LANES] = y
            if rope and blk in ROPE_BLOCKS:
                y = (y * cos_ref[...] + pltpu.roll(y, LANES - 16, 1) * sa_ref[...]
                     + pltpu.roll(y, 16, 1) * sb_ref[...])
            if blk in SCALED_BLOCKS:
                y = y * QK_SCALE
            if blk in SOFTMAX_Q_BLOCKS:
                y = y * (QK_SCALE * LOG2E)
            proj_ref[:, blk * LANES:(blk + 1) * LANES] = y.astype(BF16)

    r = _dot(h, wa1_ref[...]).astype(BF16)
    z = _dot(r, wa2_ref[...]) + ba_ref[...]
    la_ref[...] = _log_sigmoid(z) * (1.0 / GLA_TAU)


def _proj_call(x, mod5, row0, span, seq_len, w_in, wa1, wa2, ba, tables, emit_kv):
    n = x.shape[0]
    tb = TOKEN_BLOCK
    bps = max(seq_len // tb, 1)
    rope = tables is not None

    in_specs = [pl.BlockSpec((tb, D_MODEL), lambda i: (i, 0)),
                _mod_spec(row0, span, 0), _mod_spec(row0, span, 1),
                _resident((D_MODEL, D_IN)), _resident((D_MODEL, LANES)),
                _resident((LANES, 4 * LANES)), _resident((1, 4 * LANES))]
    args = [x, mod5, mod5, w_in, wa1, wa2, ba]
    if rope:
        in_specs += [pl.BlockSpec((tb, LANES), lambda i: (i % bps, 0))] * 3
        args += list(tables)
    out_specs = [pl.BlockSpec((tb, D_IN), lambda i: (i, 0)),
                 pl.BlockSpec((tb, 4 * LANES), lambda i: (i, 0))]
    out_shape = [jax.ShapeDtypeStruct((n, D_IN), BF16), jax.ShapeDtypeStruct((n, 4 * LANES), F32)]
    if emit_kv:
        out_specs += [pl.BlockSpec((tb, 4 * LANES), lambda i: (i, 0))] * 2
        out_shape += [jax.ShapeDtypeStruct((n, 4 * LANES), F32)] * 2
    return pl.pallas_call(
        functools.partial(_proj_kernel, rope=rope, emit_kv=emit_kv),
        grid=(n // tb,), in_specs=in_specs, out_specs=out_specs, out_shape=out_shape,
        compiler_params=_cparams(("arbitrary",)),
    )(*args)


def _ret_kernel(*refs, seq, chunk, has_ctx):
    dec_ref, q_ref, k_ref, v_ref, g_ref = refs[:5]
    pos = 5
    if has_ctx:
        s0_ref = refs[pos]
        pos += 1
    y_ref, st_ref, of_ref, ob_ref, s_ref, dm_ref, eq_ref, ek_ref, gc_ref = refs[pos:]
    p = pl.program_id(1)
    nc = seq // chunk
    c_f = float(chunk)
    lane = lax.broadcasted_iota(jnp.int32, (1, LANES), 1)
    hmask = [lane < DK, lane >= DK]
    rowp = lax.broadcasted_iota(jnp.int32, (chunk, 1), 0).astype(F32)
    colp = lax.broadcasted_iota(jnp.int32, (1, chunk), 1).astype(F32)
    diff = rowp - colp

    for d in range(2):
        for hh in range(2):
            ci = 2 * d + hh
            raw = dec_ref[d, 2 * p + hh]
            lg_l = _log_sigmoid(jnp.full((1, LANES), raw, F32))
            lg_c = _log_sigmoid(jnp.full((1, chunk), raw, F32))
            if d == 0:
                dm_ref[ci] = jnp.where(diff >= 0, jnp.exp(jnp.maximum(diff, 0.0) * lg_c), 0.0)
                eq_ref[ci] = jnp.exp((rowp + 1.0) * lg_l)
                ek_ref[ci] = jnp.exp((c_f - 1.0 - rowp) * lg_l)
            else:
                dm_ref[ci] = jnp.where(diff <= 0, jnp.exp(jnp.maximum(-diff, 0.0) * lg_c), 0.0)
                eq_ref[ci] = jnp.exp((c_f - rowp) * lg_l)
                ek_ref[ci] = jnp.exp(rowp * lg_l)
            gc_ref[ci] = jnp.exp(c_f * lg_l)
            if has_ctx:
                s0 = s0_ref[d, hh]
                zero = jnp.zeros((DK, DV), F32)
                s_ref[ci] = jnp.concatenate([s0, zero] if hh == 0 else [zero, s0], axis=0)
            else:
                s_ref[ci] = jnp.zeros((LANES, DV), F32)

    def step(n, carry):
        for d in range(2):
            c = n if d == 0 else nc - 1 - n
            r0 = pl.multiple_of(c * chunk, chunk)
            qc = q_ref[pl.ds(r0, chunk), :]
            kc = k_ref[pl.ds(r0, chunk), :]
            o_ref = of_ref if d == 0 else ob_ref
            for hh in range(2):
                ci = 2 * d + hh
                qh = jnp.where(hmask[hh], qc, jnp.zeros_like(qc))
                kh = jnp.where(hmask[hh], kc, jnp.zeros_like(kc))
                vh = v_ref[pl.ds(r0, chunk), hh * DV:(hh + 1) * DV]
                sc = _dot_nt(qh, kh) * dm_ref[ci]
                intra = _dot(sc.astype(BF16), vh)
                sb = s_ref[ci]
                qd = (qh.astype(F32) * eq_ref[ci]).astype(BF16)
                inter = _dot(qd, sb.astype(BF16))
                o_ref[hh, pl.ds(r0, chunk), :] = intra + inter
                kd = (kh.astype(F32) * ek_ref[ci]).astype(BF16)
                s_ref[ci] = gc_ref[ci] * sb + _dot_tn(kd, vh)
        return carry

    lax.fori_loop(0, nc, step, 0)

    for d in range(2):
        for hh in range(2):
            st_ref[d, hh] = s_ref[2 * d + hh][hh * DK:(hh + 1) * DK, :]

    def finish(n, carry):
        r0 = pl.multiple_of(n * chunk, chunk)
        for hh in range(2):
            o = of_ref[hh, pl.ds(r0, chunk), :] + ob_ref[hh, pl.ds(r0, chunk), :]
            mu = jnp.mean(o, axis=-1, keepdims=True)
            oc = o - mu
            var = jnp.mean(oc * oc, axis=-1, keepdims=True)
            g = g_ref[pl.ds(r0, chunk), hh * DV:(hh + 1) * DV].astype(F32)
            y = oc * lax.rsqrt(var + EPS) * (g * _sigmoid(g))
            y_ref[pl.ds(r0, chunk), hh * DV:(hh + 1) * DV] = y.astype(BF16)
        return carry

    lax.fori_loop(0, nc, finish, 0)


def _ret_call(proj3, decay, s0):
    b, t, _ = proj3.shape
    chunk = min(RET_CHUNK, t)
    has_ctx = s0 is not None
    st_spec = pl.BlockSpec((None, 2, 2, DK, DV), lambda i, p: (i, 0, p, 0, 0))
    in_specs = [pl.BlockSpec(memory_space=pltpu.SMEM),
                pl.BlockSpec((None, t, LANES), lambda i, p: (i, 0, CB_AQ + p)),
                pl.BlockSpec((None, t, LANES), lambda i, p: (i, 0, CB_AK + p)),
                pl.BlockSpec((None, t, 2 * DV), lambda i, p: (i, 0, CB_AV // 2 + p)),
                pl.BlockSpec((None, t, 2 * DV), lambda i, p: (i, 0, CB_AG // 2 + p))]
    args = [decay, proj3, proj3, proj3, proj3]
    if has_ctx:
        in_specs.append(st_spec)
        args.append(s0)
    return pl.pallas_call(
        functools.partial(_ret_kernel, seq=t, chunk=chunk, has_ctx=has_ctx),
        grid=(b, 2), in_specs=in_specs,
        out_specs=[pl.BlockSpec((None, t, 2 * DV), lambda i, p: (i, 0, p)), st_spec],
        out_shape=[jax.ShapeDtypeStruct((b, t, N_HEAD * DV), BF16),
                   jax.ShapeDtypeStruct((b, 2, N_HEAD, DK, DV), F32)],
        scratch_shapes=[pltpu.VMEM((2, t, DV), F32), pltpu.VMEM((2, t, DV), F32),
                        pltpu.VMEM((4, LANES, DV), F32), pltpu.VMEM((4, chunk, chunk), F32),
                        pltpu.VMEM((4, chunk, LANES), F32), pltpu.VMEM((4, chunk, LANES), F32),
                        pltpu.VMEM((4, 1, LANES), F32)],
        compiler_params=_cparams(("arbitrary", "arbitrary")),
    )(*args)


def _gla_block(q, k, x, v_pair, st_list, hmask, blk, reverse):
    row = lax.broadcasted_iota(jnp.int32, (blk, 1), 0)
    xr = (lax.broadcasted_iota(jnp.int32, (blk, blk), 0)
          ^ lax.broadcasted_iota(jnp.int32, (blk, blk), 1))
    w = x
    tot = x
    qb = q.astype(BF16)
    kb = k.astype(BF16)
    amat = []
    for hh in range(2):
        qh = jnp.where(hmask[hh], qb, jnp.zeros_like(qb))
        amat.append(jnp.where(xr == 0, _dot_nt(qh, kb), 0.0))
    h = 1
    while h < blk:
        up = (row & h) != 0
        qside = jnp.logical_not(up) if reverse else up
        f = jnp.exp(-jnp.where(qside, w, tot - w))
        ql = jnp.where(qside, q * f, 0.0).astype(BF16)
        kl = jnp.where(qside, 0.0, k * f).astype(BF16)
        for hh in range(2):
            qh = jnp.where(hmask[hh], ql, jnp.zeros_like(ql))
            amat[hh] = amat[hh] + jnp.where(xr < 2 * h, _dot_nt(qh, kl), 0.0)
        partner = jnp.where(up, pltpu.roll(tot, h, 0), pltpu.roll(tot, blk - h, 0))
        w = w + jnp.where(qside, partner, 0.0)
        tot = tot + partner
        h *= 2
    qd = (q * jnp.exp(-w)).astype(BF16)
    kd = (k * jnp.exp(-(tot - w))).astype(BF16)
    dec = jnp.exp(-tot[0:1, :])
    outs, new_st = [], []
    for hh in range(2):
        vh = v_pair[:, hh * DV:(hh + 1) * DV]
        st = st_list[hh]
        qh = jnp.where(hmask[hh], qd, jnp.zeros_like(qd))
        kh = jnp.where(hmask[hh], kd, jnp.zeros_like(kd))
        o = _dot(amat[hh].astype(BF16), vh) + _dot_nt(qh, st.astype(BF16))
        outs.append(o)
        new_st.append(dec * st + _dot_tn(vh, kh))
    return outs, new_st


def _gla_kernel(*refs, seq, blk, has_ctx):
    q_ref, k_ref, v_ref, r_ref, laf_ref, lab_ref, ng_ref = refs[:7]
    pos = 7
    if has_ctx:
        s0_ref = refs[pos]
        pos += 1
    y_ref, st_ref, of_ref, ob_ref, s_ref = refs[pos:]
    nb = seq // blk
    lane = lax.broadcasted_iota(jnp.int32, (1, LANES), 1)
    hmask = [lane < DK, lane >= DK]

    for d in range(2):
        for hh in range(2):
            if has_ctx:
                zero = jnp.zeros((DK, DV), F32)
                s0 = s0_ref[d, hh]
                full = jnp.concatenate([s0, zero] if hh == 0 else [zero, s0], axis=0)
                s_ref[2 * d + hh] = full.T
            else:
                s_ref[2 * d + hh] = jnp.zeros((DV, LANES), F32)

    def step(n, carry):
        for d in range(2):
            c = n if d == 0 else nb - 1 - n
            r0 = pl.multiple_of(c * blk, blk)
            q = q_ref[pl.ds(r0, blk), :].astype(F32)
            k = k_ref[pl.ds(r0, blk), :].astype(F32)
            la_ref = laf_ref if d == 0 else lab_ref
            x = -la_ref[pl.ds(r0, blk), :]
            vp = v_ref[pl.ds(r0, blk), :]
            sts = [s_ref[2 * d], s_ref[2 * d + 1]]
            outs, new_st = _gla_block(q, k, x, vp, sts, hmask, blk, d == 1)
            o_ref = of_ref if d == 0 else ob_ref
            for hh in range(2):
                o_ref[hh, pl.ds(r0, blk), :] = outs[hh]
                s_ref[2 * d + hh] = new_st[hh]
        return carry

    lax.fori_loop(0, nb, step, 0)

    for d in range(2):
        for hh in range(2):
            st_ref[d, hh] = s_ref[2 * d + hh][...].T[hh * DK:(hh + 1) * DK, :]

    fin = min(seq, 512)

    def finish(n, carry):
        r0 = pl.multiple_of(n * fin, fin)
        for hh in range(2):
            o = of_ref[hh, pl.ds(r0, fin), :] + ob_ref[hh, pl.ds(r0, fin), :]
            ms = jnp.mean(o * o, axis=-1, keepdims=True)
            g = r_ref[pl.ds(r0, fin), hh * DV:(hh + 1) * DV].astype(F32)
            y = o * lax.rsqrt(ms + EPS) * ng_ref[...] * (g * _sigmoid(g))
            y_ref[pl.ds(r0, fin), hh * DV:(hh + 1) * DV] = y.astype(BF16)
        return carry

    lax.fori_loop(0, seq // fin, finish, 0)


def _gla_call(proj3, la3, norm_g, s0):
    b, t, _ = proj3.shape
    blk = GLA_BLOCK
    has_ctx = s0 is not None
    st_spec = pl.BlockSpec((None, 2, 2, DK, DV), lambda i, p: (i, 0, p, 0, 0))
    in_specs = [pl.BlockSpec((None, t, LANES), lambda i, p: (i, 0, CB_BQ + p)),
                pl.BlockSpec((None, t, LANES), lambda i, p: (i, 0, CB_BK + p)),
                pl.BlockSpec((None, t, 2 * DV), lambda i, p: (i, 0, CB_BV // 2 + p)),
                pl.BlockSpec((None, t, 2 * DV), lambda i, p: (i, 0, CB_BR // 2 + p)),
                pl.BlockSpec((None, t, LANES), lambda i, p: (i, 0, p)),
                pl.BlockSpec((None, t, LANES), lambda i, p: (i, 0, 2 + p)),
                pl.BlockSpec((1, DV), lambda i, p: (0, 0))]
    args = [proj3, proj3, proj3, proj3, la3, la3, norm_g]
    if has_ctx:
        in_specs.append(st_spec)
        args.append(s0)
    return pl.pallas_call(
        functools.partial(_gla_kernel, seq=t, blk=blk, has_ctx=has_ctx),
        grid=(b, 2), in_specs=in_specs,
        out_specs=[pl.BlockSpec((None, t, 2 * DV), lambda i, p: (i, 0, p)), st_spec],
        out_shape=[jax.ShapeDtypeStruct((b, t, N_HEAD * DV), BF16),
                   jax.ShapeDtypeStruct((b, 2, N_HEAD, DK, DV), F32)],
        scratch_shapes=[pltpu.VMEM((2, t, DV), F32), pltpu.VMEM((2, t, DV), F32),
                        pltpu.VMEM((4, DV, LANES), F32)],
        compiler_params=_cparams(("arbitrary", "arbitrary")),
    )(*args)


def _diff_kernel(*refs, lam_init, n_cache, n_new):
    lam_ref, q_ref = refs[:2]
    pos = 2
    if n_cache:
        kc_ref, vc_ref = refs[pos:pos + 2]
        pos += 2
    kn_ref, vn_ref, sg_ref, y_ref = refs[pos:pos + 4]
    pos += 4
    if n_cache:
        vtc_ref = refs[pos]
        pos += 1
    vtn_ref, sa_ref, sb_ref, acc_ref = refs[pos:]
    qb = q_ref.shape[0]
    kt = min(n_new, DIFF_KEY_TILE)
    n_tiles = n_new // kt

    @pl.when(pl.program_id(2) == 0)
    def _():
        if n_cache:
            vtc_ref[...] = vc_ref[...].T
        for r in range(n_tiles):
            vtn_ref[r] = vn_ref[r * kt:(r + 1) * kt, :].T

    lp = lam_ref[...]
    lam = (jnp.exp(jnp.sum(lp[0:1] * lp[1:2], axis=-1, keepdims=True))
           - jnp.exp(jnp.sum(lp[2:3] * lp[3:4], axis=-1, keepdims=True)) + lam_init)
    lane = lax.broadcasted_iota(jnp.int32, (1, LANES), 1)
    q = q_ref[...]
    qs = [jnp.where((lane < DK) if i == 0 else (lane >= DK), q, jnp.zeros_like(q)) for i in range(2)]

    s_bufs = (sa_ref, sb_ref)
    new0 = 1 if n_cache else 0
    last = new0 + n_tiles - 1

    def tile_rows(idx):
        return n_cache if idx < new0 else kt

    def scores(idx, r=None):
        if idx < new0:
            k_tile = kc_ref[...]
        elif r is None:
            k_tile = kn_ref[(idx - new0) * kt:(idx - new0 + 1) * kt, :]
        else:
            k_tile = kn_ref[pl.ds(pl.multiple_of(r * kt, kt), kt), :]
        for i in range(2):
            s_bufs[idx % 2][i, 0:tile_rows(idx), :] = _dot_nt(k_tile, qs[i])

    def soft_pv(idx, m, l, r=None):
        n = tile_rows(idx)
        if idx < new0:
            vt_tile = vtc_ref[...]
        else:
            vt_tile = vtn_ref[idx - new0 if r is None else r]
        m_out, l_out = [], []
        for i in range(2):
            s = s_bufs[idx % 2][i, 0:n, :]
            m_new = jnp.maximum(m[i], jnp.max(s, axis=0, keepdims=True))
            e = jnp.exp2(s - m_new)
            alpha = jnp.exp2(m[i] - m_new)
            l_out.append(l[i] * alpha + jnp.sum(e, axis=0, keepdims=True))
            m_out.append(m_new)
            pv = _dot(vt_tile, e.astype(BF16))
            acc_ref[i] = pv if idx == 0 else acc_ref[i] * alpha + pv
        return m_out, l_out

    m = [jnp.full((1, qb), -1e30, F32) for _ in range(2)]
    l = [jnp.zeros((1, qb), F32) for _ in range(2)]
    scores(0)
    if last >= 1:
        scores(1)
    m, l = soft_pv(0, m, l)
    n_pairs = max(last - 1, 0) // 2
    loop_end = 1 + 2 * n_pairs

    def pair(p, carry):
        m = [carry[0], carry[1]]
        l = [carry[2], carry[3]]
        r = 2 * p + (1 - new0)
        scores(2, r + 1)
        m, l = soft_pv(1, m, l, r)
        scores(3, r + 2)
        m, l = soft_pv(2, m, l, r + 1)
        return (m[0], m[1], l[0], l[1])

    if n_pairs:
        m0, m1, l0, l1 = lax.fori_loop(0, n_pairs, pair, (m[0], m[1], l[0], l[1]))
        m, l = [m0, m1], [l0, l1]
    for idx in range(loop_end, last + 1):
        if idx + 1 <= last:
            scores(idx + 1)
        m, l = soft_pv(idx, m, l)

    o = acc_ref[0] * (1.0 / l[0]) - acc_ref[1] * (lam / l[1])
    ms = jnp.mean(o * o, axis=0, keepdims=True)
    y = o * lax.rsqrt(ms + EPS) * (sg_ref[...] * (1.0 - lam_init))
    y_ref[...] = y.T.astype(BF16)


def _diff_call(proj3, cache_k, cache_v, lam_p, subln_col, lam_init):
    b, t, _ = proj3.shape
    n_cache = 0 if cache_k is None else cache_k.shape[1]
    qb = min(Q_BLOCK, t)
    kt = min(t, DIFF_KEY_TILE)
    in_specs = [pl.BlockSpec((4, DK), lambda i, h, j: (0, 0)),
                pl.BlockSpec((None, qb, LANES), lambda i, h, j: (i, j, CB_CQ + h))]
    args = [lam_p, proj3]
    if n_cache:
        in_specs += [pl.BlockSpec((None, n_cache, LANES), lambda i, h, j: (i, 0, h)),
                     pl.BlockSpec((None, n_cache, DV), lambda i, h, j: (i, 0, h))]
        args += [cache_k, cache_v]
    in_specs += [pl.BlockSpec((None, t, LANES), lambda i, h, j: (i, 0, CB_CK + h)),
                 pl.BlockSpec((None, t, DV), lambda i, h, j: (i, 0, CB_CV + h)),
                 pl.BlockSpec((DV, 1), lambda i, h, j: (0, 0))]
    args += [proj3, proj3, subln_col]
    return pl.pallas_call(
        functools.partial(_diff_kernel, lam_init=lam_init, n_cache=n_cache, n_new=t),
        grid=(b, N_HEAD, t // qb), in_specs=in_specs,
        out_specs=pl.BlockSpec((None, qb, DV), lambda i, h, j: (i, j, h)),
        out_shape=jax.ShapeDtypeStruct((b, t, N_HEAD * DV), BF16),
        scratch_shapes=([pltpu.VMEM((DV, n_cache), BF16)] if n_cache else [])
        + [pltpu.VMEM((t // kt, DV, kt), BF16), pltpu.VMEM((2, max(kt, n_cache), qb), F32),
           pltpu.VMEM((2, max(kt, n_cache), qb), F32), pltpu.VMEM((2, DV, qb), F32)],
        compiler_params=_cparams(("arbitrary", "arbitrary", "arbitrary")),
    )(*args)


def _layer_norm(z, g, b):
    mu = jnp.mean(z, axis=-1, keepdims=True)
    zc = z - mu
    var = jnp.mean(zc * zc, axis=-1, keepdims=True)
    return zc * lax.rsqrt(var + EPS) * g + b


def _merge_kernel(x_ref, ya_ref, yb_ref, yc_ref, m_ref, g1_ref, wb_ref, wo_ref, lg_ref, lb_ref, o_ref):
    merged = None
    for i, y_ref in enumerate((ya_ref, yb_ref, yc_ref)):
        gate = _sigmoid(m_ref[:, i * D_MODEL:(i + 1) * D_MODEL].astype(F32))
        term = gate * _dot(y_ref[...], wb_ref[i])
        merged = term if merged is None else merged + term
    out = _dot(merged.astype(BF16), wo_ref[...])
    z = ALPHA * x_ref[...] + g1_ref[...] * out
    o_ref[...] = _layer_norm(z, lg_ref[...], lb_ref[...])


def _merge_call(x, ya, yb, yc, proj, mod5, row0, span, w_branch, w_out, ln_g, ln_b):
    n = x.shape[0]
    tb = TOKEN_BLOCK
    bw = N_HEAD * DV
    tok = lambda w: pl.BlockSpec((tb, w), lambda i: (i, 0))
    return pl.pallas_call(
        _merge_kernel,
        grid=(n // tb,),
        in_specs=[tok(D_MODEL), tok(bw), tok(bw), tok(bw),
                  pl.BlockSpec((tb, N_GATE_COLS), lambda i: (i, 0)),
                  _mod_spec(row0, span, 2),
                  _resident((3, bw, D_MODEL)), _resident((D_MODEL, D_MODEL)),
                  _resident((1, D_MODEL)), _resident((1, D_MODEL))],
        out_specs=tok(D_MODEL),
        out_shape=jax.ShapeDtypeStruct((n, D_MODEL), F32),
        compiler_params=_cparams(("arbitrary",)),
    )(x, ya, yb, yc, proj, mod5, w_branch, w_out, ln_g, ln_b)


def _gelu_tanh(x):
    return 0.5 * x * (1.0 + jnp.tanh(0.7978845608028654 * (x + 0.044715 * (x * x * x))))


def _ffn_kernel(xp_ref, x_ref, xn_ref, sh_ref, sc_ref, g2_ref, wu_ref, cw_ref, cb_ref, wd_ref,
                lg_ref, lb_ref, o_ref, acc_ref, *, seq_len):
    tb = x_ref.shape[0]
    halo = SUBLANES
    i = pl.program_id(0)
    xm = x_ref[...]
    xin = jnp.concatenate([xp_ref[...], xm, xn_ref[...]], axis=0)
    h = (xin * (1.0 + sc_ref[...]) + sh_ref[...]).astype(BF16)
    hm = h[halo:halo + tb]
    posn = (i * tb + lax.broadcasted_iota(jnp.int32, (tb, 1), 0)) % seq_len
    has_prev = posn != 0
    has_next = posn != seq_len - 1
    for c in range(D_FF // FF_CHUNK):
        lo, hi = c * FF_CHUNK, (c + 1) * FF_CHUNK
        a = _dot(h, wu_ref[:, lo:hi])
        b = _dot(hm, wu_ref[:, D_FF + lo:D_FF + hi])
        rows = tb + 2 * halo
        a_prev = jnp.where(has_prev, pltpu.roll(a, 1, 0)[halo:halo + tb], 0.0)
        a_next = jnp.where(has_next, pltpu.roll(a, rows - 1, 0)[halo:halo + tb], 0.0)
        cv = (a_prev * cw_ref[0:1, lo:hi] + a[halo:halo + tb] * cw_ref[1:2, lo:hi]
              + a_next * cw_ref[2:3, lo:hi] + cb_ref[:, lo:hi])
        act = (_gelu_tanh(cv) * b).astype(BF16)
        part = _dot(act, wd_ref[lo:hi, :])
        if c == 0:
            acc_ref[...] = part
        else:
            acc_ref[...] += part
    z = ALPHA * xm + g2_ref[...] * acc_ref[...]
    o_ref[...] = _layer_norm(z, lg_ref[...], lb_ref[...])


def _ffn_call(x, mod5, row0, span, seq_len, w_up, conv_w, conv_b, w_down, ln_g, ln_b):
    n = x.shape[0]
    tb = TOKEN_BLOCK
    hb = tb // SUBLANES
    last = n // SUBLANES - 1
    modspec = functools.partial(_mod_spec, row0, span)

    return pl.pallas_call(
        functools.partial(_ffn_kernel, seq_len=seq_len),
        grid=(n // tb,),
        in_specs=[pl.BlockSpec((SUBLANES, D_MODEL), lambda i: (jnp.maximum(i * hb - 1, 0), 0)),
                  pl.BlockSpec((tb, D_MODEL), lambda i: (i, 0)),
                  pl.BlockSpec((SUBLANES, D_MODEL), lambda i: (jnp.minimum((i + 1) * hb, last), 0)),
                  modspec(3), modspec(4), modspec(5),
                  _resident((D_MODEL, 2 * D_FF)), _resident((3, D_FF)), _resident((1, D_FF)),
                  _resident((D_FF, D_MODEL)), _resident((1, D_MODEL)), _resident((1, D_MODEL))],
        out_specs=pl.BlockSpec((tb, D_MODEL), lambda i: (i, 0)),
        out_shape=jax.ShapeDtypeStruct((n, D_MODEL), F32),
        scratch_shapes=[pltpu.VMEM((tb, D_MODEL), F32)],
        compiler_params=_cparams(("arbitrary",)),
    )(x, x, x, mod5, mod5, mod5, w_up, conv_w, conv_b, w_down, ln_g, ln_b)


def _rope_tables(seq):
    half = DK // 4
    t = np.arange(seq)
    inv = ROPE_BASE ** (-np.arange(half, dtype=np.float64) / half)
    ang_row = (t // GRID_W)[:, None] * inv[None, :]
    ang_col = (t % GRID_W)[:, None] * inv[None, :]
    ang = np.concatenate([ang_row, ang_row, ang_col, ang_col], axis=1)
    first = (np.arange(DK) % (2 * half)) < half
    cos = np.cos(ang)
    sa = np.where(first[None, :], -np.sin(ang), 0.0)
    sb = np.where(first[None, :], 0.0, np.sin(ang))
    tile = lambda a: jnp.asarray(np.concatenate([a, a], axis=1), F32)
    return tile(cos), tile(sa), tile(sb)


def _trunk(x, mod5, row0, seq_len, batch, l, wts, tables, ctx):
    n = x.shape[0]
    emit_kv = ctx is None
    span = n if ctx is None else seq_len
    outs = _proj_call(x, mod5, row0, span, seq_len, wts['w_in'][l], wts['wa1'][l], wts['wa2'][l],
                      wts['ba'][l], tables, emit_kv)
    proj, la = outs[0], outs[1]
    proj3 = proj.reshape(batch, seq_len, D_IN)
    la3 = la.reshape(batch, seq_len, 4 * LANES)
    lam_init = 0.8 - 0.6 * math.exp(-0.3 * l)
    if ctx is None:
        ya, st_ret = _ret_call(proj3, wts['ret_decay'][l], None)
        yb, st_gla = _gla_call(proj3, la3, wts['gla_norm_g'][l], None)
        yc = _diff_call(proj3, None, None, wts['diff_lam'][l], wts['diff_subln_g'][l], lam_init)
        extra = (outs[2], outs[3], st_ret, st_gla)
    else:
        ya, _ = _ret_call(proj3, wts['ret_decay'][l], ctx['ret'])
        yb, _ = _gla_call(proj3, la3, wts['gla_norm_g'][l], ctx['gla'])
        yc = _diff_call(proj3, ctx['dk'], ctx['dv'], wts['diff_lam'][l], wts['diff_subln_g'][l],
                        lam_init)
        extra = None
    bw = N_HEAD * DV
    x1 = _merge_call(x, ya.reshape(n, bw), yb.reshape(n, bw), yc.reshape(n, bw), proj, mod5, row0,
                     span, wts['w_branch'][l], wts['w_out'][l], wts['ln_g'][l, 0:1], wts['ln_b'][l, 0:1])
    x2 = _ffn_call(x1, mod5, row0, span, seq_len, wts['w_up'][l], wts['conv_w'][l], wts['conv_b'][l],
                   wts['w_down'][l], wts['ln_g'][l, 1:2], wts['ln_b'][l, 1:2])
    return x2, extra


def kernel(x_prompt, x_sample, cache_diff_k, cache_diff_v, state_ret, state_gla, c, c_ctx,
           ada_w, ada_b, w_in, ret_decay, gla_wa1, gla_wa2, gla_ba, gla_norm_g, diff_lam,
           diff_subln_g, w_branch, w_out, ln_g, ln_b, ffn_w_up, ffn_conv_w, ffn_conv_b, ffn_w_down):
    bp, tp, _ = x_prompt.shape
    bs, ts, _ = x_sample.shape
    past = cache_diff_k.shape[3]

    cvec = jnp.concatenate([c_ctx[None, :], c, jnp.zeros((SUBLANES - 1 - bs, D_MODEL), F32)], axis=0)
    mod = _ada_call(cvec, ada_w, ada_b).reshape(DEPTH, SUBLANES, 6, 1, D_MODEL)

    wa1 = jnp.concatenate([gla_wa1[:, 0], gla_wa1[:, 1]], axis=-1)
    wa1 = jnp.pad(wa1, ((0, 0), (0, 0), (0, LANES - 2 * GLA_RANK))).astype(BF16)
    wa2 = jnp.zeros((DEPTH, LANES, 4 * LANES), F32)
    wa2 = wa2.at[:, 0:GLA_RANK, 0:2 * LANES].set(gla_wa2[:, 0])
    wa2 = wa2.at[:, GLA_RANK:2 * GLA_RANK, 2 * LANES:].set(gla_wa2[:, 1]).astype(BF16)
    gate0 = D_IN - N_GATE_COLS
    w_in_perm = jnp.concatenate([w_in[:, :, gate0:], w_in[:, :, :gate0]], axis=-1)
    wts = {
        'w_in': w_in_perm.astype(BF16), 'wa1': wa1, 'wa2': wa2,
        'ba': gla_ba.reshape(DEPTH, 1, 4 * LANES), 'ret_decay': ret_decay,
        'gla_norm_g': gla_norm_g.reshape(DEPTH, 1, DV), 'diff_lam': diff_lam,
        'diff_subln_g': diff_subln_g.reshape(DEPTH, DV, 1),
        'w_branch': w_branch.astype(BF16), 'w_out': w_out.astype(BF16),
        'ln_g': ln_g, 'ln_b': ln_b, 'w_up': ffn_w_up.astype(BF16), 'conv_w': ffn_conv_w,
        'conv_b': ffn_conv_b.reshape(DEPTH, 1, D_FF), 'w_down': ffn_w_down.astype(BF16),
    }

    h = x_prompt.reshape(bp * tp, D_MODEL)
    dks, dvs, rets, glas = [], [], [], []
    for l in range(DEPTH):
        h, (ck, cv, st_ret, st_gla) = _trunk(h, mod[l], 0, tp, bp, l, wts, None, None)
        dks.append(ck.reshape(bp, tp, 2 * N_HEAD, DK).transpose(0, 2, 1, 3))
        dvs.append(cv.reshape(bp, tp, N_HEAD, DV).transpose(0, 2, 1, 3))
        rets.append(st_ret)
        glas.append(st_gla)
    y_prompt = h.reshape(bp, tp, D_MODEL)

    tables = _rope_tables(ts)
    z = x_sample.reshape(bs * ts, D_MODEL)
    for l in range(DEPTH):
        ctx = {
            'dk': cache_diff_k[:, l].transpose(0, 2, 1, 3).reshape(bs, past, 2 * N_HEAD * DK).astype(BF16),
            'dv': cache_diff_v[:, l].transpose(0, 2, 1, 3).reshape(bs, past, N_HEAD * DV).astype(BF16),
            'ret': state_ret[:, l], 'gla': state_gla[:, l],
        }
        z, _ = _trunk(z, mod[l], 1, ts, bs, l, wts, tables, ctx)
    y_sample = z.reshape(bs, ts, D_MODEL)

    return (y_prompt, y_sample, jnp.stack(dks, axis=1), jnp.stack(dvs, axis=1),
            jnp.stack(rets, axis=1), jnp.stack(glas, axis=1))
```

```python
import functools
import math

import numpy as np
import jax
import jax.numpy as jnp
from jax import lax
from jax.experimental import pallas as pl
from jax.experimental.pallas import tpu as pltpu

F32 = jnp.float32
BF16 = jnp.bfloat16

D_MODEL = 1024
DEPTH = 2
GRID_W = 64
N_HEAD = 4
DK = 64
DV = 128
GLA_RANK = 16
GLA_TAU = 16.0
D_FF = 2816
ROPE_BASE = 10000.0
ALPHA = (2 * DEPTH) ** 0.25
EPS = 1e-5
D_IN = 7680

LANES = 128
SUBLANES = 8
VMEM_LIMIT = 56 * 1024 * 1024

N_GATE_COLS = 3 * D_MODEL
CB_M = 0
CB_AQ, CB_AK, CB_AV, CB_AG = 24, 26, 28, 32
CB_BQ, CB_BK, CB_BV, CB_BR = 36, 38, 40, 44
CB_CQ, CB_CK, CB_CV = 48, 52, 56
ROPE_BLOCKS = (24, 25, 26, 27, 48, 49, 50, 51, 52, 53, 54, 55)
SCALED_BLOCKS = (26, 27, 36, 37)
SOFTMAX_Q_BLOCKS = (48, 49, 50, 51)
QK_SCALE = DK ** -0.5
LOG2E = math.log2(math.e)

TOKEN_BLOCK = 512
RET_CHUNK = 256
GLA_BLOCK = 128
Q_BLOCK = 512
DIFF_KEY_TILE = 512
DIFF_ONES_ROWS = 16
FF_CHUNK = 256


def _cparams(sem):
    return pltpu.CompilerParams(dimension_semantics=sem, vmem_limit_bytes=VMEM_LIMIT)


def _resident(shape, *lead):
    idx = tuple(lead) + (0,) * len(shape)
    return pl.BlockSpec((None,) * len(lead) + tuple(shape), lambda *_: idx,
                        pipeline_mode=pl.Buffered(1))


def _mod_spec(l, row0, span, k):
    return pl.BlockSpec((None, None, None, 1, D_MODEL),
                        lambda i: (l, row0 + (i * TOKEN_BLOCK) // span, k, 0, 0))


def _sigmoid(x):
    return 1.0 / (1.0 + jnp.exp(-x))


def _log_sigmoid(x):
    return jnp.minimum(x, 0.0) - jnp.log(1.0 + jnp.exp(-jnp.abs(x)))


def _dot(a, b):
    return jnp.dot(a, b, preferred_element_type=F32)


def _dot_nt(a, b):
    return lax.dot_general(a, b, (((1,), (1,)), ((), ())), preferred_element_type=F32)


def _dot_tn(a, b):
    return lax.dot_general(a, b, (((0,), (0,)), ((), ())), preferred_element_type=F32)


def _ada_kernel(c_ref, w_ref, b_ref, o_ref):
    c = c_ref[...]
    s = (c * _sigmoid(c)).astype(BF16)
    o_ref[...] = _dot(s, w_ref[...].astype(BF16)) + b_ref[...]


def _ada_call(cvec, ada_w, ada_b):
    nt = 1536
    return pl.pallas_call(
        _ada_kernel,
        grid=(DEPTH, 6 * D_MODEL // nt),
        in_specs=[pl.BlockSpec((SUBLANES, D_MODEL), lambda l, j: (0, 0)),
                  pl.BlockSpec((None, D_MODEL, nt), lambda l, j: (l, 0, j)),
                  pl.BlockSpec((None, 1, nt), lambda l, j: (l, 0, j))],
        out_specs=pl.BlockSpec((None, SUBLANES, nt), lambda l, j: (l, 0, j)),
        out_shape=jax.ShapeDtypeStruct((DEPTH, SUBLANES, 6 * D_MODEL), F32),
        compiler_params=_cparams(("arbitrary", "arbitrary")),
    )(cvec, ada_w, ada_b.reshape(DEPTH, 1, 6 * D_MODEL))


def _proj_kernel(*refs, rope, emit_kv, n_alias, seq_len):
    x_ref, sh_ref, sc_ref, w_ref, wa1_ref, wa2_ref, ba_ref = refs[:7]
    pos = 7
    if rope:
        cos_ref, sa_ref, sb_ref = refs[pos:pos + 3]
        pos += 3
    pos += n_alias
    proj_ref, la_ref = refs[pos:pos + 2]
    pos += 2
    if emit_kv:
        ck_ref, cv_ref = refs[pos:pos + 2]

    h = (x_ref[...] * (1.0 + sc_ref[...]) + sh_ref[...]).astype(BF16)
    tile = 512
    per = tile // LANES
    n_blk = D_IN // LANES
    for j in range(D_IN // tile):
        acc = _dot(h, w_ref[:, j * tile:(j + 1) * tile])
        for i in range(per):
            blk = (j * per + i + N_GATE_COLS // LANES) % n_blk
            y = acc[:, i * LANES:(i + 1) * LANES]
            if emit_kv and CB_CK <= blk < CB_CK + 4:
                for s in range(x_ref.shape[0] // seq_len):
                    for half in range(2):
                        ck_ref[s, 2 * (blk - CB_CK) + half] = (
                            y[s * seq_len:(s + 1) * seq_len, half * DK:(half + 1) * DK])
            if emit_kv and CB_CV <= blk < CB_CV + 4:
                for s in range(x_ref.shape[0] // seq_len):
                    cv_ref[s, blk - CB_CV] = y[s * seq_len:(s + 1) * seq_len, :]
            if rope and blk in ROPE_BLOCKS:
                y = (y * cos_ref[...] + pltpu.roll(y, LANES - 16, 1) * sa_ref[...]
                     + pltpu.roll(y, 16, 1) * sb_ref[...])
            if blk in SCALED_BLOCKS:
                y = y * QK_SCALE
            if blk in SOFTMAX_Q_BLOCKS:
                y = y * (QK_SCALE * LOG2E)
            proj_ref[:, blk * LANES:(blk + 1) * LANES] = y.astype(BF16)

    r = _dot(h, wa1_ref[...]).astype(BF16)
    z = _dot(r, wa2_ref[...]) + ba_ref[...]
    la_ref[...] = _log_sigmoid(z) * (1.0 / GLA_TAU)


def _proj_call(x, mod, l, row0, span, seq_len, wts, tables, emit_kv, kv_prev):
    n = x.shape[0]
    tb = TOKEN_BLOCK
    bps = max(seq_len // tb, 1)
    rope = tables is not None

    in_specs = [pl.BlockSpec((tb, D_MODEL), lambda i: (i, 0)),
                _mod_spec(l, row0, span, 0), _mod_spec(l, row0, span, 1),
                _resident((D_MODEL, D_IN), l), _resident((D_MODEL, LANES), l),
                _resident((LANES, 4 * LANES), l), _resident((1, 4 * LANES), l)]
    args = [x, mod, mod, wts['w_in'], wts['wa1'], wts['wa2'], wts['ba']]
    if rope:
        in_specs += [pl.BlockSpec((tb, LANES), lambda i: (i % bps, 0))] * 3
        args += list(tables)
    aliases = {}
    if kv_prev is not None:
        aliases = {len(args): 2, len(args) + 1: 3}
        in_specs += [pl.BlockSpec(memory_space=pl.ANY)] * 2
        args += list(kv_prev)
    out_specs = [pl.BlockSpec((tb, D_IN), lambda i: (i, 0)),
                 pl.BlockSpec((tb, 4 * LANES), lambda i: (i, 0))]
    out_shape = [jax.ShapeDtypeStruct((n, D_IN), BF16), jax.ShapeDtypeStruct((n, 4 * LANES), F32)]
    if emit_kv:
        spb = tb // seq_len
        nseq = n // seq_len
        out_specs += [pl.BlockSpec((spb, None, 2 * N_HEAD, seq_len, DK), lambda i: (i, l, 0, 0, 0)),
                      pl.BlockSpec((spb, None, N_HEAD, seq_len, DV), lambda i: (i, l, 0, 0, 0))]
        out_shape += [jax.ShapeDtypeStruct((nseq, DEPTH, 2 * N_HEAD, seq_len, DK), F32),
                      jax.ShapeDtypeStruct((nseq, DEPTH, N_HEAD, seq_len, DV), F32)]
    return pl.pallas_call(
        functools.partial(_proj_kernel, rope=rope, emit_kv=emit_kv, n_alias=len(aliases),
                          seq_len=seq_len),
        grid=(n // tb,), in_specs=in_specs, out_specs=out_specs, out_shape=out_shape,
        input_output_aliases=aliases,
        compiler_params=_cparams(("arbitrary",)),
    )(*args)


def _ret_kernel(*refs, layer, seq, chunk, has_ctx, n_alias):
    dec_ref, q_ref, k_ref, v_ref, g_ref = refs[:5]
    pos = 5
    if has_ctx:
        s0_ref = refs[pos]
        pos += 1
    pos += n_alias
    y_ref = refs[pos]
    pos += 1
    if not has_ctx:
        st_ref = refs[pos]
        pos += 1
    of_ref, ob_ref, s_ref, dm_ref, eq_ref, ek_ref, gc_ref = refs[pos:]
    p = pl.program_id(1)
    nc = seq // chunk
    c_f = float(chunk)
    lane = lax.broadcasted_iota(jnp.int32, (1, LANES), 1)
    hmask = [lane < DK, lane >= DK]
    rowp = lax.broadcasted_iota(jnp.int32, (chunk, 1), 0).astype(F32)
    colp = lax.broadcasted_iota(jnp.int32, (1, chunk), 1).astype(F32)
    diff = rowp - colp

    for d in range(2):
        for hh in range(2):
            ci = 2 * d + hh
            raw = dec_ref[layer, d, 2 * p + hh]
            lg_l = _log_sigmoid(jnp.full((1, LANES), raw, F32))
            lg_c = _log_sigmoid(jnp.full((1, chunk), raw, F32))
            if d == 0:
                dm_ref[ci] = jnp.where(diff >= 0, jnp.exp(jnp.maximum(diff, 0.0) * lg_c), 0.0)
                eq_ref[ci] = jnp.exp((rowp + 1.0) * lg_l)
                ek_ref[ci] = jnp.exp((c_f - 1.0 - rowp) * lg_l)
            else:
                dm_ref[ci] = jnp.where(diff <= 0, jnp.exp(jnp.maximum(-diff, 0.0) * lg_c), 0.0)
                eq_ref[ci] = jnp.exp((c_f - rowp) * lg_l)
                ek_ref[ci] = jnp.exp(rowp * lg_l)
            gc_ref[ci] = jnp.exp(c_f * lg_l)
            if has_ctx:
                s0 = s0_ref[d, hh]
                zero = jnp.zeros((DK, DV), F32)
                s_ref[ci] = jnp.concatenate([s0, zero] if hh == 0 else [zero, s0], axis=0)
            else:
                s_ref[ci] = jnp.zeros((LANES, DV), F32)

    def step(n, carry):
        for d in range(2):
            c = n if d == 0 else nc - 1 - n
            r0 = pl.multiple_of(c * chunk, chunk)
            qc = q_ref[pl.ds(r0, chunk), :]
            kc = k_ref[pl.ds(r0, chunk), :]
            o_ref = of_ref if d == 0 else ob_ref
            for hh in range(2):
                ci = 2 * d + hh
                qh = jnp.where(hmask[hh], qc, jnp.zeros_like(qc))
                kh = jnp.where(hmask[hh], kc, jnp.zeros_like(kc))
                vh = v_ref[pl.ds(r0, chunk), hh * DV:(hh + 1) * DV]
                sc = _dot_nt(qh, kh) * dm_ref[ci]
                intra = _dot(sc.astype(BF16), vh)
                sb = s_ref[ci]
                qd = (qh.astype(F32) * eq_ref[ci]).astype(BF16)
                inter = _dot(qd, sb.astype(BF16))
                o_ref[hh, pl.ds(r0, chunk), :] = intra + inter
                kd = (kh.astype(F32) * ek_ref[ci]).astype(BF16)
                s_ref[ci] = gc_ref[ci] * sb + _dot_tn(kd, vh)
        return carry

    lax.fori_loop(0, nc, step, 0)

    if not has_ctx:
        for d in range(2):
            for hh in range(2):
                st_ref[d, hh] = s_ref[2 * d + hh][hh * DK:(hh + 1) * DK, :]

    def finish(n, carry):
        r0 = pl.multiple_of(n * chunk, chunk)
        for hh in range(2):
            o = of_ref[hh, pl.ds(r0, chunk), :] + ob_ref[hh, pl.ds(r0, chunk), :]
            mu = jnp.mean(o, axis=-1, keepdims=True)
            oc = o - mu
            var = jnp.mean(oc * oc, axis=-1, keepdims=True)
            g = g_ref[pl.ds(r0, chunk), hh * DV:(hh + 1) * DV].astype(F32)
            y = oc * lax.rsqrt(var + EPS) * (g * _sigmoid(g))
            y_ref[pl.ds(r0, chunk), hh * DV:(hh + 1) * DV] = y.astype(BF16)
        return carry

    lax.fori_loop(0, nc, finish, 0)


def _state_io(l, batch, s0_all, st_prev, n_in):
    spec = pl.BlockSpec((None, None, 2, 2, DK, DV), lambda i, p: (i, l, 0, p, 0, 0))
    if s0_all is not None:
        return [spec], [s0_all], [], [], {}
    shape = jax.ShapeDtypeStruct((batch, DEPTH, 2, N_HEAD, DK, DV), F32)
    if st_prev is None:
        return [], [], [spec], [shape], {}
    return [pl.BlockSpec(memory_space=pl.ANY)], [st_prev], [spec], [shape], {n_in: 1}


def _ret_call(proj3, decay, l, s0_all, st_prev):
    b, t, _ = proj3.shape
    chunk = min(RET_CHUNK, t)
    has_ctx = s0_all is not None
    in_specs = [pl.BlockSpec(memory_space=pltpu.SMEM),
                pl.BlockSpec((None, t, LANES), lambda i, p: (i, 0, CB_AQ + p)),
                pl.BlockSpec((None, t, LANES), lambda i, p: (i, 0, CB_AK + p)),
                pl.BlockSpec((None, t, 2 * DV), lambda i, p: (i, 0, CB_AV // 2 + p)),
                pl.BlockSpec((None, t, 2 * DV), lambda i, p: (i, 0, CB_AG // 2 + p))]
    args = [decay, proj3, proj3, proj3, proj3]
    st_in, st_args, st_out, st_shape, aliases = _state_io(l, b, s0_all, st_prev, len(args))
    return pl.pallas_call(
        functools.partial(_ret_kernel, layer=l, seq=t, chunk=chunk, has_ctx=has_ctx,
                          n_alias=len(aliases)),
        grid=(b, 2), in_specs=in_specs + st_in,
        out_specs=[pl.BlockSpec((None, t, 2 * DV), lambda i, p: (i, 0, p))] + st_out,
        out_shape=[jax.ShapeDtypeStruct((b, t, N_HEAD * DV), BF16)] + st_shape,
        input_output_aliases=aliases,
        scratch_shapes=[pltpu.VMEM((2, t, DV), F32), pltpu.VMEM((2, t, DV), F32),
                        pltpu.VMEM((4, LANES, DV), F32), pltpu.VMEM((4, chunk, chunk), F32),
                        pltpu.VMEM((4, chunk, LANES), F32), pltpu.VMEM((4, chunk, LANES), F32),
                        pltpu.VMEM((4, 1, LANES), F32)],
        compiler_params=_cparams(("arbitrary", "arbitrary")),
    )(*(args + st_args))


def _gla_block(q, k, x, v_pair, st_list, hmask, blk, reverse):
    row = lax.broadcasted_iota(jnp.int32, (blk, 1), 0)
    xr = (lax.broadcasted_iota(jnp.int32, (blk, blk), 0)
          ^ lax.broadcasted_iota(jnp.int32, (blk, blk), 1))
    w = x
    tot = x
    qb = q.astype(BF16)
    kb = k.astype(BF16)
    amat = []
    for hh in range(2):
        qh = jnp.where(hmask[hh], qb, jnp.zeros_like(qb))
        amat.append(jnp.where(xr == 0, _dot_nt(qh, kb), 0.0))
    h = 1
    while h < blk:
        up = (row & h) != 0
        qside = jnp.logical_not(up) if reverse else up
        f = jnp.exp(-jnp.where(qside, w, tot - w))
        ql = jnp.where(qside, q * f, 0.0).astype(BF16)
        kl = jnp.where(qside, 0.0, k * f).astype(BF16)
        for hh in range(2):
            qh = jnp.where(hmask[hh], ql, jnp.zeros_like(ql))
            amat[hh] = amat[hh] + jnp.where(xr < 2 * h, _dot_nt(qh, kl), 0.0)
        partner = jnp.where(up, pltpu.roll(tot, h, 0), pltpu.roll(tot, blk - h, 0))
        w = w + jnp.where(qside, partner, 0.0)
        tot = tot + partner
        h *= 2
    qd = (q * jnp.exp(-w)).astype(BF16)
    kd = (k * jnp.exp(-(tot - w))).astype(BF16)
    dec = jnp.exp(-tot[0:1, :])
    outs, new_st = [], []
    for hh in range(2):
        vh = v_pair[:, hh * DV:(hh + 1) * DV]
        st = st_list[hh]
        qh = jnp.where(hmask[hh], qd, jnp.zeros_like(qd))
        kh = jnp.where(hmask[hh], kd, jnp.zeros_like(kd))
        o = _dot(amat[hh].astype(BF16), vh) + _dot_nt(qh, st.astype(BF16))
        outs.append(o)
        new_st.append(dec * st + _dot_tn(vh, kh))
    return outs, new_st


def _gla_kernel(*refs, seq, blk, has_ctx, n_alias):
    q_ref, k_ref, v_ref, r_ref, laf_ref, lab_ref, ng_ref = refs[:7]
    pos = 7
    if has_ctx:
        s0_ref = refs[pos]
        pos += 1
    pos += n_alias
    y_ref = refs[pos]
    pos += 1
    if not has_ctx:
        st_ref = refs[pos]
        pos += 1
    of_ref, ob_ref, s_ref = refs[pos:]
    nb = seq // blk
    lane = lax.broadcasted_iota(jnp.int32, (1, LANES), 1)
    hmask = [lane < DK, lane >= DK]

    for d in range(2):
        for hh in range(2):
            if has_ctx:
                zero = jnp.zeros((DK, DV), F32)
                s0 = s0_ref[d, hh]
                full = jnp.concatenate([s0, zero] if hh == 0 else [zero, s0], axis=0)
                s_ref[2 * d + hh] = full.T
            else:
                s_ref[2 * d + hh] = jnp.zeros((DV, LANES), F32)

    def step(n, carry):
        for d in range(2):
            c = n if d == 0 else nb - 1 - n
            r0 = pl.multiple_of(c * blk, blk)
            q = q_ref[pl.ds(r0, blk), :].astype(F32)
            k = k_ref[pl.ds(r0, blk), :].astype(F32)
            la_ref = laf_ref if d == 0 else lab_ref
            x = -la_ref[pl.ds(r0, blk), :]
            vp = v_ref[pl.ds(r0, blk), :]
            sts = [s_ref[2 * d], s_ref[2 * d + 1]]
            outs, new_st = _gla_block(q, k, x, vp, sts, hmask, blk, d == 1)
            o_ref = of_ref if d == 0 else ob_ref
            for hh in range(2):
                o_ref[hh, pl.ds(r0, blk), :] = outs[hh]
                s_ref[2 * d + hh] = new_st[hh]
        return carry

    lax.fori_loop(0, nb, step, 0)

    if not has_ctx:
        for d in range(2):
            for hh in range(2):
                st_ref[d, hh] = s_ref[2 * d + hh][...].T[hh * DK:(hh + 1) * DK, :]

    fin = min(seq, 512)

    def finish(n, carry):
        r0 = pl.multiple_of(n * fin, fin)
        for hh in range(2):
            o = of_ref[hh, pl.ds(r0, fin), :] + ob_ref[hh, pl.ds(r0, fin), :]
            ms = jnp.mean(o * o, axis=-1, keepdims=True)
            g = r_ref[pl.ds(r0, fin), hh * DV:(hh + 1) * DV].astype(F32)
            y = o * lax.rsqrt(ms + EPS) * ng_ref[...] * (g * _sigmoid(g))
            y_ref[pl.ds(r0, fin), hh * DV:(hh + 1) * DV] = y.astype(BF16)
        return carry

    lax.fori_loop(0, seq // fin, finish, 0)


def _gla_call(proj3, la3, norm_g, l, s0_all, st_prev):
    b, t, _ = proj3.shape
    blk = GLA_BLOCK
    has_ctx = s0_all is not None
    in_specs = [pl.BlockSpec((None, t, LANES), lambda i, p: (i, 0, CB_BQ + p)),
                pl.BlockSpec((None, t, LANES), lambda i, p: (i, 0, CB_BK + p)),
                pl.BlockSpec((None, t, 2 * DV), lambda i, p: (i, 0, CB_BV // 2 + p)),
                pl.BlockSpec((None, t, 2 * DV), lambda i, p: (i, 0, CB_BR // 2 + p)),
                pl.BlockSpec((None, t, LANES), lambda i, p: (i, 0, p)),
                pl.BlockSpec((None, t, LANES), lambda i, p: (i, 0, 2 + p)),
                pl.BlockSpec((None, 1, DV), lambda i, p: (l, 0, 0))]
    args = [proj3, proj3, proj3, proj3, la3, la3, norm_g]
    st_in, st_args, st_out, st_shape, aliases = _state_io(l, b, s0_all, st_prev, len(args))
    return pl.pallas_call(
        functools.partial(_gla_kernel, seq=t, blk=blk, has_ctx=has_ctx, n_alias=len(aliases)),
        grid=(b, 2), in_specs=in_specs + st_in,
        out_specs=[pl.BlockSpec((None, t, 2 * DV), lambda i, p: (i, 0, p))] + st_out,
        out_shape=[jax.ShapeDtypeStruct((b, t, N_HEAD * DV), BF16)] + st_shape,
        input_output_aliases=aliases,
        scratch_shapes=[pltpu.VMEM((2, t, DV), F32), pltpu.VMEM((2, t, DV), F32),
                        pltpu.VMEM((4, DV, LANES), F32)],
        compiler_params=_cparams(("arbitrary", "arbitrary")),
    )(*(args + st_args))


def _diff_kernel(*refs, lam_init, n_cache, n_new):
    lam_ref, q_ref = refs[:2]
    pos = 2
    if n_cache:
        kc_ref, vc_ref = refs[pos:pos + 2]
        pos += 2
    kn_ref, vn_ref, sg_ref, y_ref = refs[pos:pos + 4]
    pos += 4
    if n_cache:
        vtc_ref = refs[pos]
        pos += 1
    vtn_ref, sa_ref, sb_ref, acc_ref = refs[pos:]
    qb = q_ref.shape[0]
    kt = min(n_new, DIFF_KEY_TILE)
    n_tiles = n_new // kt

    @pl.when(pl.program_id(2) == 0)
    def _():
        if n_cache:
            vtc_ref[0:DV, :] = vc_ref[...].T
            vtc_ref[DV:, :] = jnp.ones((DIFF_ONES_ROWS, n_cache), BF16)
        for r in range(n_tiles):
            vtn_ref[r, 0:DV, :] = vn_ref[r * kt:(r + 1) * kt, :].T
            vtn_ref[r, DV:, :] = jnp.ones((DIFF_ONES_ROWS, kt), BF16)

    lp = lam_ref[...]
    lam = (jnp.exp(jnp.sum(lp[0:1] * lp[1:2], axis=-1, keepdims=True))
           - jnp.exp(jnp.sum(lp[2:3] * lp[3:4], axis=-1, keepdims=True)) + lam_init)
    lane = lax.broadcasted_iota(jnp.int32, (1, LANES), 1)
    q = q_ref[...]
    qs = [jnp.where((lane < DK) if i == 0 else (lane >= DK), q, jnp.zeros_like(q)) for i in range(2)]

    s_bufs = (sa_ref, sb_ref)
    new0 = 1 if n_cache else 0
    last = new0 + n_tiles - 1

    def tile_rows(idx):
        return n_cache if idx < new0 else kt

    def scores(idx, r=None):
        if idx < new0:
            k_tile = kc_ref[...]
        elif r is None:
            k_tile = kn_ref[(idx - new0) * kt:(idx - new0 + 1) * kt, :]
        else:
            k_tile = kn_ref[pl.ds(pl.multiple_of(r * kt, kt), kt), :]
        for i in range(2):
            s_bufs[idx % 2][i, 0:tile_rows(idx), :] = _dot_nt(k_tile, qs[i])

    def soft_pv(idx, m, r=None):
        n = tile_rows(idx)
        if idx < new0:
            vt_tile = vtc_ref[...]
        else:
            vt_tile = vtn_ref[idx - new0 if r is None else r]
        m_out = []
        for i in range(2):
            s = s_bufs[idx % 2][i, 0:n, :]
            m_new = jnp.maximum(m[i], jnp.max(s, axis=0, keepdims=True))
            e = jnp.exp2(s - m_new).astype(BF16)
            m_out.append(m_new)
            pv = _dot(vt_tile, e)
            acc_ref[i] = pv if idx == 0 else acc_ref[i] * jnp.exp2(m[i] - m_new) + pv
        return m_out

    m = [jnp.full((1, qb), -1e30, F32) for _ in range(2)]
    scores(0)
    if last >= 1:
        scores(1)
    m = soft_pv(0, m)
    n_pairs = max(last - 1, 0) // 2
    loop_end = 1 + 2 * n_pairs

    def pair(p, carry):
        m = list(carry)
        r = 2 * p + (1 - new0)
        scores(2, r + 1)
        m = soft_pv(1, m, r)
        scores(3, r + 2)
        m = soft_pv(2, m, r + 1)
        return tuple(m)

    if n_pairs:
        m = list(lax.fori_loop(0, n_pairs, pair, tuple(m)))
    for idx in range(loop_end, last + 1):
        if idx + 1 <= last:
            scores(idx + 1)
        m = soft_pv(idx, m)

    acc0 = acc_ref[0]
    acc1 = acc_ref[1]
    o = (acc0[0:DV] * (1.0 / acc0[DV:DV + 1]) - acc1[0:DV] * (lam / acc1[DV:DV + 1]))
    ms = jnp.mean(o * o, axis=0, keepdims=True)
    y = o * lax.rsqrt(ms + EPS) * (sg_ref[...] * (1.0 - lam_init))
    y_ref[...] = y.T.astype(BF16)


def _diff_call(proj3, cache_k, cache_v, lam_p, subln_col, l):
    b, t, _ = proj3.shape
    n_cache = 0 if cache_k is None else cache_k.shape[1]
    qb = min(Q_BLOCK, t)
    kt = min(t, DIFF_KEY_TILE)
    vrows = DV + DIFF_ONES_ROWS
    lam_init = 0.8 - 0.6 * math.exp(-0.3 * l)
    in_specs = [pl.BlockSpec((None, 4, DK), lambda i, h, j: (l, 0, 0)),
                pl.BlockSpec((None, qb, LANES), lambda i, h, j: (i, j, CB_CQ + h))]
    args = [lam_p, proj3]
    if n_cache:
        in_specs += [pl.BlockSpec((None, n_cache, LANES), lambda i, h, j: (i, 0, h)),
                     pl.BlockSpec((None, n_cache, DV), lambda i, h, j: (i, 0, h))]
        args += [cache_k, cache_v]
    in_specs += [pl.BlockSpec((None, t, LANES), lambda i, h, j: (i, 0, CB_CK + h)),
                 pl.BlockSpec((None, t, DV), lambda i, h, j: (i, 0, CB_CV + h)),
                 pl.BlockSpec((None, DV, 1), lambda i, h, j: (l, 0, 0))]
    args += [proj3, proj3, subln_col]
    return pl.pallas_call(
        functools.partial(_diff_kernel, lam_init=lam_init, n_cache=n_cache, n_new=t),
        grid=(b, N_HEAD, t // qb), in_specs=in_specs,
        out_specs=pl.BlockSpec((None, qb, DV), lambda i, h, j: (i, j, h)),
        out_shape=jax.ShapeDtypeStruct((b, t, N_HEAD * DV), BF16),
        scratch_shapes=([pltpu.VMEM((vrows, n_cache), BF16)] if n_cache else [])
        + [pltpu.VMEM((t // kt, vrows, kt), BF16), pltpu.VMEM((2, max(kt, n_cache), qb), F32),
           pltpu.VMEM((2, max(kt, n_cache), qb), F32), pltpu.VMEM((2, vrows, qb), F32)],
        compiler_params=_cparams(("arbitrary", "arbitrary", "arbitrary")),
    )(*args)


def _layer_norm(z, g, b):
    mu = jnp.mean(z, axis=-1, keepdims=True)
    zc = z - mu
    var = jnp.mean(zc * zc, axis=-1, keepdims=True)
    return zc * lax.rsqrt(var + EPS) * g + b


def _merge_kernel(x_ref, ya_ref, yb_ref, yc_ref, m_ref, g1_ref, wb_ref, wo_ref, lg_ref, lb_ref, o_ref):
    merged = None
    for i, y_ref in enumerate((ya_ref, yb_ref, yc_ref)):
        gate = _sigmoid(m_ref[:, i * D_MODEL:(i + 1) * D_MODEL].astype(F32))
        term = gate * _dot(y_ref[...], wb_ref[i])
        merged = term if merged is None else merged + term
    out = _dot(merged.astype(BF16), wo_ref[...])
    z = ALPHA * x_ref[...] + g1_ref[...] * out
    o_ref[...] = _layer_norm(z, lg_ref[...], lb_ref[...])


def _merge_call(x, ya, yb, yc, proj, mod, l, row0, span, wts):
    n = x.shape[0]
    tb = TOKEN_BLOCK
    bw = N_HEAD * DV
    tok = lambda w: pl.BlockSpec((tb, w), lambda i: (i, 0))
    return pl.pallas_call(
        _merge_kernel,
        grid=(n // tb,),
        in_specs=[tok(D_MODEL), tok(bw), tok(bw), tok(bw),
                  pl.BlockSpec((tb, N_GATE_COLS), lambda i: (i, 0)),
                  _mod_spec(l, row0, span, 2),
                  _resident((3, bw, D_MODEL), l), _resident((D_MODEL, D_MODEL), l),
                  _resident((1, D_MODEL), l, 0), _resident((1, D_MODEL), l, 0)],
        out_specs=tok(D_MODEL),
        out_shape=jax.ShapeDtypeStruct((n, D_MODEL), F32),
        compiler_params=_cparams(("arbitrary",)),
    )(x, ya, yb, yc, proj, mod, wts['w_branch'], wts['w_out'], wts['ln_g'], wts['ln_b'])


def _gelu_tanh(x):
    return 0.5 * x * (1.0 + jnp.tanh(0.7978845608028654 * (x + 0.044715 * (x * x * x))))


def _ffn_kernel(xp_ref, x_ref, xn_ref, sh_ref, sc_ref, g2_ref, wu_ref, cw_ref, cb_ref, wd_ref,
                lg_ref, lb_ref, o_ref, acc_ref, *, seq_len):
    tb = x_ref.shape[0]
    halo = SUBLANES
    i = pl.program_id(0)
    xm = x_ref[...]
    xin = jnp.concatenate([xp_ref[...], xm, xn_ref[...]], axis=0)
    h = (xin * (1.0 + sc_ref[...]) + sh_ref[...]).astype(BF16)
    hm = h[halo:halo + tb]
    posn = (i * tb + lax.broadcasted_iota(jnp.int32, (tb, 1), 0)) % seq_len
    has_prev = posn != 0
    has_next = posn != seq_len - 1
    for c in range(D_FF // FF_CHUNK):
        lo, hi = c * FF_CHUNK, (c + 1) * FF_CHUNK
        a = _dot(h, wu_ref[:, lo:hi])
        b = _dot(hm, wu_ref[:, D_FF + lo:D_FF + hi])
        rows = tb + 2 * halo
        a_prev = jnp.where(has_prev, pltpu.roll(a, 1, 0)[halo:halo + tb], 0.0)
        a_next = jnp.where(has_next, pltpu.roll(a, rows - 1, 0)[halo:halo + tb], 0.0)
        cv = (a_prev * cw_ref[0:1, lo:hi] + a[halo:halo + tb] * cw_ref[1:2, lo:hi]
              + a_next * cw_ref[2:3, lo:hi] + cb_ref[:, lo:hi])
        act = (_gelu_tanh(cv) * b).astype(BF16)
        part = _dot(act, wd_ref[lo:hi, :])
        if c == 0:
            acc_ref[...] = part
        else:
            acc_ref[...] += part
    z = ALPHA * xm + g2_ref[...] * acc_ref[...]
    o_ref[...] = _layer_norm(z, lg_ref[...], lb_ref[...])


def _ffn_call(x, mod, l, row0, span, seq_len, wts):
    n = x.shape[0]
    tb = TOKEN_BLOCK
    hb = tb // SUBLANES
    last = n // SUBLANES - 1
    modspec = functools.partial(_mod_spec, l, row0, span)

    return pl.pallas_call(
        functools.partial(_ffn_kernel, seq_len=seq_len),
        grid=(n // tb,),
        in_specs=[pl.BlockSpec((SUBLANES, D_MODEL), lambda i: (jnp.maximum(i * hb - 1, 0), 0)),
                  pl.BlockSpec((tb, D_MODEL), lambda i: (i, 0)),
                  pl.BlockSpec((SUBLANES, D_MODEL), lambda i: (jnp.minimum((i + 1) * hb, last), 0)),
                  modspec(3), modspec(4), modspec(5),
                  _resident((D_MODEL, 2 * D_FF), l), _resident((3, D_FF), l),
                  _resident((1, D_FF), l), _resident((D_FF, D_MODEL), l),
                  _resident((1, D_MODEL), l, 1), _resident((1, D_MODEL), l, 1)],
        out_specs=pl.BlockSpec((tb, D_MODEL), lambda i: (i, 0)),
        out_shape=jax.ShapeDtypeStruct((n, D_MODEL), F32),
        scratch_shapes=[pltpu.VMEM((tb, D_MODEL), F32)],
        compiler_params=_cparams(("arbitrary",)),
    )(x, x, x, mod, mod, mod, wts['w_up'], wts['conv_w'], wts['conv_b'], wts['w_down'],
      wts['ln_g'], wts['ln_b'])


def _rope_tables(seq):
    half = DK // 4
    t = np.arange(seq)
    inv = ROPE_BASE ** (-np.arange(half, dtype=np.float64) / half)
    ang_row = (t // GRID_W)[:, None] * inv[None, :]
    ang_col = (t % GRID_W)[:, None] * inv[None, :]
    ang = np.concatenate([ang_row, ang_row, ang_col, ang_col], axis=1)
    first = (np.arange(DK) % (2 * half)) < half
    cos = np.cos(ang)
    sa = np.where(first[None, :], -np.sin(ang), 0.0)
    sb = np.where(first[None, :], 0.0, np.sin(ang))
    tile = lambda a: jnp.asarray(np.concatenate([a, a], axis=1), F32)
    return tile(cos), tile(sa), tile(sb)


def _trunk(x, mod, row0, seq_len, batch, l, wts, tables, ctx, carried):
    n = x.shape[0]
    span = n if ctx is None else seq_len
    kv_prev = None if carried is None else carried[:2]
    outs = _proj_call(x, mod, l, row0, span, seq_len, wts, tables, ctx is None, kv_prev)
    proj, la = outs[0], outs[1]
    proj3 = proj.reshape(batch, seq_len, D_IN)
    la3 = la.reshape(batch, seq_len, 4 * LANES)
    if ctx is None:
        ret_prev, gla_prev = (None, None) if carried is None else carried[2:]
        ya, st_ret = _ret_call(proj3, wts['ret_decay'], l, None, ret_prev)
        yb, st_gla = _gla_call(proj3, la3, wts['gla_norm_g'], l, None, gla_prev)
        yc = _diff_call(proj3, None, None, wts['diff_lam'], wts['diff_subln_g'], l)
        carried = (outs[2], outs[3], st_ret, st_gla)
    else:
        ya, = _ret_call(proj3, wts['ret_decay'], l, ctx['ret'], None)
        yb, = _gla_call(proj3, la3, wts['gla_norm_g'], l, ctx['gla'], None)
        yc = _diff_call(proj3, ctx['dk'][l], ctx['dv'][l], wts['diff_lam'], wts['diff_subln_g'], l)
    bw = N_HEAD * DV
    x1 = _merge_call(x, ya.reshape(n, bw), yb.reshape(n, bw), yc.reshape(n, bw), proj, mod, l,
                     row0, span, wts)
    x2 = _ffn_call(x1, mod, l, row0, span, seq_len, wts)
    return x2, carried


def kernel(x_prompt, x_sample, cache_diff_k, cache_diff_v, state_ret, state_gla, c, c_ctx,
           ada_w, ada_b, w_in, ret_decay, gla_wa1, gla_wa2, gla_ba, gla_norm_g, diff_lam,
           diff_subln_g, w_branch, w_out, ln_g, ln_b, ffn_w_up, ffn_conv_w, ffn_conv_b, ffn_w_down):
    bp, tp, _ = x_prompt.shape
    bs, ts, _ = x_sample.shape
    past = cache_diff_k.shape[3]

    cvec = jnp.concatenate([c_ctx[None, :], c, jnp.zeros((SUBLANES - 1 - bs, D_MODEL), F32)], axis=0)
    mod = _ada_call(cvec, ada_w, ada_b).reshape(DEPTH, SUBLANES, 6, 1, D_MODEL)

    wa1 = jnp.concatenate([gla_wa1[:, 0], gla_wa1[:, 1]], axis=-1)
    wa1 = jnp.pad(wa1, ((0, 0), (0, 0), (0, LANES - 2 * GLA_RANK))).astype(BF16)
    wa2 = jnp.zeros((DEPTH, LANES, 4 * LANES), F32)
    wa2 = wa2.at[:, 0:GLA_RANK, 0:2 * LANES].set(gla_wa2[:, 0])
    wa2 = wa2.at[:, GLA_RANK:2 * GLA_RANK, 2 * LANES:].set(gla_wa2[:, 1]).astype(BF16)
    wts = {
        'w_in': w_in.astype(BF16), 'wa1': wa1, 'wa2': wa2,
        'ba': gla_ba.reshape(DEPTH, 1, 4 * LANES), 'ret_decay': ret_decay,
        'gla_norm_g': gla_norm_g.reshape(DEPTH, 1, DV), 'diff_lam': diff_lam,
        'diff_subln_g': diff_subln_g.reshape(DEPTH, DV, 1),
        'w_branch': w_branch.astype(BF16), 'w_out': w_out.astype(BF16),
        'ln_g': ln_g.reshape(DEPTH, 2, 1, D_MODEL), 'ln_b': ln_b.reshape(DEPTH, 2, 1, D_MODEL),
        'w_up': ffn_w_up.astype(BF16), 'conv_w': ffn_conv_w,
        'conv_b': ffn_conv_b.reshape(DEPTH, 1, D_FF), 'w_down': ffn_w_down.astype(BF16),
    }

    h = x_prompt.reshape(bp * tp, D_MODEL)
    carried = None
    for l in range(DEPTH):
        h, carried = _trunk(h, mod, 0, tp, bp, l, wts, None, None, carried)
    y_prompt = h.reshape(bp, tp, D_MODEL)

    tables = _rope_tables(ts)
    ctx = {
        'dk': cache_diff_k.transpose(1, 0, 3, 2, 4).reshape(DEPTH, bs, past, 2 * N_HEAD * DK).astype(BF16),
        'dv': cache_diff_v.transpose(1, 0, 3, 2, 4).reshape(DEPTH, bs, past, N_HEAD * DV).astype(BF16),
        'ret': state_ret, 'gla': state_gla,
    }
    z = x_sample.reshape(bs * ts, D_MODEL)
    for l in range(DEPTH):
        z, _ = _trunk(z, mod, 1, ts, bs, l, wts, tables, ctx, None)
    y_sample = z.reshape(bs, ts, D_MODEL)

    return (y_prompt, y_sample) + tuple(carried)
```

```python
import functools
import math

import numpy as np
import jax
import jax.numpy as jnp
from jax import lax
from jax.experimental import pallas as pl
from jax.experimental.pallas import tpu as pltpu

F32 = jnp.float32
BF16 = jnp.bfloat16

D_MODEL = 1024
DEPTH = 2
GRID_W = 64
N_HEAD = 4
DK = 64
DV = 128
GLA_RANK = 16
GLA_TAU = 16.0
D_FF = 2816
ROPE_BASE = 10000.0
ALPHA = (2 * DEPTH) ** 0.25
EPS = 1e-5
D_IN = 7680

LANES = 128
SUBLANES = 8
VMEM_LIMIT = 56 * 1024 * 1024

N_GATE_COLS = 3 * D_MODEL
CB_M = 0
CB_AQ, CB_AK, CB_AV, CB_AG = 24, 26, 28, 32
CB_BQ, CB_BK, CB_BV, CB_BR = 36, 38, 40, 44
CB_CQ, CB_CK, CB_CV = 48, 52, 56
ROPE_BLOCKS = (24, 25, 26, 27, 48, 49, 50, 51, 52, 53, 54, 55)
SCALED_BLOCKS = (26, 27, 36, 37)
SOFTMAX_Q_BLOCKS = (48, 49, 50, 51)
QK_SCALE = DK ** -0.5
LOG2E = math.log2(math.e)

TOKEN_BLOCK = 512
RET_CHUNK = 256
GLA_BLOCK = 128
Q_BLOCK = 1024
DIFF_KEY_TILE = 512
DIFF_ONES_ROWS = 16
FF_CHUNK = 256


def _cparams(sem):
    return pltpu.CompilerParams(dimension_semantics=sem, vmem_limit_bytes=VMEM_LIMIT)


def _resident(shape, *lead):
    idx = tuple(lead) + (0,) * len(shape)
    return pl.BlockSpec((None,) * len(lead) + tuple(shape), lambda *_: idx,
                        pipeline_mode=pl.Buffered(1))


def _mod_spec(l, row0, span, k):
    return pl.BlockSpec((None, None, None, 1, D_MODEL),
                        lambda i: (l, row0 + (i * TOKEN_BLOCK) // span, k, 0, 0))


def _sigmoid(x):
    return 1.0 / (1.0 + jnp.exp(-x))


def _log_sigmoid(x):
    return jnp.minimum(x, 0.0) - jnp.log(1.0 + jnp.exp(-jnp.abs(x)))


def _dot(a, b):
    return jnp.dot(a, b, preferred_element_type=F32)


def _dot_nt(a, b):
    return lax.dot_general(a, b, (((1,), (1,)), ((), ())), preferred_element_type=F32)


def _dot_tn(a, b):
    return lax.dot_general(a, b, (((0,), (0,)), ((), ())), preferred_element_type=F32)


def _ada_kernel(c_ref, w_ref, b_ref, o_ref):
    c = c_ref[...]
    s = (c * _sigmoid(c)).astype(BF16)
    o_ref[...] = _dot(s, w_ref[...].astype(BF16)) + b_ref[...]


def _ada_call(cvec, ada_w, ada_b):
    nt = 1536
    return pl.pallas_call(
        _ada_kernel,
        grid=(DEPTH, 6 * D_MODEL // nt),
        in_specs=[pl.BlockSpec((SUBLANES, D_MODEL), lambda l, j: (0, 0)),
                  pl.BlockSpec((None, D_MODEL, nt), lambda l, j: (l, 0, j)),
                  pl.BlockSpec((None, 1, nt), lambda l, j: (l, 0, j))],
        out_specs=pl.BlockSpec((None, SUBLANES, nt), lambda l, j: (l, 0, j)),
        out_shape=jax.ShapeDtypeStruct((DEPTH, SUBLANES, 6 * D_MODEL), F32),
        compiler_params=_cparams(("arbitrary", "arbitrary")),
    )(cvec, ada_w, ada_b.reshape(DEPTH, 1, 6 * D_MODEL))


def _proj_kernel(*refs, rope, emit_kv, n_alias, seq_len):
    x_ref, sh_ref, sc_ref, w_ref, wa1_ref, wa2_ref, ba_ref = refs[:7]
    pos = 7
    if rope:
        cos_ref, sa_ref, sb_ref = refs[pos:pos + 3]
        pos += 3
    pos += n_alias
    proj_ref, la_ref = refs[pos:pos + 2]
    pos += 2
    if emit_kv:
        ck_ref, cv_ref = refs[pos:pos + 2]

    h = (x_ref[...] * (1.0 + sc_ref[...]) + sh_ref[...]).astype(BF16)
    tile = 512
    per = tile // LANES
    n_blk = D_IN // LANES
    for j in range(D_IN // tile):
        acc = _dot(h, w_ref[:, j * tile:(j + 1) * tile])
        for i in range(per):
            blk = (j * per + i + N_GATE_COLS // LANES) % n_blk
            y = acc[:, i * LANES:(i + 1) * LANES]
            if emit_kv and CB_CK <= blk < CB_CK + 4:
                for s in range(x_ref.shape[0] // seq_len):
                    for half in range(2):
                        ck_ref[s, 2 * (blk - CB_CK) + half] = (
                            y[s * seq_len:(s + 1) * seq_len, half * DK:(half + 1) * DK])
            if emit_kv and CB_CV <= blk < CB_CV + 4:
                for s in range(x_ref.shape[0] // seq_len):
                    cv_ref[s, blk - CB_CV] = y[s * seq_len:(s + 1) * seq_len, :]
            if rope and blk in ROPE_BLOCKS:
                y = (y * cos_ref[...] + pltpu.roll(y, LANES - 16, 1) * sa_ref[...]
                     + pltpu.roll(y, 16, 1) * sb_ref[...])
            if blk in SCALED_BLOCKS:
                y = y * QK_SCALE
            if blk in SOFTMAX_Q_BLOCKS:
                y = y * (QK_SCALE * LOG2E)
            proj_ref[:, blk * LANES:(blk + 1) * LANES] = y.astype(BF16)

    r = _dot(h, wa1_ref[...]).astype(BF16)
    z = _dot(r, wa2_ref[...]) + ba_ref[...]
    la_ref[...] = _log_sigmoid(z) * (1.0 / GLA_TAU)


def _proj_call(x, mod, l, row0, span, seq_len, wts, tables, emit_kv, kv_prev):
    n = x.shape[0]
    tb = TOKEN_BLOCK
    bps = max(seq_len // tb, 1)
    rope = tables is not None

    in_specs = [pl.BlockSpec((tb, D_MODEL), lambda i: (i, 0)),
                _mod_spec(l, row0, span, 0), _mod_spec(l, row0, span, 1),
                _resident((D_MODEL, D_IN), l), _resident((D_MODEL, LANES), l),
                _resident((LANES, 4 * LANES), l), _resident((1, 4 * LANES), l)]
    args = [x, mod, mod, wts['w_in'], wts['wa1'], wts['wa2'], wts['ba']]
    if rope:
        in_specs += [pl.BlockSpec((tb, LANES), lambda i: (i % bps, 0))] * 3
        args += list(tables)
    aliases = {}
    if kv_prev is not None:
        aliases = {len(args): 2, len(args) + 1: 3}
        in_specs += [pl.BlockSpec(memory_space=pl.ANY)] * 2
        args += list(kv_prev)
    out_specs = [pl.BlockSpec((tb, D_IN), lambda i: (i, 0)),
                 pl.BlockSpec((tb, 4 * LANES), lambda i: (i, 0))]
    out_shape = [jax.ShapeDtypeStruct((n, D_IN), BF16), jax.ShapeDtypeStruct((n, 4 * LANES), F32)]
    if emit_kv:
        spb = tb // seq_len
        nseq = n // seq_len
        out_specs += [pl.BlockSpec((spb, None, 2 * N_HEAD, seq_len, DK), lambda i: (i, l, 0, 0, 0)),
                      pl.BlockSpec((spb, None, N_HEAD, seq_len, DV), lambda i: (i, l, 0, 0, 0))]
        out_shape += [jax.ShapeDtypeStruct((nseq, DEPTH, 2 * N_HEAD, seq_len, DK), F32),
                      jax.ShapeDtypeStruct((nseq, DEPTH, N_HEAD, seq_len, DV), F32)]
    return pl.pallas_call(
        functools.partial(_proj_kernel, rope=rope, emit_kv=emit_kv, n_alias=len(aliases),
                          seq_len=seq_len),
        grid=(n // tb,), in_specs=in_specs, out_specs=out_specs, out_shape=out_shape,
        input_output_aliases=aliases,
        compiler_params=_cparams(("arbitrary",)),
    )(*args)


def _ret_kernel(*refs, layer, seq, chunk, has_ctx, n_alias):
    dec_ref, q_ref, k_ref, v_ref, g_ref = refs[:5]
    pos = 5
    if has_ctx:
        s0_ref = refs[pos]
        pos += 1
    pos += n_alias
    y_ref = refs[pos]
    pos += 1
    if not has_ctx:
        st_ref = refs[pos]
        pos += 1
    of_ref, ob_ref, s_ref, dm_ref, eq_ref, ek_ref, gc_ref = refs[pos:]
    p = pl.program_id(1)
    nc = seq // chunk
    c_f = float(chunk)
    lane = lax.broadcasted_iota(jnp.int32, (1, LANES), 1)
    hmask = [lane < DK, lane >= DK]
    rowp = lax.broadcasted_iota(jnp.int32, (chunk, 1), 0).astype(F32)
    colp = lax.broadcasted_iota(jnp.int32, (1, chunk), 1).astype(F32)
    diff = rowp - colp

    for d in range(2):
        for hh in range(2):
            ci = 2 * d + hh
            raw = dec_ref[layer, d, 2 * p + hh]
            lg_l = _log_sigmoid(jnp.full((1, LANES), raw, F32))
            lg_c = _log_sigmoid(jnp.full((1, chunk), raw, F32))
            if d == 0:
                dm_ref[ci] = jnp.where(diff >= 0, jnp.exp(jnp.maximum(diff, 0.0) * lg_c), 0.0)
                eq_ref[ci] = jnp.exp((rowp + 1.0) * lg_l)
                ek_ref[ci] = jnp.exp((c_f - 1.0 - rowp) * lg_l)
            else:
                dm_ref[ci] = jnp.where(diff <= 0, jnp.exp(jnp.maximum(-diff, 0.0) * lg_c), 0.0)
                eq_ref[ci] = jnp.exp((c_f - rowp) * lg_l)
                ek_ref[ci] = jnp.exp(rowp * lg_l)
            gc_ref[ci] = jnp.exp(c_f * lg_l)
            if has_ctx:
                s0 = s0_ref[d, hh]
                zero = jnp.zeros((DK, DV), F32)
                s_ref[ci] = jnp.concatenate([s0, zero] if hh == 0 else [zero, s0], axis=0)
            else:
                s_ref[ci] = jnp.zeros((LANES, DV), F32)

    def step(n, carry):
        pending = []
        for d in range(2):
            c = n if d == 0 else nc - 1 - n
            r0 = pl.multiple_of(c * chunk, chunk)
            qc = q_ref[pl.ds(r0, chunk), :]
            kc = k_ref[pl.ds(r0, chunk), :]
            for hh in range(2):
                ci = 2 * d + hh
                qh = jnp.where(hmask[hh], qc, jnp.zeros_like(qc))
                kh = jnp.where(hmask[hh], kc, jnp.zeros_like(kc))
                vh = v_ref[pl.ds(r0, chunk), hh * DV:(hh + 1) * DV]
                sc = _dot_nt(qh, kh)
                sb = s_ref[ci]
                qd = (qh.astype(F32) * eq_ref[ci]).astype(BF16)
                inter = _dot(qd, sb.astype(BF16))
                kd = (kh.astype(F32) * ek_ref[ci]).astype(BF16)
                s_ref[ci] = gc_ref[ci] * sb + _dot_tn(kd, vh)
                pending.append((d, hh, r0, sc, inter, vh))
        for d, hh, r0, sc, inter, vh in pending:
            o_ref = of_ref if d == 0 else ob_ref
            intra = _dot((sc * dm_ref[2 * d + hh]).astype(BF16), vh)
            o_ref[hh, pl.ds(r0, chunk), :] = intra + inter
        return carry

    lax.fori_loop(0, nc, step, 0)

    if not has_ctx:
        for d in range(2):
            for hh in range(2):
                st_ref[d, hh] = s_ref[2 * d + hh][hh * DK:(hh + 1) * DK, :]

    def finish(n, carry):
        r0 = pl.multiple_of(n * chunk, chunk)
        for hh in range(2):
            o = of_ref[hh, pl.ds(r0, chunk), :] + ob_ref[hh, pl.ds(r0, chunk), :]
            mu = jnp.mean(o, axis=-1, keepdims=True)
            oc = o - mu
            var = jnp.mean(oc * oc, axis=-1, keepdims=True)
            g = g_ref[pl.ds(r0, chunk), hh * DV:(hh + 1) * DV].astype(F32)
            y = oc * lax.rsqrt(var + EPS) * (g * _sigmoid(g))
            y_ref[pl.ds(r0, chunk), hh * DV:(hh + 1) * DV] = y.astype(BF16)
        return carry

    lax.fori_loop(0, nc, finish, 0)


def _state_io(l, batch, s0_all, st_prev, n_in):
    spec = pl.BlockSpec((None, None, 2, 2, DK, DV), lambda i, p: (i, l, 0, p, 0, 0))
    if s0_all is not None:
        return [spec], [s0_all], [], [], {}
    shape = jax.ShapeDtypeStruct((batch, DEPTH, 2, N_HEAD, DK, DV), F32)
    if st_prev is None:
        return [], [], [spec], [shape], {}
    return [pl.BlockSpec(memory_space=pl.ANY)], [st_prev], [spec], [shape], {n_in: 1}


def _ret_call(proj3, decay, l, s0_all, st_prev):
    b, t, _ = proj3.shape
    chunk = min(RET_CHUNK, t)
    has_ctx = s0_all is not None
    in_specs = [pl.BlockSpec(memory_space=pltpu.SMEM),
                pl.BlockSpec((None, t, LANES), lambda i, p: (i, 0, CB_AQ + p)),
                pl.BlockSpec((None, t, LANES), lambda i, p: (i, 0, CB_AK + p)),
                pl.BlockSpec((None, t, 2 * DV), lambda i, p: (i, 0, CB_AV // 2 + p)),
                pl.BlockSpec((None, t, 2 * DV), lambda i, p: (i, 0, CB_AG // 2 + p))]
    args = [decay, proj3, proj3, proj3, proj3]
    st_in, st_args, st_out, st_shape, aliases = _state_io(l, b, s0_all, st_prev, len(args))
    return pl.pallas_call(
        functools.partial(_ret_kernel, layer=l, seq=t, chunk=chunk, has_ctx=has_ctx,
                          n_alias=len(aliases)),
        grid=(b, 2), in_specs=in_specs + st_in,
        out_specs=[pl.BlockSpec((None, t, 2 * DV), lambda i, p: (i, 0, p))] + st_out,
        out_shape=[jax.ShapeDtypeStruct((b, t, N_HEAD * DV), BF16)] + st_shape,
        input_output_aliases=aliases,
        scratch_shapes=[pltpu.VMEM((2, t, DV), F32), pltpu.VMEM((2, t, DV), F32),
                        pltpu.VMEM((4, LANES, DV), F32), pltpu.VMEM((4, chunk, chunk), F32),
                        pltpu.VMEM((4, chunk, LANES), F32), pltpu.VMEM((4, chunk, LANES), F32),
                        pltpu.VMEM((4, 1, LANES), F32)],
        compiler_params=_cparams(("arbitrary", "arbitrary")),
    )(*(args + st_args))


def _gla_levels(blk, reverse):
    t = lax.broadcasted_iota(jnp.int32, (blk, blk), 0)
    s = lax.broadcasted_iota(jnp.int32, (blk, blk), 1)
    xr = t ^ s
    lvl = jnp.zeros((blk, blk), jnp.int32)
    h = 2
    while h < blk:
        lvl = lvl + jnp.where(xr >= h, 1, 0)
        h *= 2
    allowed = (s > t) if reverse else (s < t)
    return jnp.where(t == s, -1, jnp.where(allowed, lvl, -2))


def _gla_block(q, k, la, v_pair, st_list, hmask, lvl, blk, reverse):
    row = lax.broadcasted_iota(jnp.int32, (blk, 1), 0)
    w = la
    tot = la
    def both_heads(qx):
        return jnp.concatenate([jnp.where(m, qx, jnp.zeros_like(qx)) for m in hmask], axis=0)

    lvl2 = jnp.concatenate([lvl, lvl], axis=0)
    amat = jnp.where(lvl2 == -1, _dot_nt(both_heads(q.astype(BF16)), k.astype(BF16)), 0.0)
    h = 1
    li = 0
    while h < blk:
        up = (row & h) != 0
        qside = jnp.logical_not(up) if reverse else up
        f = jnp.exp(jnp.where(qside, w, tot - w))
        ql = (q * f).astype(BF16)
        kl = (k * f).astype(BF16)
        amat = jnp.where(lvl2 == li, _dot_nt(both_heads(ql), kl), amat)
        partner = jnp.where(up, pltpu.roll(tot, h, 0), pltpu.roll(tot, blk - h, 0))
        w = w + jnp.where(qside, partner, 0.0)
        tot = tot + partner
        h *= 2
        li += 1
    qd = (q * jnp.exp(w)).astype(BF16)
    kd = (k * jnp.exp(tot - w)).astype(BF16)
    dec = jnp.exp(tot[0:1, :])
    outs, new_st = [], []
    for hh in range(2):
        vh = v_pair[:, hh * DV:(hh + 1) * DV]
        st = st_list[hh]
        qh = jnp.where(hmask[hh], qd, jnp.zeros_like(qd))
        kh = jnp.where(hmask[hh], kd, jnp.zeros_like(kd))
        inter = _dot_nt(qh, st.astype(BF16))
        new_st.append(dec * st + _dot_tn(vh, kh))
        outs.append(_dot(amat[hh * blk:(hh + 1) * blk].astype(BF16), vh) + inter)
    return outs, new_st


def _gla_kernel(*refs, seq, blk, has_ctx, n_alias):
    q_ref, k_ref, v_ref, r_ref, laf_ref, lab_ref, ng_ref = refs[:7]
    pos = 7
    if has_ctx:
        s0_ref = refs[pos]
        pos += 1
    pos += n_alias
    y_ref = refs[pos]
    pos += 1
    if not has_ctx:
        st_ref = refs[pos]
        pos += 1
    of_ref, ob_ref, s_ref, lv_ref = refs[pos:]
    nb = seq // blk
    lane = lax.broadcasted_iota(jnp.int32, (1, LANES), 1)
    hmask = [lane < DK, lane >= DK]

    for d in range(2):
        lv_ref[d] = _gla_levels(blk, d == 1)
        for hh in range(2):
            if has_ctx:
                zero = jnp.zeros((DK, DV), F32)
                s0 = s0_ref[d, hh]
                full = jnp.concatenate([s0, zero] if hh == 0 else [zero, s0], axis=0)
                s_ref[2 * d + hh] = full.T
            else:
                s_ref[2 * d + hh] = jnp.zeros((DV, LANES), F32)

    def step(n, carry):
        for d in range(2):
            c = n if d == 0 else nb - 1 - n
            r0 = pl.multiple_of(c * blk, blk)
            q = q_ref[pl.ds(r0, blk), :].astype(F32)
            k = k_ref[pl.ds(r0, blk), :].astype(F32)
            la_ref = laf_ref if d == 0 else lab_ref
            la = la_ref[pl.ds(r0, blk), :]
            vp = v_ref[pl.ds(r0, blk), :]
            sts = [s_ref[2 * d], s_ref[2 * d + 1]]
            outs, new_st = _gla_block(q, k, la, vp, sts, hmask, lv_ref[d], blk, d == 1)
            o_ref = of_ref if d == 0 else ob_ref
            for hh in range(2):
                o_ref[hh, pl.ds(r0, blk), :] = outs[hh]
                s_ref[2 * d + hh] = new_st[hh]
        return carry

    lax.fori_loop(0, nb, step, 0)

    if not has_ctx:
        for d in range(2):
            for hh in range(2):
                st_ref[d, hh] = s_ref[2 * d + hh][...].T[hh * DK:(hh + 1) * DK, :]

    fin = min(seq, 512)

    def finish(n, carry):
        r0 = pl.multiple_of(n * fin, fin)
        for hh in range(2):
            o = of_ref[hh, pl.ds(r0, fin), :] + ob_ref[hh, pl.ds(r0, fin), :]
            ms = jnp.mean(o * o, axis=-1, keepdims=True)
            g = r_ref[pl.ds(r0, fin), hh * DV:(hh + 1) * DV].astype(F32)
            y = o * lax.rsqrt(ms + EPS) * ng_ref[...] * (g * _sigmoid(g))
            y_ref[pl.ds(r0, fin), hh * DV:(hh + 1) * DV] = y.astype(BF16)
        return carry

    lax.fori_loop(0, seq // fin, finish, 0)


def _gla_call(proj3, la3, norm_g, l, s0_all, st_prev):
    b, t, _ = proj3.shape
    blk = GLA_BLOCK
    has_ctx = s0_all is not None
    in_specs = [pl.BlockSpec((None, t, LANES), lambda i, p: (i, 0, CB_BQ + p)),
                pl.BlockSpec((None, t, LANES), lambda i, p: (i, 0, CB_BK + p)),
                pl.BlockSpec((None, t, 2 * DV), lambda i, p: (i, 0, CB_BV // 2 + p)),
                pl.BlockSpec((None, t, 2 * DV), lambda i, p: (i, 0, CB_BR // 2 + p)),
                pl.BlockSpec((None, t, LANES), lambda i, p: (i, 0, p)),
                pl.BlockSpec((None, t, LANES), lambda i, p: (i, 0, 2 + p)),
                pl.BlockSpec((None, 1, DV), lambda i, p: (l, 0, 0))]
    args = [proj3, proj3, proj3, proj3, la3, la3, norm_g]
    st_in, st_args, st_out, st_shape, aliases = _state_io(l, b, s0_all, st_prev, len(args))
    return pl.pallas_call(
        functools.partial(_gla_kernel, seq=t, blk=blk, has_ctx=has_ctx, n_alias=len(aliases)),
        grid=(b, 2), in_specs=in_specs + st_in,
        out_specs=[pl.BlockSpec((None, t, 2 * DV), lambda i, p: (i, 0, p))] + st_out,
        out_shape=[jax.ShapeDtypeStruct((b, t, N_HEAD * DV), BF16)] + st_shape,
        input_output_aliases=aliases,
        scratch_shapes=[pltpu.VMEM((2, t, DV), F32), pltpu.VMEM((2, t, DV), F32),
                        pltpu.VMEM((4, DV, LANES), F32), pltpu.VMEM((2, blk, blk), jnp.int32)],
        compiler_params=_cparams(("arbitrary", "arbitrary")),
    )(*(args + st_args))


def _diff_kernel(*refs, lam_init, n_cache, n_new):
    lam_ref, q_ref = refs[:2]
    pos = 2
    if n_cache:
        kc_ref, vc_ref = refs[pos:pos + 2]
        pos += 2
    kn_ref, vn_ref, sg_ref, y_ref = refs[pos:pos + 4]
    pos += 4
    if n_cache:
        vtc_ref = refs[pos]
        pos += 1
    vtn_ref, sa_ref, sb_ref, acc_ref = refs[pos:]
    qb = q_ref.shape[0]
    kt = min(n_new, DIFF_KEY_TILE)
    n_tiles = n_new // kt

    @pl.when(pl.program_id(2) == 0)
    def _():
        if n_cache:
            vtc_ref[0:DV, :] = vc_ref[...].T
            vtc_ref[DV:, :] = jnp.ones((DIFF_ONES_ROWS, n_cache), BF16)
        for r in range(n_tiles):
            vtn_ref[r, 0:DV, :] = vn_ref[r * kt:(r + 1) * kt, :].T
            vtn_ref[r, DV:, :] = jnp.ones((DIFF_ONES_ROWS, kt), BF16)

    lp = lam_ref[...]
    lam = (jnp.exp(jnp.sum(lp[0:1] * lp[1:2], axis=-1, keepdims=True))
           - jnp.exp(jnp.sum(lp[2:3] * lp[3:4], axis=-1, keepdims=True)) + lam_init)
    lane = lax.broadcasted_iota(jnp.int32, (1, LANES), 1)
    q = q_ref[...]
    qs = [jnp.where((lane < DK) if i == 0 else (lane >= DK), q, jnp.zeros_like(q)) for i in range(2)]

    s_bufs = (sa_ref, sb_ref)
    new0 = 1 if n_cache else 0
    last = new0 + n_tiles - 1

    def tile_rows(idx):
        return n_cache if idx < new0 else kt

    def scores(idx, r=None):
        if idx < new0:
            k_tile = kc_ref[...]
        elif r is None:
            k_tile = kn_ref[(idx - new0) * kt:(idx - new0 + 1) * kt, :]
        else:
            k_tile = kn_ref[pl.ds(pl.multiple_of(r * kt, kt), kt), :]
        for i in range(2):
            s_bufs[idx % 2][i, 0:tile_rows(idx), :] = _dot_nt(k_tile, qs[i])

    def soft(idx, m):
        n = tile_rows(idx)
        m_out, es, alphas = [], [], []
        for i in range(2):
            s = s_bufs[idx % 2][i, 0:n, :]
            m_new = jnp.maximum(m[i], jnp.max(s, axis=0, keepdims=True))
            es.append(jnp.exp2(s - m_new).astype(BF16))
            alphas.append(jnp.exp2(m[i] - m_new))
            m_out.append(m_new)
        return m_out, (es, alphas)

    def pv(idx, soft_out, r=None):
        es, alphas = soft_out
        if idx < new0:
            vt_tile = vtc_ref[...]
        else:
            vt_tile = vtn_ref[idx - new0 if r is None else r]
        for i in range(2):
            part = _dot(vt_tile, es[i])
            acc_ref[i] = part if idx == 0 else acc_ref[i] * alphas[i] + part

    m = [jnp.full((1, qb), -1e30, F32) for _ in range(2)]
    scores(0)
    if last >= 1:
        scores(1)
    m, so = soft(0, m)
    pv(0, so)
    n_pairs = max(last - 1, 0) // 2
    loop_end = 1 + 2 * n_pairs

    def pair(p, carry):
        m = list(carry)
        r = 2 * p + (1 - new0)
        scores(2, r + 1)
        m, so1 = soft(1, m)
        scores(3, r + 2)
        pv(1, so1, r)
        m, so2 = soft(2, m)
        pv(2, so2, r + 1)
        return tuple(m)

    if n_pairs:
        m = list(lax.fori_loop(0, n_pairs, pair, tuple(m)))
    for idx in range(loop_end, last + 1):
        if idx + 1 <= last:
            scores(idx + 1)
        m, so = soft(idx, m)
        pv(idx, so)

    acc0 = acc_ref[0]
    acc1 = acc_ref[1]
    o = (acc0[0:DV] * (1.0 / acc0[DV:DV + 1]) - acc1[0:DV] * (lam / acc1[DV:DV + 1]))
    ms = jnp.mean(o * o, axis=0, keepdims=True)
    y = o * lax.rsqrt(ms + EPS) * (sg_ref[...] * (1.0 - lam_init))
    y_ref[...] = y.T.astype(BF16)


def _diff_call(proj3, cache_k, cache_v, lam_p, subln_col, l):
    b, t, _ = proj3.shape
    n_cache = 0 if cache_k is None else cache_k.shape[1]
    qb = min(Q_BLOCK, t)
    kt = min(t, DIFF_KEY_TILE)
    vrows = DV + DIFF_ONES_ROWS
    lam_init = 0.8 - 0.6 * math.exp(-0.3 * l)
    in_specs = [pl.BlockSpec((None, 4, DK), lambda i, h, j: (l, 0, 0)),
                pl.BlockSpec((None, qb, LANES), lambda i, h, j: (i, j, CB_CQ + h))]
    args = [lam_p, proj3]
    if n_cache:
        in_specs += [pl.BlockSpec((None, n_cache, LANES), lambda i, h, j: (i, 0, h)),
                     pl.BlockSpec((None, n_cache, DV), lambda i, h, j: (i, 0, h))]
        args += [cache_k, cache_v]
    in_specs += [pl.BlockSpec((None, t, LANES), lambda i, h, j: (i, 0, CB_CK + h)),
                 pl.BlockSpec((None, t, DV), lambda i, h, j: (i, 0, CB_CV + h)),
                 pl.BlockSpec((None, DV, 1), lambda i, h, j: (l, 0, 0))]
    args += [proj3, proj3, subln_col]
    return pl.pallas_call(
        functools.partial(_diff_kernel, lam_init=lam_init, n_cache=n_cache, n_new=t),
        grid=(b, N_HEAD, t // qb), in_specs=in_specs,
        out_specs=pl.BlockSpec((None, qb, DV), lambda i, h, j: (i, j, h)),
        out_shape=jax.ShapeDtypeStruct((b, t, N_HEAD * DV), BF16),
        scratch_shapes=([pltpu.VMEM((vrows, n_cache), BF16)] if n_cache else [])
        + [pltpu.VMEM((t // kt, vrows, kt), BF16), pltpu.VMEM((2, max(kt, n_cache), qb), F32),
           pltpu.VMEM((2, max(kt, n_cache), qb), F32), pltpu.VMEM((2, vrows, qb), F32)],
        compiler_params=_cparams(("arbitrary", "arbitrary", "arbitrary")),
    )(*args)


def _layer_norm(z, g, b):
    mu = jnp.mean(z, axis=-1, keepdims=True)
    zc = z - mu
    var = jnp.mean(zc * zc, axis=-1, keepdims=True)
    return zc * lax.rsqrt(var + EPS) * g + b


def _merge_kernel(x_ref, ya_ref, yb_ref, yc_ref, m_ref, g1_ref, wb_ref, wo_ref, lg_ref, lb_ref, o_ref):
    half = x_ref.shape[0] // 2
    merged = []
    for p in range(2):
        rs = slice(p * half, (p + 1) * half)
        acc = None
        for i, y_ref in enumerate((ya_ref, yb_ref, yc_ref)):
            gate = _sigmoid(m_ref[rs, i * D_MODEL:(i + 1) * D_MODEL].astype(F32))
            term = gate * _dot(y_ref[rs, :], wb_ref[i])
            acc = term if acc is None else acc + term
        merged.append(acc.astype(BF16))
    for p in range(2):
        rs = slice(p * half, (p + 1) * half)
        out = _dot(merged[p], wo_ref[...])
        z = ALPHA * x_ref[rs, :] + g1_ref[...] * out
        o_ref[rs, :] = _layer_norm(z, lg_ref[...], lb_ref[...])


def _merge_call(x, ya, yb, yc, proj, mod, l, row0, span, wts):
    n = x.shape[0]
    tb = TOKEN_BLOCK
    bw = N_HEAD * DV
    tok = lambda w: pl.BlockSpec((tb, w), lambda i: (i, 0))
    return pl.pallas_call(
        _merge_kernel,
        grid=(n // tb,),
        in_specs=[tok(D_MODEL), tok(bw), tok(bw), tok(bw),
                  pl.BlockSpec((tb, N_GATE_COLS), lambda i: (i, 0)),
                  _mod_spec(l, row0, span, 2),
                  _resident((3, bw, D_MODEL), l), _resident((D_MODEL, D_MODEL), l),
                  _resident((1, D_MODEL), l, 0), _resident((1, D_MODEL), l, 0)],
        out_specs=tok(D_MODEL),
        out_shape=jax.ShapeDtypeStruct((n, D_MODEL), F32),
        compiler_params=_cparams(("arbitrary",)),
    )(x, ya, yb, yc, proj, mod, wts['w_branch'], wts['w_out'], wts['ln_g'], wts['ln_b'])


def _gelu_tanh(x):
    k = -2.0 * 0.7978845608028654
    w = x * (x * x * (k * 0.044715) + k)
    return x / (1.0 + jnp.exp(w))


def _ffn_kernel(xp_ref, x_ref, xn_ref, sh_ref, sc_ref, g2_ref, wu_ref, cw_ref, cb_ref, wd_ref,
                lg_ref, lb_ref, o_ref, acc_ref, *, seq_len):
    tb = x_ref.shape[0]
    halo = SUBLANES
    i = pl.program_id(0)
    xm = x_ref[...]
    scale = 1.0 + sc_ref[...]
    shift = sh_ref[...]
    starts_seq = (i * tb) % seq_len == 0
    ends_seq = ((i + 1) * tb) % seq_len == 0
    h_prev = jnp.where(starts_seq, 0.0, xp_ref[...] * scale + shift)
    h_next = jnp.where(ends_seq, 0.0, xn_ref[...] * scale + shift)
    h_mid = xm * scale + shift
    hm = h_mid.astype(BF16)
    h = jnp.concatenate([h_prev, h_mid, h_next], axis=0).astype(BF16)
    rows = tb + 2 * halo
    n_chunk = D_FF // FF_CHUNK
    split = (rows // 2 + 15) // 16 * 16

    def up(c):
        w = wu_ref[:, 2 * c * FF_CHUNK:2 * (c + 1) * FF_CHUNK]
        return jnp.concatenate([_dot(h[:split], w), _dot(h[split:], w)], axis=0)

    u_next = up(0)
    for c in range(n_chunk):
        lo, hi = c * FF_CHUNK, (c + 1) * FF_CHUNK
        u = u_next
        if c + 1 < n_chunk:
            u_next = up(c + 1)
        a = u[:, :FF_CHUNK]
        b = u[halo:halo + tb, FF_CHUNK:]
        a_prev = pltpu.roll(a, 1, 0)[halo:halo + tb]
        a_mid = a[halo:halo + tb]
        a_next = pltpu.roll(a, rows - 1, 0)[halo:halo + tb]
        w0, w1, w2 = cw_ref[0:1, lo:hi], cw_ref[1:2, lo:hi], cw_ref[2:3, lo:hi]
        cb = cb_ref[:, lo:hi]
        cv = a_prev * w0 + a_mid * w1 + a_next * w2 + cb
        for r in range(seq_len, tb, seq_len):
            sl = slice(r - SUBLANES, r + SUBLANES)
            rid = lax.broadcasted_iota(jnp.int32, (2 * SUBLANES, 1), 0)
            seam = (jnp.where(rid == SUBLANES, 0.0, a_prev[sl]) * w0 + a_mid[sl] * w1
                    + jnp.where(rid == SUBLANES - 1, 0.0, a_next[sl]) * w2 + cb)
            cv = jnp.concatenate([cv[:r - SUBLANES], seam, cv[r + SUBLANES:]], axis=0)
        act = (_gelu_tanh(cv) * b).astype(BF16)
        part = _dot(act, wd_ref[lo:hi, :])
        if c == 0:
            acc_ref[...] = part
        else:
            acc_ref[...] += part
    z = ALPHA * xm + g2_ref[...] * acc_ref[...]
    o_ref[...] = _layer_norm(z, lg_ref[...], lb_ref[...])


def _ffn_call(x, mod, l, row0, span, seq_len, wts):
    n = x.shape[0]
    tb = TOKEN_BLOCK
    hb = tb // SUBLANES
    last = n // SUBLANES - 1
    modspec = functools.partial(_mod_spec, l, row0, span)

    return pl.pallas_call(
        functools.partial(_ffn_kernel, seq_len=seq_len),
        grid=(n // tb,),
        in_specs=[pl.BlockSpec((SUBLANES, D_MODEL), lambda i: (jnp.maximum(i * hb - 1, 0), 0)),
                  pl.BlockSpec((tb, D_MODEL), lambda i: (i, 0)),
                  pl.BlockSpec((SUBLANES, D_MODEL), lambda i: (jnp.minimum((i + 1) * hb, last), 0)),
                  modspec(3), modspec(4), modspec(5),
                  _resident((D_MODEL, 2 * D_FF), l), _resident((3, D_FF), l),
                  _resident((1, D_FF), l), _resident((D_FF, D_MODEL), l),
                  _resident((1, D_MODEL), l, 1), _resident((1, D_MODEL), l, 1)],
        out_specs=pl.BlockSpec((tb, D_MODEL), lambda i: (i, 0)),
        out_shape=jax.ShapeDtypeStruct((n, D_MODEL), F32),
        scratch_shapes=[pltpu.VMEM((tb, D_MODEL), F32)],
        compiler_params=_cparams(("arbitrary",)),
    )(x, x, x, mod, mod, mod, wts['w_up'], wts['conv_w'], wts['conv_b'], wts['w_down'],
      wts['ln_g'], wts['ln_b'])


def _rope_tables(seq):
    half = DK // 4
    t = np.arange(seq)
    inv = ROPE_BASE ** (-np.arange(half, dtype=np.float64) / half)
    ang_row = (t // GRID_W)[:, None] * inv[None, :]
    ang_col = (t % GRID_W)[:, None] * inv[None, :]
    ang = np.concatenate([ang_row, ang_row, ang_col, ang_col], axis=1)
    first = (np.arange(DK) % (2 * half)) < half
    cos = np.cos(ang)
    sa = np.where(first[None, :], -np.sin(ang), 0.0)
    sb = np.where(first[None, :], 0.0, np.sin(ang))
    tile = lambda a: jnp.asarray(np.concatenate([a, a], axis=1), F32)
    return tile(cos), tile(sa), tile(sb)


def _regroup_w_up(w_up):
    n_chunk = D_FF // FF_CHUNK
    w = w_up.reshape(DEPTH, D_MODEL, 2, n_chunk, FF_CHUNK)
    return w.transpose(0, 1, 3, 2, 4).reshape(DEPTH, D_MODEL, 2 * D_FF)


def _trunk(x, mod, row0, seq_len, batch, l, wts, tables, ctx, carried):
    n = x.shape[0]
    span = n if ctx is None else seq_len
    kv_prev = None if carried is None else carried[:2]
    outs = _proj_call(x, mod, l, row0, span, seq_len, wts, tables, ctx is None, kv_prev)
    proj, la = outs[0], outs[1]
    proj3 = proj.reshape(batch, seq_len, D_IN)
    la3 = la.reshape(batch, seq_len, 4 * LANES)
    if ctx is None:
        ret_prev, gla_prev = (None, None) if carried is None else carried[2:]
        ya, st_ret = _ret_call(proj3, wts['ret_decay'], l, None, ret_prev)
        yb, st_gla = _gla_call(proj3, la3, wts['gla_norm_g'], l, None, gla_prev)
        yc = _diff_call(proj3, None, None, wts['diff_lam'], wts['diff_subln_g'], l)
        carried = (outs[2], outs[3], st_ret, st_gla)
    else:
        ya, = _ret_call(proj3, wts['ret_decay'], l, ctx['ret'], None)
        yb, = _gla_call(proj3, la3, wts['gla_norm_g'], l, ctx['gla'], None)
        yc = _diff_call(proj3, ctx['dk'][l], ctx['dv'][l], wts['diff_lam'], wts['diff_subln_g'], l)
    bw = N_HEAD * DV
    x1 = _merge_call(x, ya.reshape(n, bw), yb.reshape(n, bw), yc.reshape(n, bw), proj, mod, l,
                     row0, span, wts)
    x2 = _ffn_call(x1, mod, l, row0, span, seq_len, wts)
    return x2, carried


def kernel(x_prompt, x_sample, cache_diff_k, cache_diff_v, state_ret, state_gla, c, c_ctx,
           ada_w, ada_b, w_in, ret_decay, gla_wa1, gla_wa2, gla_ba, gla_norm_g, diff_lam,
           diff_subln_g, w_branch, w_out, ln_g, ln_b, ffn_w_up, ffn_conv_w, ffn_conv_b, ffn_w_down):
    bp, tp, _ = x_prompt.shape
    bs, ts, _ = x_sample.shape
    past = cache_diff_k.shape[3]

    cvec = jnp.concatenate([c_ctx[None, :], c, jnp.zeros((SUBLANES - 1 - bs, D_MODEL), F32)], axis=0)
    mod = _ada_call(cvec, ada_w, ada_b).reshape(DEPTH, SUBLANES, 6, 1, D_MODEL)

    wa1 = jnp.concatenate([gla_wa1[:, 0], gla_wa1[:, 1]], axis=-1)
    wa1 = jnp.pad(wa1, ((0, 0), (0, 0), (0, LANES - 2 * GLA_RANK))).astype(BF16)
    wa2 = jnp.zeros((DEPTH, LANES, 4 * LANES), F32)
    wa2 = wa2.at[:, 0:GLA_RANK, 0:2 * LANES].set(gla_wa2[:, 0])
    wa2 = wa2.at[:, GLA_RANK:2 * GLA_RANK, 2 * LANES:].set(gla_wa2[:, 1]).astype(BF16)
    wts = {
        'w_in': w_in.astype(BF16), 'wa1': wa1, 'wa2': wa2,
        'ba': gla_ba.reshape(DEPTH, 1, 4 * LANES), 'ret_decay': ret_decay,
        'gla_norm_g': gla_norm_g.reshape(DEPTH, 1, DV), 'diff_lam': diff_lam,
        'diff_subln_g': diff_subln_g.reshape(DEPTH, DV, 1),
        'w_branch': w_branch.astype(BF16), 'w_out': w_out.astype(BF16),
        'ln_g': ln_g.reshape(DEPTH, 2, 1, D_MODEL), 'ln_b': ln_b.reshape(DEPTH, 2, 1, D_MODEL),
        'w_up': _regroup_w_up(ffn_w_up).astype(BF16), 'conv_w': ffn_conv_w,
        'conv_b': ffn_conv_b.reshape(DEPTH, 1, D_FF), 'w_down': ffn_w_down.astype(BF16),
    }

    h = x_prompt.reshape(bp * tp, D_MODEL)
    carried = None
    for l in range(DEPTH):
        h, carried = _trunk(h, mod, 0, tp, bp, l, wts, None, None, carried)
    y_prompt = h.reshape(bp, tp, D_MODEL)

    tables = _rope_tables(ts)
    ctx = {
        'dk': cache_diff_k.transpose(1, 0, 3, 2, 4).reshape(DEPTH, bs, past, 2 * N_HEAD * DK).astype(BF16),
        'dv': cache_diff_v.transpose(1, 0, 3, 2, 4).reshape(DEPTH, bs, past, N_HEAD * DV).astype(BF16),
        'ret': state_ret, 'gla': state_gla,
    }
    z = x_sample.reshape(bs * ts, D_MODEL)
    for l in range(DEPTH):
        z, _ = _trunk(z, mod, 1, ts, bs, l, wts, tables, ctx, None)
    y_sample = z.reshape(bs, ts, D_MODEL)

    return (y_prompt, y_sample) + tuple(carried)
```

```python
import functools
import math

import numpy as np
import jax
import jax.numpy as jnp
from jax import lax
from jax.experimental import pallas as pl
from jax.experimental.pallas import tpu as pltpu

F32 = jnp.float32
BF16 = jnp.bfloat16

D_MODEL = 1024
DEPTH = 2
GRID_W = 64
N_HEAD = 4
DK = 64
DV = 128
GLA_RANK = 16
GLA_TAU = 16.0
D_FF = 2816
ROPE_BASE = 10000.0
ALPHA = (2 * DEPTH) ** 0.25
EPS = 1e-5
D_IN = 7680

LANES = 128
SUBLANES = 8
VMEM_LIMIT = 56 * 1024 * 1024

N_GATE_COLS = 3 * D_MODEL
CB_M = 0
CB_AQ, CB_AK, CB_AV, CB_AG = 24, 26, 28, 32
CB_BQ, CB_BK, CB_BV, CB_BR = 36, 38, 40, 44
CB_CQ, CB_CK, CB_CV = 48, 52, 56
ROPE_BLOCKS = (24, 25, 26, 27, 48, 49, 50, 51, 52, 53, 54, 55)
SCALED_BLOCKS = (26, 27, 36, 37)
SOFTMAX_Q_BLOCKS = (48, 49, 50, 51)
QK_SCALE = DK ** -0.5
LOG2E = math.log2(math.e)

TOKEN_BLOCK = 512
RET_CHUNK = 256
GLA_BLOCK = 128
Q_BLOCK = 1024
DIFF_KEY_TILE = 512
DIFF_ONES_ROWS = 16
FF_CHUNK = 256


def _cparams(sem):
    return pltpu.CompilerParams(dimension_semantics=sem, vmem_limit_bytes=VMEM_LIMIT)


def _resident(shape, *lead):
    idx = tuple(lead) + (0,) * len(shape)
    return pl.BlockSpec((None,) * len(lead) + tuple(shape), lambda *_: idx,
                        pipeline_mode=pl.Buffered(1))


def _mod_spec(l, row0, span, k):
    return pl.BlockSpec((None, None, None, 1, D_MODEL),
                        lambda i: (l, row0 + (i * TOKEN_BLOCK) // span, k, 0, 0))


def _sigmoid(x):
    return 1.0 / (1.0 + jnp.exp(-x))


def _log_sigmoid(x):
    return jnp.minimum(x, 0.0) - jnp.log(1.0 + jnp.exp(-jnp.abs(x)))


def _dot(a, b):
    return jnp.dot(a, b, preferred_element_type=F32)


def _dot_nt(a, b):
    return lax.dot_general(a, b, (((1,), (1,)), ((), ())), preferred_element_type=F32)


def _dot_tn(a, b):
    return lax.dot_general(a, b, (((0,), (0,)), ((), ())), preferred_element_type=F32)


def _ada_kernel(c_ref, w_ref, b_ref, o_ref):
    c = c_ref[...]
    s = (c * _sigmoid(c)).astype(BF16)
    o_ref[...] = _dot(s, w_ref[...].astype(BF16)) + b_ref[...]


def _ada_call(cvec, ada_w, ada_b):
    nt = 1536
    return pl.pallas_call(
        _ada_kernel,
        grid=(DEPTH, 6 * D_MODEL // nt),
        in_specs=[pl.BlockSpec((SUBLANES, D_MODEL), lambda l, j: (0, 0)),
                  pl.BlockSpec((None, D_MODEL, nt), lambda l, j: (l, 0, j)),
                  pl.BlockSpec((None, 1, nt), lambda l, j: (l, 0, j))],
        out_specs=pl.BlockSpec((None, SUBLANES, nt), lambda l, j: (l, 0, j)),
        out_shape=jax.ShapeDtypeStruct((DEPTH, SUBLANES, 6 * D_MODEL), F32),
        compiler_params=_cparams(("arbitrary", "arbitrary")),
    )(cvec, ada_w, ada_b.reshape(DEPTH, 1, 6 * D_MODEL))


def _proj_kernel(*refs, rope, emit_kv, n_alias, seq_len):
    x_ref, sh_ref, sc_ref, w_ref, wa1_ref, wa2_ref, ba_ref = refs[:7]
    pos = 7
    if rope:
        cos_ref, sa_ref, sb_ref = refs[pos:pos + 3]
        pos += 3
    pos += n_alias
    proj_ref, la_ref = refs[pos:pos + 2]
    pos += 2
    if emit_kv:
        ck_ref, cv_ref = refs[pos:pos + 2]

    h = (x_ref[...] * (1.0 + sc_ref[...]) + sh_ref[...]).astype(BF16)
    tile = 512
    per = tile // LANES
    n_blk = D_IN // LANES
    for j in range(D_IN // tile):
        acc = _dot(h, w_ref[:, j * tile:(j + 1) * tile])
        for i in range(per):
            blk = (j * per + i + N_GATE_COLS // LANES) % n_blk
            y = acc[:, i * LANES:(i + 1) * LANES]
            if emit_kv and CB_CK <= blk < CB_CK + 4:
                for s in range(x_ref.shape[0] // seq_len):
                    for half in range(2):
                        ck_ref[s, 2 * (blk - CB_CK) + half] = (
                            y[s * seq_len:(s + 1) * seq_len, half * DK:(half + 1) * DK])
            if emit_kv and CB_CV <= blk < CB_CV + 4:
                for s in range(x_ref.shape[0] // seq_len):
                    cv_ref[s, blk - CB_CV] = y[s * seq_len:(s + 1) * seq_len, :]
            if rope and blk in ROPE_BLOCKS:
                y = (y * cos_ref[...] + pltpu.roll(y, LANES - 16, 1) * sa_ref[...]
                     + pltpu.roll(y, 16, 1) * sb_ref[...])
            if blk in SCALED_BLOCKS:
                y = y * QK_SCALE
            if blk in SOFTMAX_Q_BLOCKS:
                y = y * (QK_SCALE * LOG2E)
            proj_ref[:, blk * LANES:(blk + 1) * LANES] = y.astype(BF16)

    r = _dot(h, wa1_ref[...]).astype(BF16)
    z = _dot(r, wa2_ref[...]) + ba_ref[...]
    la_ref[...] = _log_sigmoid(z) * (1.0 / GLA_TAU)


def _proj_call(x, mod, l, row0, span, seq_len, wts, tables, emit_kv, kv_prev):
    n = x.shape[0]
    tb = TOKEN_BLOCK
    assert n % tb == 0 and (seq_len % tb == 0 or tb % seq_len == 0)
    bps = max(seq_len // tb, 1)
    rope = tables is not None

    in_specs = [pl.BlockSpec((tb, D_MODEL), lambda i: (i, 0)),
                _mod_spec(l, row0, span, 0), _mod_spec(l, row0, span, 1),
                _resident((D_MODEL, D_IN), l), _resident((D_MODEL, LANES), l),
                _resident((LANES, 4 * LANES), l), _resident((1, 4 * LANES), l)]
    args = [x, mod, mod, wts['w_in'], wts['wa1'], wts['wa2'], wts['ba']]
    if rope:
        in_specs += [pl.BlockSpec((tb, LANES), lambda i: (i % bps, 0))] * 3
        args += list(tables)
    aliases = {}
    if kv_prev is not None:
        aliases = {len(args): 2, len(args) + 1: 3}
        in_specs += [pl.BlockSpec(memory_space=pl.ANY)] * 2
        args += list(kv_prev)
    out_specs = [pl.BlockSpec((tb, D_IN), lambda i: (i, 0)),
                 pl.BlockSpec((tb, 4 * LANES), lambda i: (i, 0))]
    out_shape = [jax.ShapeDtypeStruct((n, D_IN), BF16), jax.ShapeDtypeStruct((n, 4 * LANES), F32)]
    if emit_kv:
        spb = tb // seq_len
        nseq = n // seq_len
        out_specs += [pl.BlockSpec((spb, None, 2 * N_HEAD, seq_len, DK), lambda i: (i, l, 0, 0, 0)),
                      pl.BlockSpec((spb, None, N_HEAD, seq_len, DV), lambda i: (i, l, 0, 0, 0))]
        out_shape += [jax.ShapeDtypeStruct((nseq, DEPTH, 2 * N_HEAD, seq_len, DK), F32),
                      jax.ShapeDtypeStruct((nseq, DEPTH, N_HEAD, seq_len, DV), F32)]
    return pl.pallas_call(
        functools.partial(_proj_kernel, rope=rope, emit_kv=emit_kv, n_alias=len(aliases),
                          seq_len=seq_len),
        grid=(n // tb,), in_specs=in_specs, out_specs=out_specs, out_shape=out_shape,
        input_output_aliases=aliases,
        compiler_params=_cparams(("arbitrary",)),
    )(*args)


def _ret_kernel(*refs, layer, seq, chunk, has_ctx, n_alias):
    dec_ref, q_ref, k_ref, v_ref, g_ref = refs[:5]
    pos = 5
    if has_ctx:
        s0_ref = refs[pos]
        pos += 1
    pos += n_alias
    y_ref = refs[pos]
    pos += 1
    if not has_ctx:
        st_ref = refs[pos]
        pos += 1
    of_ref, ob_ref, s_ref, dm_ref, eq_ref, ek_ref, gc_ref = refs[pos:]
    p = pl.program_id(1)
    nc = seq // chunk
    c_f = float(chunk)
    lane = lax.broadcasted_iota(jnp.int32, (1, LANES), 1)
    hmask = [lane < DK, lane >= DK]
    rowp = lax.broadcasted_iota(jnp.int32, (chunk, 1), 0).astype(F32)
    colp = lax.broadcasted_iota(jnp.int32, (1, chunk), 1).astype(F32)
    diff = rowp - colp

    @pl.when((pl.program_id(0) == 0) & (p == 0))
    def _():
        for pair in range(2):
            for d in range(2):
                for hh in range(2):
                    ti = 4 * pair + 2 * d + hh
                    raw = dec_ref[layer, d, 2 * pair + hh]
                    lg_l = _log_sigmoid(jnp.full((1, LANES), raw, F32))
                    lg_c = _log_sigmoid(jnp.full((1, chunk), raw, F32))
                    if d == 0:
                        dm_ref[ti] = jnp.where(diff >= 0, jnp.exp(jnp.maximum(diff, 0.0) * lg_c), 0.0)
                        eq_ref[ti] = jnp.exp((rowp + 1.0) * lg_l)
                        ek_ref[ti] = jnp.exp((c_f - 1.0 - rowp) * lg_l)
                    else:
                        dm_ref[ti] = jnp.where(diff <= 0, jnp.exp(jnp.maximum(-diff, 0.0) * lg_c), 0.0)
                        eq_ref[ti] = jnp.exp((c_f - rowp) * lg_l)
                        ek_ref[ti] = jnp.exp(rowp * lg_l)
                    gc_ref[ti] = jnp.exp(c_f * lg_l)

    for d in range(2):
        for hh in range(2):
            ci = 2 * d + hh
            if has_ctx:
                s0 = s0_ref[d, hh]
                zero = jnp.zeros((DK, DV), F32)
                s_ref[ci] = jnp.concatenate([s0, zero] if hh == 0 else [zero, s0], axis=0)
            else:
                s_ref[ci] = jnp.zeros((LANES, DV), F32)

    def step(n, carry):
        pending = []
        for d in range(2):
            c = n if d == 0 else nc - 1 - n
            r0 = pl.multiple_of(c * chunk, chunk)
            qc = q_ref[pl.ds(r0, chunk), :]
            kc = k_ref[pl.ds(r0, chunk), :]
            for hh in range(2):
                ci = 2 * d + hh
                qh = jnp.where(hmask[hh], qc, jnp.zeros_like(qc))
                kh = jnp.where(hmask[hh], kc, jnp.zeros_like(kc))
                vh = v_ref[pl.ds(r0, chunk), hh * DV:(hh + 1) * DV]
                sc = _dot_nt(qh, kh)
                sb = s_ref[ci]
                ti = 4 * p + ci
                qd = (qh.astype(F32) * eq_ref[ti]).astype(BF16)
                inter = _dot(qd, sb.astype(BF16))
                kd = (kh.astype(F32) * ek_ref[ti]).astype(BF16)
                s_ref[ci] = gc_ref[ti] * sb + _dot_tn(kd, vh)
                pending.append((d, hh, r0, sc, inter, vh))
        for d, hh, r0, sc, inter, vh in pending:
            o_ref = of_ref if d == 0 else ob_ref
            intra = _dot((sc * dm_ref[4 * p + 2 * d + hh]).astype(BF16), vh)
            o_ref[hh, pl.ds(r0, chunk), :] = intra + inter
        return carry

    lax.fori_loop(0, nc, step, 0)

    if not has_ctx:
        for d in range(2):
            for hh in range(2):
                st_ref[d, hh] = s_ref[2 * d + hh][hh * DK:(hh + 1) * DK, :]

    def finish(n, carry):
        r0 = pl.multiple_of(n * chunk, chunk)
        for hh in range(2):
            o = of_ref[hh, pl.ds(r0, chunk), :] + ob_ref[hh, pl.ds(r0, chunk), :]
            mu = jnp.mean(o, axis=-1, keepdims=True)
            oc = o - mu
            var = jnp.mean(oc * oc, axis=-1, keepdims=True)
            g = g_ref[pl.ds(r0, chunk), hh * DV:(hh + 1) * DV].astype(F32)
            y = oc * lax.rsqrt(var + EPS) * (g * _sigmoid(g))
            y_ref[pl.ds(r0, chunk), hh * DV:(hh + 1) * DV] = y.astype(BF16)
        return carry

    lax.fori_loop(0, nc, finish, 0)


def _state_io(l, batch, s0_all, st_prev, n_in):
    spec = pl.BlockSpec((None, None, 2, 2, DK, DV), lambda i, p: (i, l, 0, p, 0, 0))
    if s0_all is not None:
        return [spec], [s0_all], [], [], {}
    shape = jax.ShapeDtypeStruct((batch, DEPTH, 2, N_HEAD, DK, DV), F32)
    return [pl.BlockSpec(memory_space=pl.ANY)], [st_prev], [spec], [shape], {n_in: 1}


def _ret_call(proj3, decay, l, s0_all, st_prev):
    b, t, _ = proj3.shape
    chunk = min(RET_CHUNK, t)
    assert t % chunk == 0
    has_ctx = s0_all is not None
    in_specs = [pl.BlockSpec(memory_space=pltpu.SMEM),
                pl.BlockSpec((None, t, LANES), lambda i, p: (i, 0, CB_AQ + p)),
                pl.BlockSpec((None, t, LANES), lambda i, p: (i, 0, CB_AK + p)),
                pl.BlockSpec((None, t, 2 * DV), lambda i, p: (i, 0, CB_AV // 2 + p)),
                pl.BlockSpec((None, t, 2 * DV), lambda i, p: (i, 0, CB_AG // 2 + p))]
    args = [decay, proj3, proj3, proj3, proj3]
    st_in, st_args, st_out, st_shape, aliases = _state_io(l, b, s0_all, st_prev, len(args))
    return pl.pallas_call(
        functools.partial(_ret_kernel, layer=l, seq=t, chunk=chunk, has_ctx=has_ctx,
                          n_alias=len(aliases)),
        grid=(b, 2), in_specs=in_specs + st_in,
        out_specs=[pl.BlockSpec((None, t, 2 * DV), lambda i, p: (i, 0, p))] + st_out,
        out_shape=[jax.ShapeDtypeStruct((b, t, N_HEAD * DV), BF16)] + st_shape,
        input_output_aliases=aliases,
        scratch_shapes=[pltpu.VMEM((2, t, DV), F32), pltpu.VMEM((2, t, DV), F32),
                        pltpu.VMEM((4, LANES, DV), F32), pltpu.VMEM((8, chunk, chunk), F32),
                        pltpu.VMEM((8, chunk, LANES), F32), pltpu.VMEM((8, chunk, LANES), F32),
                        pltpu.VMEM((8, 1, LANES), F32)],
        compiler_params=_cparams(("arbitrary", "arbitrary")),
    )(*(args + st_args))


def _gla_levels(blk, reverse):
    t = lax.broadcasted_iota(jnp.int32, (blk, blk), 0)
    s = lax.broadcasted_iota(jnp.int32, (blk, blk), 1)
    xr = t ^ s
    lvl = jnp.zeros((blk, blk), jnp.int32)
    h = 2
    while h < blk:
        lvl = lvl + jnp.where(xr >= h, 1, 0)
        h *= 2
    allowed = (s > t) if reverse else (s < t)
    return jnp.where(t == s, -1, jnp.where(allowed, lvl, -2))


def _gla_block(q, k, la, v_pair, st_list, hmask, lvl, blk, reverse):
    row = lax.broadcasted_iota(jnp.int32, (blk, 1), 0)
    w = la
    tot = la
    def both_heads(qx):
        return jnp.concatenate([jnp.where(m, qx, jnp.zeros_like(qx)) for m in hmask], axis=0)

    lvl2 = jnp.concatenate([lvl, lvl], axis=0)
    amat = jnp.where(lvl2 == -1, _dot_nt(both_heads(q.astype(BF16)), k.astype(BF16)), 0.0)
    h = 1
    li = 0
    while h < blk:
        up = (row & h) != 0
        qside = jnp.logical_not(up) if reverse else up
        f = jnp.exp(jnp.where(qside, w, tot - w))
        ql = (q * f).astype(BF16)
        kl = (k * f).astype(BF16)
        amat = jnp.where(lvl2 == li, _dot_nt(both_heads(ql), kl), amat)
        partner = jnp.where(up, pltpu.roll(tot, h, 0), pltpu.roll(tot, blk - h, 0))
        w = w + jnp.where(qside, partner, 0.0)
        tot = tot + partner
        h *= 2
        li += 1
    qd = (q * jnp.exp(w)).astype(BF16)
    kd = (k * jnp.exp(tot - w)).astype(BF16)
    dec = jnp.exp(tot[0:1, :])
    outs, new_st = [], []
    for hh in range(2):
        vh = v_pair[:, hh * DV:(hh + 1) * DV]
        st = st_list[hh]
        qh = jnp.where(hmask[hh], qd, jnp.zeros_like(qd))
        kh = jnp.where(hmask[hh], kd, jnp.zeros_like(kd))
        inter = _dot_nt(qh, st.astype(BF16))
        new_st.append(dec * st + _dot_tn(vh, kh))
        outs.append(_dot(amat[hh * blk:(hh + 1) * blk].astype(BF16), vh) + inter)
    return outs, new_st


def _gla_kernel(*refs, seq, blk, has_ctx, n_alias):
    q_ref, k_ref, v_ref, r_ref, laf_ref, lab_ref, ng_ref = refs[:7]
    pos = 7
    if has_ctx:
        s0_ref = refs[pos]
        pos += 1
    pos += n_alias
    y_ref = refs[pos]
    pos += 1
    if not has_ctx:
        st_ref = refs[pos]
        pos += 1
    of_ref, ob_ref, s_ref, lv_ref = refs[pos:]
    nb = seq // blk
    lane = lax.broadcasted_iota(jnp.int32, (1, LANES), 1)
    hmask = [lane < DK, lane >= DK]

    @pl.when((pl.program_id(0) == 0) & (pl.program_id(1) == 0))
    def _():
        for d in range(2):
            lv_ref[d] = _gla_levels(blk, d == 1)

    for d in range(2):
        for hh in range(2):
            if has_ctx:
                zero = jnp.zeros((DK, DV), F32)
                s0 = s0_ref[d, hh]
                full = jnp.concatenate([s0, zero] if hh == 0 else [zero, s0], axis=0)
                s_ref[2 * d + hh] = full.T
            else:
                s_ref[2 * d + hh] = jnp.zeros((DV, LANES), F32)

    def step(n, carry):
        for d in range(2):
            c = n if d == 0 else nb - 1 - n
            r0 = pl.multiple_of(c * blk, blk)
            q = q_ref[pl.ds(r0, blk), :].astype(F32)
            k = k_ref[pl.ds(r0, blk), :].astype(F32)
            la_ref = laf_ref if d == 0 else lab_ref
            la = la_ref[pl.ds(r0, blk), :]
            vp = v_ref[pl.ds(r0, blk), :]
            sts = [s_ref[2 * d], s_ref[2 * d + 1]]
            outs, new_st = _gla_block(q, k, la, vp, sts, hmask, lv_ref[d], blk, d == 1)
            o_ref = of_ref if d == 0 else ob_ref
            for hh in range(2):
                o_ref[hh, pl.ds(r0, blk), :] = outs[hh]
                s_ref[2 * d + hh] = new_st[hh]
        return carry

    lax.fori_loop(0, nb, step, 0)

    if not has_ctx:
        for d in range(2):
            for hh in range(2):
                st_ref[d, hh] = s_ref[2 * d + hh][...].T[hh * DK:(hh + 1) * DK, :]

    fin = min(seq, 512)

    def finish(n, carry):
        r0 = pl.multiple_of(n * fin, fin)
        for hh in range(2):
            o = of_ref[hh, pl.ds(r0, fin), :] + ob_ref[hh, pl.ds(r0, fin), :]
            ms = jnp.mean(o * o, axis=-1, keepdims=True)
            g = r_ref[pl.ds(r0, fin), hh * DV:(hh + 1) * DV].astype(F32)
            y = o * lax.rsqrt(ms + EPS) * ng_ref[...] * (g * _sigmoid(g))
            y_ref[pl.ds(r0, fin), hh * DV:(hh + 1) * DV] = y.astype(BF16)
        return carry

    lax.fori_loop(0, seq // fin, finish, 0)


def _gla_call(proj3, la3, norm_g, l, s0_all, st_prev):
    b, t, _ = proj3.shape
    blk = GLA_BLOCK
    assert t % blk == 0
    has_ctx = s0_all is not None
    in_specs = [pl.BlockSpec((None, t, LANES), lambda i, p: (i, 0, CB_BQ + p)),
                pl.BlockSpec((None, t, LANES), lambda i, p: (i, 0, CB_BK + p)),
                pl.BlockSpec((None, t, 2 * DV), lambda i, p: (i, 0, CB_BV // 2 + p)),
                pl.BlockSpec((None, t, 2 * DV), lambda i, p: (i, 0, CB_BR // 2 + p)),
                pl.BlockSpec((None, t, LANES), lambda i, p: (i, 0, p)),
                pl.BlockSpec((None, t, LANES), lambda i, p: (i, 0, 2 + p)),
                pl.BlockSpec((None, 1, DV), lambda i, p: (l, 0, 0))]
    args = [proj3, proj3, proj3, proj3, la3, la3, norm_g]
    st_in, st_args, st_out, st_shape, aliases = _state_io(l, b, s0_all, st_prev, len(args))
    return pl.pallas_call(
        functools.partial(_gla_kernel, seq=t, blk=blk, has_ctx=has_ctx, n_alias=len(aliases)),
        grid=(b, 2), in_specs=in_specs + st_in,
        out_specs=[pl.BlockSpec((None, t, 2 * DV), lambda i, p: (i, 0, p))] + st_out,
        out_shape=[jax.ShapeDtypeStruct((b, t, N_HEAD * DV), BF16)] + st_shape,
        input_output_aliases=aliases,
        scratch_shapes=[pltpu.VMEM((2, t, DV), F32), pltpu.VMEM((2, t, DV), F32),
                        pltpu.VMEM((4, DV, LANES), F32), pltpu.VMEM((2, blk, blk), jnp.int32)],
        compiler_params=_cparams(("arbitrary", "arbitrary")),
    )(*(args + st_args))


def _diff_kernel(*refs, lam_init, n_cache, n_new):
    lam_ref, q_ref = refs[:2]
    pos = 2
    if n_cache:
        kc_ref, vc_ref = refs[pos:pos + 2]
        pos += 2
    kn_ref, vn_ref, sg_ref, y_ref = refs[pos:pos + 4]
    pos += 4
    if n_cache:
        vtc_ref = refs[pos]
        pos += 1
    vtn_ref, sa_ref, sb_ref, acc_ref = refs[pos:]
    qb = q_ref.shape[0]
    kt = min(n_new, DIFF_KEY_TILE)
    n_tiles = n_new // kt

    @pl.when(pl.program_id(2) == 0)
    def _():
        if n_cache:
            vtc_ref[0:DV, :] = vc_ref[...].T
            vtc_ref[DV:, :] = jnp.ones((DIFF_ONES_ROWS, n_cache), BF16)
        for r in range(n_tiles):
            vtn_ref[r, 0:DV, :] = vn_ref[r * kt:(r + 1) * kt, :].T
            vtn_ref[r, DV:, :] = jnp.ones((DIFF_ONES_ROWS, kt), BF16)

    lam = _diff_lambda(lam_ref, lam_init)
    lane = lax.broadcasted_iota(jnp.int32, (1, LANES), 1)
    q = q_ref[...]
    qs =[jnp.where((lane < DK) if i == 0 else (lane >= DK), q, jnp.zeros_like(q)) for i in range(2)]

    s_bufs = (sa_ref, sb_ref)
    new0 = 1 if n_cache else 0
    last = new0 + n_tiles - 1

    def tile_rows(idx):
        return n_cache if idx < new0 else kt

    def scores(idx, r=None):
        if idx < new0:
            k_tile = kc_ref[...]
        elif r is None:
            k_tile = kn_ref[(idx - new0) * kt:(idx - new0 + 1) * kt, :]
        else:
            k_tile = kn_ref[pl.ds(pl.multiple_of(r * kt, kt), kt), :]
        for i in range(2):
            s_bufs[idx % 2][i, 0:tile_rows(idx), :] = _dot_nt(k_tile, qs[i])

    def soft(idx, m):
        n = tile_rows(idx)
        m_out, es, alphas = [], [], []
        for i in range(2):
            s = s_bufs[idx % 2][i, 0:n, :]
            m_new = jnp.maximum(m[i], jnp.max(s, axis=0, keepdims=True))
            es.append(jnp.exp2(s - m_new).astype(BF16))
            alphas.append(jnp.exp2(m[i] - m_new))
            m_out.append(m_new)
        return m_out, (es, alphas)

    def pv(idx, soft_out, r=None):
        es, alphas = soft_out
        if idx < new0:
            vt_tile = vtc_ref[...]
        else:
            vt_tile = vtn_ref[idx - new0 if r is None else r]
        for i in range(2):
            part = _dot(vt_tile, es[i])
            acc_ref[i] = part if idx == 0 else acc_ref[i] * alphas[i] + part

    m = [jnp.full((1, qb), -1e30, F32) for _ in range(2)]
    scores(0)
    if last >= 1:
        scores(1)
    m, so = soft(0, m)
    pv(0, so)
    n_pairs = max(last - 1, 0) // 2
    loop_end = 1 + 2 * n_pairs

    def pair(p, carry):
        m = list(carry)
        r = 2 * p + (1 - new0)
        scores(2, r + 1)
        m, so1 = soft(1, m)
        scores(3, r + 2)
        pv(1, so1, r)
        m, so2 = soft(2, m)
        pv(2, so2, r + 1)
        return tuple(m)

    if n_pairs:
        m = list(lax.fori_loop(0, n_pairs, pair, tuple(m)))
    for idx in range(loop_end, last + 1):
        if idx + 1 <= last:
            scores(idx + 1)
        m, so = soft(idx, m)
        pv(idx, so)

    acc0 = acc_ref[0]
    acc1 = acc_ref[1]
    o = (acc0[0:DV] * (1.0 / acc0[DV:DV + 1]) - acc1[0:DV] * (lam / acc1[DV:DV + 1]))
    ms = jnp.mean(o * o, axis=0, keepdims=True)
    y = o * lax.rsqrt(ms + EPS) * (sg_ref[...] * (1.0 - lam_init))
    y_ref[...] = y.T.astype(BF16)


def _diff_lambda(lam_ref, lam_init):
    lp = lam_ref[...]
    return (jnp.exp(jnp.sum(lp[0:1] * lp[1:2], axis=-1, keepdims=True))
            - jnp.exp(jnp.sum(lp[2:3] * lp[3:4], axis=-1, keepdims=True)) + lam_init)


def _diff_short_kernel(lam_ref, q_ref, k_ref, v_ref, sg_ref, y_ref, *, lam_init):
    t = q_ref.shape[0]
    lam = _diff_lambda(lam_ref, lam_init)
    lane = lax.broadcasted_iota(jnp.int32, (1, LANES), 1)
    ones = jnp.ones((DIFF_ONES_ROWS, t), BF16)
    cols = [slice(h * LANES, (h + 1) * LANES) for h in range(N_HEAD)]
    scores = []
    for h in range(N_HEAD):
        q = q_ref[:, cols[h]]
        k = k_ref[:, cols[h]]
        for i in range(2):
            qi = jnp.where((lane < DK) if i == 0 else (lane >= DK), q, jnp.zeros_like(q))
            scores.append(_dot_nt(k, qi))
    for h in range(N_HEAD):
        vt = jnp.concatenate([v_ref[:, cols[h]].T, ones], axis=0)
        acc = []
        for i in range(2):
            s = scores[2 * h + i]
            e = jnp.exp2(s - jnp.max(s, axis=0, keepdims=True)).astype(BF16)
            acc.append(_dot(vt, e))
        o = (acc[0][0:DV] * (1.0 / acc[0][DV:DV + 1]) - acc[1][0:DV] * (lam / acc[1][DV:DV + 1]))
        ms = jnp.mean(o * o, axis=0, keepdims=True)
        y = o * lax.rsqrt(ms + EPS) * (sg_ref[...] * (1.0 - lam_init))
        y_ref[:, cols[h]] = y.T.astype(BF16)


def _diff_short_call(proj3, lam_p, subln_col, l):
    b, t, _ = proj3.shape
    lam_init = 0.8 - 0.6 * math.exp(-0.3 * l)
    width = N_HEAD * LANES
    col = lambda cb: pl.BlockSpec((None, t, width), lambda i: (i, 0, cb * LANES // width))
    return pl.pallas_call(
        functools.partial(_diff_short_kernel, lam_init=lam_init),
        grid=(b,),
        in_specs=[pl.BlockSpec((None, 4, DK), lambda i: (l, 0, 0)), col(CB_CQ), col(CB_CK), col(CB_CV),
                  pl.BlockSpec((None, DV, 1), lambda i: (l, 0, 0))],
        out_specs=pl.BlockSpec((None, t, N_HEAD * DV), lambda i: (i, 0, 0)),
        out_shape=jax.ShapeDtypeStruct((b, t, N_HEAD * DV), BF16),
        compiler_params=_cparams(("arbitrary",)),
    )(lam_p, proj3, proj3, proj3, subln_col)


def _diff_call(proj3, cache_k, cache_v, lam_p, subln_col, l):
    b, t, _ = proj3.shape
    if cache_k is None and t <= DIFF_KEY_TILE:
        return _diff_short_call(proj3, lam_p, subln_col, l)
    n_cache = 0 if cache_k is None else cache_k.shape[1]
    qb = min(Q_BLOCK, t)
    kt = min(t, DIFF_KEY_TILE)
    assert t % qb == 0 and t % kt == 0
    vrows = DV + DIFF_ONES_ROWS
    lam_init = 0.8 - 0.6 * math.exp(-0.3 * l)
    in_specs = [pl.BlockSpec((None, 4, DK), lambda i, h, j: (l, 0, 0)),
                pl.BlockSpec((None, qb, LANES), lambda i, h, j: (i, j, CB_CQ + h))]
    args = [lam_p, proj3]
    if n_cache:
        in_specs += [pl.BlockSpec((None, n_cache, LANES), lambda i, h, j: (i, 0, h)),
                     pl.BlockSpec((None, n_cache, DV), lambda i, h, j: (i, 0, h))]
        args += [cache_k, cache_v]
    in_specs += [pl.BlockSpec((None, t, LANES), lambda i, h, j: (i, 0, CB_CK + h)),
                 pl.BlockSpec((None, t, DV), lambda i, h, j: (i, 0, CB_CV + h)),
                 pl.BlockSpec((None, DV, 1), lambda i, h, j: (l, 0, 0))]
    args += [proj3, proj3, subln_col]
    return pl.pallas_call(
        functools.partial(_diff_kernel, lam_init=lam_init, n_cache=n_cache, n_new=t),
        grid=(b, N_HEAD, t // qb), in_specs=in_specs,
        out_specs=pl.BlockSpec((None, qb, DV), lambda i, h, j: (i, j, h)),
        out_shape=jax.ShapeDtypeStruct((b, t, N_HEAD * DV), BF16),
        scratch_shapes=([pltpu.VMEM((vrows, n_cache), BF16)] if n_cache else [])
        + [pltpu.VMEM((t // kt, vrows, kt), BF16), pltpu.VMEM((2, max(kt, n_cache), qb), F32),
           pltpu.VMEM((2, max(kt, n_cache), qb), F32), pltpu.VMEM((2, vrows, qb), F32)],
        compiler_params=_cparams(("arbitrary", "arbitrary", "arbitrary")),
    )(*args)


def _layer_norm(z, g, b):
    mu = jnp.mean(z, axis=-1, keepdims=True)
    zc = z - mu
    var = jnp.mean(zc * zc, axis=-1, keepdims=True)
    return zc * lax.rsqrt(var + EPS) * g + b


def _merge_kernel(x_ref, ya_ref, yb_ref, yc_ref, m_ref, g1_ref, wb_ref, wo_ref, lg_ref, lb_ref, o_ref):
    half = x_ref.shape[0] // 2
    merged = []
    for p in range(2):
        rs = slice(p * half, (p + 1) * half)
        acc = None
        for i, y_ref in enumerate((ya_ref, yb_ref, yc_ref)):
            gate = _sigmoid(m_ref[rs, i * D_MODEL:(i + 1) * D_MODEL].astype(F32))
            term = gate * _dot(y_ref[rs, :], wb_ref[i])
            acc = term if acc is None else acc + term
        merged.append(acc.astype(BF16))
    for p in range(2):
        rs = slice(p * half, (p + 1) * half)
        out = _dot(merged[p], wo_ref[...])
        z = ALPHA * x_ref[rs, :] + g1_ref[...] * out
        o_ref[rs, :] = _layer_norm(z, lg_ref[...], lb_ref[...])


def _merge_call(x, ya, yb, yc, proj, mod, l, row0, span, wts):
    n = x.shape[0]
    tb = TOKEN_BLOCK
    bw = N_HEAD * DV
    tok = lambda w: pl.BlockSpec((tb, w), lambda i: (i, 0))
    return pl.pallas_call(
        _merge_kernel,
        grid=(n // tb,),
        in_specs=[tok(D_MODEL), tok(bw), tok(bw), tok(bw),
                  pl.BlockSpec((tb, N_GATE_COLS), lambda i: (i, 0)),
                  _mod_spec(l, row0, span, 2),
                  _resident((3, bw, D_MODEL), l), _resident((D_MODEL, D_MODEL), l),
                  _resident((1, D_MODEL), l, 0), _resident((1, D_MODEL), l, 0)],
        out_specs=tok(D_MODEL),
        out_shape=jax.ShapeDtypeStruct((n, D_MODEL), F32),
        compiler_params=_cparams(("arbitrary",)),
    )(x, ya, yb, yc, proj, mod, wts['w_branch'], wts['w_out'], wts['ln_g'], wts['ln_b'])


def _regroup_kernel(w_ref, o_ref):
    for c in range(D_FF // FF_CHUNK):
        lo, hi = c * FF_CHUNK, (c + 1) * FF_CHUNK
        o_ref[:, 2 * lo:2 * lo + FF_CHUNK] = w_ref[:, lo:hi].astype(BF16)
        o_ref[:, 2 * lo + FF_CHUNK:2 * hi] = w_ref[:, D_FF + lo:D_FF + hi].astype(BF16)


def _regroup_w_up(w_up):
    rows = D_MODEL // 4
    return pl.pallas_call(
        _regroup_kernel,
        grid=(DEPTH, D_MODEL // rows),
        in_specs=[pl.BlockSpec((None, rows, 2 * D_FF), lambda l, j: (l, j, 0))],
        out_specs=pl.BlockSpec((None, rows, 2 * D_FF), lambda l, j: (l, j, 0)),
        out_shape=jax.ShapeDtypeStruct((DEPTH, D_MODEL, 2 * D_FF), BF16),
        compiler_params=_cparams(("arbitrary", "arbitrary")),
    )(w_up)


def _gelu_tanh(x):
    k = -2.0 * 0.7978845608028654
    w = x * (x * x * (k * 0.044715) + k)
    return x / (1.0 + jnp.exp(w))


def _ffn_kernel(xp_ref, x_ref, xn_ref, sh_ref, sc_ref, g2_ref, wu_ref, cw_ref, cb_ref, wd_ref,
                lg_ref, lb_ref, o_ref, acc_ref, *, seq_len):
    tb = x_ref.shape[0]
    halo = SUBLANES
    i = pl.program_id(0)
    xm = x_ref[...]
    scale = 1.0 + sc_ref[...]
    shift = sh_ref[...]
    starts_seq = (i * tb) % seq_len == 0
    ends_seq = ((i + 1) * tb) % seq_len == 0
    h_prev = jnp.where(starts_seq, 0.0, xp_ref[...] * scale + shift)
    h_next = jnp.where(ends_seq, 0.0, xn_ref[...] * scale + shift)
    h_mid = xm * scale + shift
    hm = h_mid.astype(BF16)
    h = jnp.concatenate([h_prev, h_mid, h_next], axis=0).astype(BF16)
    rows = tb + 2 * halo
    n_chunk = D_FF // FF_CHUNK

    cut = (rows // 2 + 15) // 16 * 16

    def up(c):
        w = wu_ref[:, 2 * c * FF_CHUNK:2 * (c + 1) * FF_CHUNK]
        u = jnp.concatenate([_dot(h[:cut], w), _dot(h[cut:], w)], axis=0)
        return u[:, :FF_CHUNK], u[halo:halo + tb, FF_CHUNK:]

    ab_next = up(0)
    for c in range(n_chunk):
        lo, hi = c * FF_CHUNK, (c + 1) * FF_CHUNK
        a, b = ab_next
        if c + 1 < n_chunk:
            ab_next = up(c + 1)
        a_prev = pltpu.roll(a, 1, 0)[halo:halo + tb]
        a_mid = a[halo:halo + tb]
        a_next = pltpu.roll(a, rows - 1, 0)[halo:halo + tb]
        w0, w1, w2 = cw_ref[0:1, lo:hi], cw_ref[1:2, lo:hi], cw_ref[2:3, lo:hi]
        cb = cb_ref[:, lo:hi]
        cv = a_prev * w0 + a_mid * w1 + a_next * w2 + cb
        for r in range(seq_len, tb, seq_len):
            sl = slice(r - SUBLANES, r + SUBLANES)
            rid = lax.broadcasted_iota(jnp.int32, (2 * SUBLANES, 1), 0)
            seam = (jnp.where(rid == SUBLANES, 0.0, a_prev[sl]) * w0 + a_mid[sl] * w1
                    + jnp.where(rid == SUBLANES - 1, 0.0, a_next[sl]) * w2 + cb)
            cv = jnp.concatenate([cv[:r - SUBLANES], seam, cv[r + SUBLANES:]], axis=0)
        act = (_gelu_tanh(cv) * b).astype(BF16)
        part = _dot(act, wd_ref[lo:hi, :])
        if c == 0:
            acc_ref[...] = part
        else:
            acc_ref[...] += part
    z = ALPHA * xm + g2_ref[...] * acc_ref[...]
    o_ref[...] = _layer_norm(z, lg_ref[...], lb_ref[...])


def _ffn_call(x, mod, l, row0, span, seq_len, wts):
    n = x.shape[0]
    tb = TOKEN_BLOCK
    hb = tb // SUBLANES
    last = n // SUBLANES - 1
    modspec = functools.partial(_mod_spec, l, row0, span)

    return pl.pallas_call(
        functools.partial(_ffn_kernel, seq_len=seq_len),
        grid=(n // tb,),
        in_specs=[pl.BlockSpec((SUBLANES, D_MODEL), lambda i: (jnp.maximum(i * hb - 1, 0), 0)),
                  pl.BlockSpec((tb, D_MODEL), lambda i: (i, 0)),
                  pl.BlockSpec((SUBLANES, D_MODEL), lambda i: (jnp.minimum((i + 1) * hb, last), 0)),
                  modspec(3), modspec(4), modspec(5),
                  _resident((D_MODEL, 2 * D_FF), l), _resident((3, D_FF), l),
                  _resident((1, D_FF), l), _resident((D_FF, D_MODEL), l),
                  _resident((1, D_MODEL), l, 1), _resident((1, D_MODEL), l, 1)],
        out_specs=pl.BlockSpec((tb, D_MODEL), lambda i: (i, 0)),
        out_shape=jax.ShapeDtypeStruct((n, D_MODEL), F32),
        scratch_shapes=[pltpu.VMEM((tb, D_MODEL), F32)],
        compiler_params=_cparams(("arbitrary",)),
    )(x, x, x, mod, mod, mod, wts['w_up'], wts['conv_w'], wts['conv_b'], wts['w_down'],
      wts['ln_g'], wts['ln_b'])


def _rope_tables(seq):
    half = DK // 4
    t = np.arange(seq)
    inv = ROPE_BASE ** (-np.arange(half, dtype=np.float64) / half)
    ang_row = (t // GRID_W)[:, None] * inv[None, :]
    ang_col = (t % GRID_W)[:, None] * inv[None, :]
    ang = np.concatenate([ang_row, ang_row, ang_col, ang_col], axis=1)
    first = (np.arange(DK) % (2 * half)) < half
    cos = np.cos(ang)
    sa = np.where(first[None, :], -np.sin(ang), 0.0)
    sb = np.where(first[None, :], 0.0, np.sin(ang))
    tile = lambda a: jnp.asarray(np.concatenate([a, a], axis=1), F32)
    return tile(cos), tile(sa), tile(sb)


def _trunk(x, mod, row0, seq_len, batch, l, wts, tables, ctx, carried):
    n = x.shape[0]
    span = n if ctx is None else seq_len
    kv_prev = None if carried is None else carried[:2]
    outs = _proj_call(x, mod, l, row0, span, seq_len, wts, tables, ctx is None, kv_prev)
    proj, la = outs[0], outs[1]
    proj3 = proj.reshape(batch, seq_len, D_IN)
    la3 = la.reshape(batch, seq_len, 4 * LANES)
    if ctx is None:
        ret_prev, gla_prev = carried[2:]
        ya, st_ret = _ret_call(proj3, wts['ret_decay'], l, None, ret_prev)
        yb, st_gla = _gla_call(proj3, la3, wts['gla_norm_g'], l, None, gla_prev)
        yc = _diff_call(proj3, None, None, wts['diff_lam'], wts['diff_subln_g'], l)
        carried = (outs[2], outs[3], st_ret, st_gla)
    else:
        ya, = _ret_call(proj3, wts['ret_decay'], l, ctx['ret'], None)
        yb, = _gla_call(proj3, la3, wts['gla_norm_g'], l, ctx['gla'], None)
        yc = _diff_call(proj3, ctx['dk'][l], ctx['dv'][l], wts['diff_lam'], wts['diff_subln_g'], l)
    bw = N_HEAD * DV
    x1 = _merge_call(x, ya.reshape(n, bw), yb.reshape(n, bw), yc.reshape(n, bw), proj, mod, l,
                     row0, span, wts)
    x2 = _ffn_call(x1, mod, l, row0, span, seq_len, wts)
    return x2, carried


def kernel(x_prompt, x_sample, cache_diff_k, cache_diff_v, state_ret, state_gla, c, c_ctx,
           ada_w, ada_b, w_in, ret_decay, gla_wa1, gla_wa2, gla_ba, gla_norm_g, diff_lam,
           diff_subln_g, w_branch, w_out, ln_g, ln_b, ffn_w_up, ffn_conv_w, ffn_conv_b, ffn_w_down):
    bp, tp, _ = x_prompt.shape
    bs, ts, _ = x_sample.shape
    past = cache_diff_k.shape[3]

    cvec = jnp.concatenate([c_ctx[None, :], c, jnp.zeros((SUBLANES - 1 - bs, D_MODEL), F32)], axis=0)
    mod = _ada_call(cvec, ada_w, ada_b).reshape(DEPTH, SUBLANES, 6, 1, D_MODEL)

    wa1 = jnp.concatenate([gla_wa1[:, 0], gla_wa1[:, 1]], axis=-1)
    wa1 = jnp.pad(wa1, ((0, 0), (0, 0), (0, LANES - 2 * GLA_RANK))).astype(BF16)
    wa2 = jnp.zeros((DEPTH, LANES, 4 * LANES), F32)
    wa2 = wa2.at[:, 0:GLA_RANK, 0:2 * LANES].set(gla_wa2[:, 0])
    wa2 = wa2.at[:, GLA_RANK:2 * GLA_RANK, 2 * LANES:].set(gla_wa2[:, 1]).astype(BF16)
    wts = {
        'w_in': w_in.astype(BF16), 'wa1': wa1, 'wa2': wa2,
        'ba': gla_ba.reshape(DEPTH, 1, 4 * LANES), 'ret_decay': ret_decay,
        'gla_norm_g': gla_norm_g.reshape(DEPTH, 1, DV), 'diff_lam': diff_lam,
        'diff_subln_g': diff_subln_g.reshape(DEPTH, DV, 1),
        'w_branch': w_branch.astype(BF16), 'w_out': w_out.astype(BF16),
        'ln_g': ln_g.reshape(DEPTH, 2, 1, D_MODEL), 'ln_b': ln_b.reshape(DEPTH, 2, 1, D_MODEL),
        'w_up': _regroup_w_up(ffn_w_up), 'conv_w': ffn_conv_w,
        'conv_b': ffn_conv_b.reshape(DEPTH, 1, D_FF), 'w_down': ffn_w_down.astype(BF16),
    }

    h = x_prompt.reshape(bp * tp, D_MODEL)
    carried = (jnp.zeros((bp, DEPTH, 2 * N_HEAD, tp, DK), F32), jnp.zeros((bp, DEPTH, N_HEAD, tp, DV), F32),
               jnp.zeros((bp, DEPTH, 2, N_HEAD, DK, DV), F32), jnp.zeros((bp, DEPTH, 2, N_HEAD, DK, DV), F32))
    for l in range(DEPTH):
        h, carried = _trunk(h, mod, 0, tp, bp, l, wts, None, None, carried)
    y_prompt = h.reshape(bp, tp, D_MODEL)

    tables = _rope_tables(ts)
    ctx = {
        'dk': cache_diff_k.transpose(1, 0, 3, 2, 4).reshape(DEPTH, bs, past, 2 * N_HEAD * DK).astype(BF16),
        'dv': cache_diff_v.transpose(1, 0, 3, 2, 4).reshape(DEPTH, bs, past, N_HEAD * DV).astype(BF16),
        'ret': state_ret, 'gla': state_gla,
    }
    z = x_sample.reshape(bs * ts, D_MODEL)
    for l in range(DEPTH):
        z, _ = _trunk(z, mod, 1, ts, bs, l, wts, tables, ctx, None)
    y_sample = z.reshape(bs, ts, D_MODEL)

    return (y_prompt, y_sample) + tuple(carried)
```

```python
import functools
import math

import numpy as np
import jax
import jax.numpy as jnp
from jax import lax
from jax.experimental import pallas as pl
from jax.experimental.pallas import tpu as pltpu

F32 = jnp.float32
BF16 = jnp.bfloat16

D_MODEL = 1024
DEPTH = 2
GRID_W = 64
N_HEAD = 4
DK = 64
DV = 128
GLA_RANK = 16
GLA_TAU = 16.0
D_FF = 2816
ROPE_BASE = 10000.0
ALPHA = (2 * DEPTH) ** 0.25
EPS = 1e-5
D_IN = 7680

LANES = 128
SUBLANES = 8
VMEM_LIMIT = 56 * 1024 * 1024

N_GATE_COLS = 3 * D_MODEL
_QK_BLOCKS = N_HEAD * DK // LANES
_V_BLOCKS = N_HEAD * DV // LANES
_segment_blocks = (_QK_BLOCKS, _QK_BLOCKS, _V_BLOCKS, _V_BLOCKS,
                   _QK_BLOCKS, _QK_BLOCKS, _V_BLOCKS, _V_BLOCKS,
                   2 * _QK_BLOCKS, 2 * _QK_BLOCKS, _V_BLOCKS)
(CB_AQ, CB_AK, CB_AV, CB_AG, CB_BQ, CB_BK, CB_BV, CB_BR, CB_CQ, CB_CK, CB_CV) = (
    N_GATE_COLS // LANES + sum(_segment_blocks[:i]) for i in range(len(_segment_blocks)))
ROPE_BLOCKS = (tuple(range(CB_AQ, CB_AK + _QK_BLOCKS))
               + tuple(range(CB_CQ, CB_CK + 2 * _QK_BLOCKS)))
SCALED_BLOCKS = (tuple(range(CB_AK, CB_AK + _QK_BLOCKS))
                 + tuple(range(CB_BQ, CB_BQ + _QK_BLOCKS)))
SOFTMAX_Q_BLOCKS = tuple(range(CB_CQ, CB_CQ + 2 * _QK_BLOCKS))
QK_SCALE = DK ** -0.5
LOG2E = math.log2(math.e)

TOKEN_BLOCK = 512
RET_CHUNK = 256
GLA_BLOCK = 128
Q_BLOCK = 1024
DIFF_KEY_TILE = 512
DIFF_ONES_ROWS = 16
FF_CHUNK = 256


def _cparams(sem):
    return pltpu.CompilerParams(dimension_semantics=sem, vmem_limit_bytes=VMEM_LIMIT)


def _resident(shape, *lead):
    idx = tuple(lead) + (0,) * len(shape)
    return pl.BlockSpec((None,) * len(lead) + tuple(shape), lambda *_: idx,
                        pipeline_mode=pl.Buffered(1))


def _mod_spec(l, row0, span, k):
    return pl.BlockSpec((None, None, None, 1, D_MODEL),
                        lambda i: (l, row0 + (i * TOKEN_BLOCK) // span, k, 0, 0))


def _sigmoid(x):
    return 1.0 / (1.0 + jnp.exp(-x))


def _log_sigmoid(x):
    return jnp.minimum(x, 0.0) - jnp.log(1.0 + jnp.exp(-jnp.abs(x)))


def _dot(a, b):
    return jnp.dot(a, b, preferred_element_type=F32)


def _dot_nt(a, b):
    return lax.dot_general(a, b, (((1,), (1,)), ((), ())), preferred_element_type=F32)


def _dot_tn(a, b):
    return lax.dot_general(a, b, (((0,), (0,)), ((), ())), preferred_element_type=F32)


def _ada_kernel(c_ref, w_ref, b_ref, o_ref):
    c = c_ref[...]
    s = (c * _sigmoid(c)).astype(BF16)
    o_ref[...] = _dot(s, w_ref[...].astype(BF16)) + b_ref[...]


def _ada_call(cvec, ada_w, ada_b):
    nt = 1536
    return pl.pallas_call(
        _ada_kernel,
        grid=(DEPTH, 6 * D_MODEL // nt),
        in_specs=[pl.BlockSpec((SUBLANES, D_MODEL), lambda l, j: (0, 0)),
                  pl.BlockSpec((None, D_MODEL, nt), lambda l, j: (l, 0, j)),
                  pl.BlockSpec((None, 1, nt), lambda l, j: (l, 0, j))],
        out_specs=pl.BlockSpec((None, SUBLANES, nt), lambda l, j: (l, 0, j)),
        out_shape=jax.ShapeDtypeStruct((DEPTH, SUBLANES, 6 * D_MODEL), F32),
        compiler_params=_cparams(("arbitrary", "arbitrary")),
    )(cvec, ada_w, ada_b.reshape(DEPTH, 1, 6 * D_MODEL))


def _proj_kernel(*refs, rope, emit_kv, n_alias, seq_len):
    x_ref, sh_ref, sc_ref, w_ref, wa1_ref, wa2_ref, ba_ref = refs[:7]
    pos = 7
    if rope:
        cos_ref, sa_ref, sb_ref = refs[pos:pos + 3]
        pos += 3
    pos += n_alias
    proj_ref, la_ref = refs[pos:pos + 2]
    pos += 2
    if emit_kv:
        ck_ref, cv_ref = refs[pos:pos + 2]

    h = (x_ref[...] * (1.0 + sc_ref[...]) + sh_ref[...]).astype(BF16)
    tile = 512
    per = tile // LANES
    n_blk = D_IN // LANES
    r = _dot(h, wa1_ref[...]).astype(BF16)
    for j in range(D_IN // tile):
        acc = _dot(h, w_ref[:, j * tile:(j + 1) * tile])
        if j == 0:
            z = _dot(r, wa2_ref[...]) + ba_ref[...]
        if j == 1:
            la_ref[...] = _log_sigmoid(z) * (1.0 / GLA_TAU)
        for i in range(per):
            blk = (j * per + i + N_GATE_COLS // LANES) % n_blk
            y = acc[:, i * LANES:(i + 1) * LANES]
            if emit_kv and CB_CK <= blk < CB_CK + 4:
                for s in range(x_ref.shape[0] // seq_len):
                    for half in range(2):
                        ck_ref[s, 2 * (blk - CB_CK) + half] = (
                            y[s * seq_len:(s + 1) * seq_len, half * DK:(half + 1) * DK])
            if emit_kv and CB_CV <= blk < CB_CV + 4:
                for s in range(x_ref.shape[0] // seq_len):
                    cv_ref[s, blk - CB_CV] = y[s * seq_len:(s + 1) * seq_len, :]
            if rope and blk in ROPE_BLOCKS:
                y = (y * cos_ref[...] + pltpu.roll(y, LANES - 16, 1) * sa_ref[...]
                     + pltpu.roll(y, 16, 1) * sb_ref[...])
            if blk in SCALED_BLOCKS:
                y = y * QK_SCALE
            if blk in SOFTMAX_Q_BLOCKS:
                y = y * (QK_SCALE * LOG2E)
            proj_ref[:, blk * LANES:(blk + 1) * LANES] = y.astype(BF16)


def _proj_call(x, mod, l, row0, span, seq_len, wts, tables, emit_kv, kv_prev):
    n = x.shape[0]
    tb = TOKEN_BLOCK
    assert n % tb == 0 and (seq_len % tb == 0 or tb % seq_len == 0)
    bps = max(seq_len // tb, 1)
    rope = tables is not None

    in_specs = [pl.BlockSpec((tb, D_MODEL), lambda i: (i, 0)),
                _mod_spec(l, row0, span, 0), _mod_spec(l, row0, span, 1),
                _resident((D_MODEL, D_IN), l), _resident((D_MODEL, LANES), l),
                _resident((LANES, 4 * LANES), l), _resident((1, 4 * LANES), l)]
    args = [x, mod, mod, wts['w_in'], wts['wa1'], wts['wa2'], wts['ba']]
    if rope:
        in_specs += [pl.BlockSpec((tb, LANES), lambda i: (i % bps, 0))] * 3
        args += list(tables)
    aliases = {}
    if kv_prev is not None:
        aliases = {len(args): 2, len(args) + 1: 3}
        in_specs += [pl.BlockSpec(memory_space=pl.ANY)] * 2
        args += list(kv_prev)
    out_specs = [pl.BlockSpec((tb, D_IN), lambda i: (i, 0)),
                 pl.BlockSpec((tb, 4 * LANES), lambda i: (i, 0))]
    out_shape = [jax.ShapeDtypeStruct((n, D_IN), BF16), jax.ShapeDtypeStruct((n, 4 * LANES), F32)]
    if emit_kv:
        spb = tb // seq_len
        nseq = n // seq_len
        out_specs += [pl.BlockSpec((spb, None, 2 * N_HEAD, seq_len, DK), lambda i: (i, l, 0, 0, 0)),
                      pl.BlockSpec((spb, None, N_HEAD, seq_len, DV), lambda i: (i, l, 0, 0, 0))]
        out_shape += [jax.ShapeDtypeStruct((nseq, DEPTH, 2 * N_HEAD, seq_len, DK), F32),
                      jax.ShapeDtypeStruct((nseq, DEPTH, N_HEAD, seq_len, DV), F32)]
    return pl.pallas_call(
        functools.partial(_proj_kernel, rope=rope, emit_kv=emit_kv, n_alias=len(aliases),
                          seq_len=seq_len),
        grid=(n // tb,), in_specs=in_specs, out_specs=out_specs, out_shape=out_shape,
        input_output_aliases=aliases,
        compiler_params=_cparams(("arbitrary",)),
    )(*args)


def _ret_kernel(*refs, layer, seq, chunk, has_ctx, n_alias):
    dec_ref, q_ref, k_ref, v_ref, g_ref = refs[:5]
    pos = 5
    if has_ctx:
        s0_ref = refs[pos]
        pos += 1
    pos += n_alias
    y_ref = refs[pos]
    pos += 1
    if not has_ctx:
        st_ref = refs[pos]
        pos += 1
    of_ref, ob_ref, s_ref, dm_ref, eq_ref, ek_ref, gc_ref = refs[pos:]
    p = pl.program_id(1)
    nc = seq // chunk
    c_f = float(chunk)
    lane = lax.broadcasted_iota(jnp.int32, (1, LANES), 1)
    hmask = [lane < DK, lane >= DK]
    rowp = lax.broadcasted_iota(jnp.int32, (chunk, 1), 0).astype(F32)
    colp = lax.broadcasted_iota(jnp.int32, (1, chunk), 1).astype(F32)
    diff = rowp - colp

    @pl.when((pl.program_id(0) == 0) & (p == 0))
    def _():
        for pair in range(2):
            for d in range(2):
                for hh in range(2):
                    ti = 4 * pair + 2 * d + hh
                    raw = dec_ref[layer, d, 2 * pair + hh]
                    lg_l = _log_sigmoid(jnp.full((1, LANES), raw, F32))
                    lg_c = _log_sigmoid(jnp.full((1, chunk), raw, F32))
                    if d == 0:
                        dm_ref[ti] = jnp.where(diff >= 0, jnp.exp(jnp.maximum(diff, 0.0) * lg_c), 0.0)
                        eq_ref[ti] = jnp.exp((rowp + 1.0) * lg_l)
                        ek_ref[ti] = jnp.exp((c_f - 1.0 - rowp) * lg_l)
                    else:
                        dm_ref[ti] = jnp.where(diff <= 0, jnp.exp(jnp.maximum(-diff, 0.0) * lg_c), 0.0)
                        eq_ref[ti] = jnp.exp((c_f - rowp) * lg_l)
                        ek_ref[ti] = jnp.exp(rowp * lg_l)
                    gc_ref[ti] = jnp.exp(c_f * lg_l)

    for d in range(2):
        for hh in range(2):
            ci = 2 * d + hh
            if has_ctx:
                s0 = s0_ref[d, hh]
                zero = jnp.zeros((DK, DV), F32)
                s_ref[ci] = jnp.concatenate([s0, zero] if hh == 0 else [zero, s0], axis=0)
            else:
                s_ref[ci] = jnp.zeros((LANES, DV), F32)

    def step(n, carry):
        pending = []
        for d in range(2):
            c = n if d == 0 else nc - 1 - n
            r0 = pl.multiple_of(c * chunk, chunk)
            qc = q_ref[pl.ds(r0, chunk), :]
            kc = k_ref[pl.ds(r0, chunk), :]
            for hh in range(2):
                ci = 2 * d + hh
                qh = jnp.where(hmask[hh], qc, jnp.zeros_like(qc))
                kh = jnp.where(hmask[hh], kc, jnp.zeros_like(kc))
                vh = v_ref[pl.ds(r0, chunk), hh * DV:(hh + 1) * DV]
                sc = _dot_nt(qh, kh)
                sb = s_ref[ci]
                ti = 4 * p + ci
                qd = (qh.astype(F32) * eq_ref[ti]).astype(BF16)
                inter = _dot(qd, sb.astype(BF16))
                kd = (kh.astype(F32) * ek_ref[ti]).astype(BF16)
                s_ref[ci] = gc_ref[ti] * sb + _dot_tn(kd, vh)
                pending.append((d, hh, r0, sc, inter, vh))
        for d, hh, r0, sc, inter, vh in pending:
            o_ref = of_ref if d == 0 else ob_ref
            intra = _dot((sc * dm_ref[4 * p + 2 * d + hh]).astype(BF16), vh)
            o_ref[hh, pl.ds(r0, chunk), :] = intra + inter
        return carry

    lax.fori_loop(0, nc, step, 0)

    if not has_ctx:
        for d in range(2):
            for hh in range(2):
                st_ref[d, hh] = s_ref[2 * d + hh][hh * DK:(hh + 1) * DK, :]

    def finish(n, carry):
        r0 = pl.multiple_of(n * chunk, chunk)
        for hh in range(2):
            o = of_ref[hh, pl.ds(r0, chunk), :] + ob_ref[hh, pl.ds(r0, chunk), :]
            mu = jnp.mean(o, axis=-1, keepdims=True)
            oc = o - mu
            var = jnp.mean(oc * oc, axis=-1, keepdims=True)
            g = g_ref[pl.ds(r0, chunk), hh * DV:(hh + 1) * DV].astype(F32)
            y = oc * lax.rsqrt(var + EPS) * (g * _sigmoid(g))
            y_ref[pl.ds(r0, chunk), hh * DV:(hh + 1) * DV] = y.astype(BF16)
        return carry

    lax.fori_loop(0, nc, finish, 0)


def _state_io(l, batch, s0_all, st_prev, n_in):
    spec = pl.BlockSpec((None, None, 2, 2, DK, DV), lambda i, p: (i, l, 0, p, 0, 0))
    if s0_all is not None:
        return [spec], [s0_all], [], [], {}
    shape = jax.ShapeDtypeStruct((batch, DEPTH, 2, N_HEAD, DK, DV), F32)
    return [pl.BlockSpec(memory_space=pl.ANY)], [st_prev], [spec], [shape], {n_in: 1}


def _ret_call(proj3, decay, l, s0_all, st_prev):
    b, t, _ = proj3.shape
    chunk = min(RET_CHUNK, t)
    assert t % chunk == 0
    has_ctx = s0_all is not None
    in_specs = [pl.BlockSpec(memory_space=pltpu.SMEM),
                pl.BlockSpec((None, t, LANES), lambda i, p: (i, 0, CB_AQ + p)),
                pl.BlockSpec((None, t, LANES), lambda i, p: (i, 0, CB_AK + p)),
                pl.BlockSpec((None, t, 2 * DV), lambda i, p: (i, 0, CB_AV // 2 + p)),
                pl.BlockSpec((None, t, 2 * DV), lambda i, p: (i, 0, CB_AG // 2 + p))]
    args = [decay, proj3, proj3, proj3, proj3]
    st_in, st_args, st_out, st_shape, aliases = _state_io(l, b, s0_all, st_prev, len(args))
    return pl.pallas_call(
        functools.partial(_ret_kernel, layer=l, seq=t, chunk=chunk, has_ctx=has_ctx,
                          n_alias=len(aliases)),
        grid=(b, 2), in_specs=in_specs + st_in,
        out_specs=[pl.BlockSpec((None, t, 2 * DV), lambda i, p: (i, 0, p))] + st_out,
        out_shape=[jax.ShapeDtypeStruct((b, t, N_HEAD * DV), BF16)] + st_shape,
        input_output_aliases=aliases,
        scratch_shapes=[pltpu.VMEM((2, t, DV), F32), pltpu.VMEM((2, t, DV), F32),
                        pltpu.VMEM((4, LANES, DV), F32), pltpu.VMEM((8, chunk, chunk), F32),
                        pltpu.VMEM((8, chunk, LANES), F32), pltpu.VMEM((8, chunk, LANES), F32),
                        pltpu.VMEM((8, 1, LANES), F32)],
        compiler_params=_cparams(("arbitrary", "arbitrary")),
    )(*(args + st_args))


def _gla_levels(blk, reverse):
    t = lax.broadcasted_iota(jnp.int32, (blk, blk), 0)
    s = lax.broadcasted_iota(jnp.int32, (blk, blk), 1)
    xr = t ^ s
    lvl = jnp.zeros((blk, blk), jnp.int32)
    h = 2
    while h < blk:
        lvl = lvl + jnp.where(xr >= h, 1, 0)
        h *= 2
    allowed = (s > t) if reverse else (s < t)
    return jnp.where(t == s, -1, jnp.where(allowed, lvl, -2))


def _gla_sweep(q, k, la, hmask, lvl, blk, reverse):
    row = lax.broadcasted_iota(jnp.int32, (blk, 1), 0)
    w = la
    tot = la
    def both_heads(qx):
        return jnp.concatenate([jnp.where(m, qx, jnp.zeros_like(qx)) for m in hmask], axis=0)

    lvl2 = jnp.concatenate([lvl, lvl], axis=0)
    amat = jnp.where(lvl2 == -1, _dot_nt(both_heads(q.astype(BF16)), k.astype(BF16)), 0.0)
    h = 1
    li = 0
    while h < blk:
        up = (row & h) != 0
        qside = jnp.logical_not(up) if reverse else up
        f = jnp.exp(jnp.where(qside, w, tot - w))
        ql = (q * f).astype(BF16)
        kl = (k * f).astype(BF16)
        amat = jnp.where(lvl2 == li, _dot_nt(both_heads(ql), kl), amat)
        partner = jnp.where(up, pltpu.roll(tot, h, 0), pltpu.roll(tot, blk - h, 0))
        w = w + jnp.where(qside, partner, 0.0)
        tot = tot + partner
        h *= 2
        li += 1
    return amat, w, tot


def _gla_state_dots(q, k, w, tot, v_pair, st_list, hmask):
    qd = (q * jnp.exp(w)).astype(BF16)
    kd = (k * jnp.exp(tot - w)).astype(BF16)
    dec = jnp.exp(tot[0:1, :])
    inter, new_st = [], []
    for hh in range(2):
        vh = v_pair[:, hh * DV:(hh + 1) * DV]
        st = st_list[hh]
        qh = jnp.where(hmask[hh], qd, jnp.zeros_like(qd))
        kh = jnp.where(hmask[hh], kd, jnp.zeros_like(kd))
        inter.append(_dot_nt(qh, st.astype(BF16)))
        new_st.append(dec * st + _dot_tn(vh, kh))
    return inter, new_st


def _gla_kernel(*refs, seq, blk, has_ctx, n_alias):
    q_ref, k_ref, v_ref, r_ref, laf_ref, lab_ref, ng_ref = refs[:7]
    pos = 7
    if has_ctx:
        s0_ref = refs[pos]
        pos += 1
    pos += n_alias
    y_ref = refs[pos]
    pos += 1
    if not has_ctx:
        st_ref = refs[pos]
        pos += 1
    of_ref, ob_ref, s_ref, lv_ref = refs[pos:]
    nb = seq // blk
    lane = lax.broadcasted_iota(jnp.int32, (1, LANES), 1)
    hmask = [lane < DK, lane >= DK]

    @pl.when((pl.program_id(0) == 0) & (pl.program_id(1) == 0))
    def _():
        for d in range(2):
            lv_ref[d] = _gla_levels(blk, d == 1)

    for d in range(2):
        for hh in range(2):
            if has_ctx:
                zero = jnp.zeros((DK, DV), F32)
                s0 = s0_ref[d, hh]
                full = jnp.concatenate([s0, zero] if hh == 0 else [zero, s0], axis=0)
                s_ref[2 * d + hh] = full.T
            else:
                s_ref[2 * d + hh] = jnp.zeros((DV, LANES), F32)

    def step(n, carry):
        work = []
        for d in range(2):
            c = n if d == 0 else nb - 1 - n
            r0 = pl.multiple_of(c * blk, blk)
            q = q_ref[pl.ds(r0, blk), :].astype(F32)
            k = k_ref[pl.ds(r0, blk), :].astype(F32)
            la_ref = laf_ref if d == 0 else lab_ref
            amat, w, tot = _gla_sweep(q, k, la_ref[pl.ds(r0, blk), :], hmask, lv_ref[d], blk, d == 1)
            work.append((r0, q, k, amat, w, tot, v_ref[pl.ds(r0, blk), :]))
        inters = []
        for d, (r0, q, k, amat, w, tot, vp) in enumerate(work):
            inter, new_st = _gla_state_dots(q, k, w, tot, vp, [s_ref[2 * d], s_ref[2 * d + 1]], hmask)
            inters.append(inter)
            for hh in range(2):
                s_ref[2 * d + hh] = new_st[hh]
        for d, (r0, q, k, amat, w, tot, vp) in enumerate(work):
            o_ref = of_ref if d == 0 else ob_ref
            for hh in range(2):
                intra = _dot(amat[hh * blk:(hh + 1) * blk].astype(BF16), vp[:, hh * DV:(hh + 1) * DV])
                o_ref[hh, pl.ds(r0, blk), :] = intra + inters[d][hh]
        return carry

    lax.fori_loop(0, nb, step, 0)

    if not has_ctx:
        for d in range(2):
            for hh in range(2):
                st_ref[d, hh] = s_ref[2 * d + hh][...].T[hh * DK:(hh + 1) * DK, :]

    fin = min(seq, 512)

    def finish(n, carry):
        r0 = pl.multiple_of(n * fin, fin)
        for hh in range(2):
            o = of_ref[hh, pl.ds(r0, fin), :] + ob_ref[hh, pl.ds(r0, fin), :]
            ms = jnp.mean(o * o, axis=-1, keepdims=True)
            g = r_ref[pl.ds(r0, fin), hh * DV:(hh + 1) * DV].astype(F32)
            y = o * lax.rsqrt(ms + EPS) * ng_ref[...] * (g * _sigmoid(g))
            y_ref[pl.ds(r0, fin), hh * DV:(hh + 1) * DV] = y.astype(BF16)
        return carry

    lax.fori_loop(0, seq // fin, finish, 0)


def _gla_call(proj3, la3, norm_g, l, s0_all, st_prev):
    b, t, _ = proj3.shape
    blk = GLA_BLOCK
    assert t % blk == 0
    has_ctx = s0_all is not None
    in_specs = [pl.BlockSpec((None, t, LANES), lambda i, p: (i, 0, CB_BQ + p)),
                pl.BlockSpec((None, t, LANES), lambda i, p: (i, 0, CB_BK + p)),
                pl.BlockSpec((None, t, 2 * DV), lambda i, p: (i, 0, CB_BV // 2 + p)),
                pl.BlockSpec((None, t, 2 * DV), lambda i, p: (i, 0, CB_BR // 2 + p)),
                pl.BlockSpec((None, t, LANES), lambda i, p: (i, 0, p)),
                pl.BlockSpec((None, t, LANES), lambda i, p: (i, 0, 2 + p)),
                pl.BlockSpec((None, 1, DV), lambda i, p: (l, 0, 0))]
    args = [proj3, proj3, proj3, proj3, la3, la3, norm_g]
    st_in, st_args, st_out, st_shape, aliases = _state_io(l, b, s0_all, st_prev, len(args))
    return pl.pallas_call(
        functools.partial(_gla_kernel, seq=t, blk=blk, has_ctx=has_ctx, n_alias=len(aliases)),
        grid=(b, 2), in_specs=in_specs + st_in,
        out_specs=[pl.BlockSpec((None, t, 2 * DV), lambda i, p: (i, 0, p))] + st_out,
        out_shape=[jax.ShapeDtypeStruct((b, t, N_HEAD * DV), BF16)] + st_shape,
        input_output_aliases=aliases,
        scratch_shapes=[pltpu.VMEM((2, t, DV), F32), pltpu.VMEM((2, t, DV), F32),
                        pltpu.VMEM((4, DV, LANES), F32), pltpu.VMEM((2, blk, blk), jnp.int32)],
        compiler_params=_cparams(("arbitrary", "arbitrary")),
    )(*(args + st_args))


def _diff_kernel(*refs, lam_init, n_cache, n_new):
    lam_ref, q_ref = refs[:2]
    pos = 2
    if n_cache:
        kc_ref, vc_ref = refs[pos:pos + 2]
        pos += 2
    kn_ref, vn_ref, sg_ref, y_ref = refs[pos:pos + 4]
    pos += 4
    if n_cache:
        vtc_ref = refs[pos]
        pos += 1
    vtn_ref, sa_ref, sb_ref, acc_ref = refs[pos:]
    qb = q_ref.shape[0]
    kt = min(n_new, DIFF_KEY_TILE)
    n_tiles = n_new // kt

    @pl.when(pl.program_id(2) == 0)
    def _():
        if n_cache:
            vtc_ref[0:DV, :] = vc_ref[...].T
            vtc_ref[DV:, :] = jnp.ones((DIFF_ONES_ROWS, n_cache), BF16)
        for r in range(n_tiles):
            vtn_ref[r, 0:DV, :] = vn_ref[r * kt:(r + 1) * kt, :].T
            vtn_ref[r, DV:, :] = jnp.ones((DIFF_ONES_ROWS, kt), BF16)

    lam = _diff_lambda(lam_ref, lam_init)
    lane = lax.broadcasted_iota(jnp.int32, (1, LANES), 1)
    q = q_ref[...]
    qs =[jnp.where((lane < DK) if i == 0 else (lane >= DK), q, jnp.zeros_like(q)) for i in range(2)]

    s_bufs = (sa_ref, sb_ref)
    new0 = 1 if n_cache else 0
    last = new0 + n_tiles - 1

    def tile_rows(idx):
        return n_cache if idx < new0 else kt

    def scores(idx, r=None):
        if idx < new0:
            k_tile = kc_ref[...]
        elif r is None:
            k_tile = kn_ref[(idx - new0) * kt:(idx - new0 + 1) * kt, :]
        else:
            k_tile = kn_ref[pl.ds(pl.multiple_of(r * kt, kt), kt), :]
        for i in range(2):
            s_bufs[idx % 2][i, 0:tile_rows(idx), :] = _dot_nt(k_tile, qs[i])

    def soft(idx, m):
        n = tile_rows(idx)
        m_out, es, alphas = [], [], []
        for i in range(2):
            s = s_bufs[idx % 2][i, 0:n, :]
            m_new = jnp.maximum(m[i], jnp.max(s, axis=0, keepdims=True))
            es.append(jnp.exp2(s - m_new).astype(BF16))
            alphas.append(jnp.exp2(m[i] - m_new))
            m_out.append(m_new)
        return m_out, (es, alphas)

    def pv(idx, soft_out, r=None):
        es, alphas = soft_out
        if idx < new0:
            vt_tile = vtc_ref[...]
        else:
            vt_tile = vtn_ref[idx - new0 if r is None else r]
        for i in range(2):
            part = _dot(vt_tile, es[i])
            acc_ref[i] = part if idx == 0 else acc_ref[i] * alphas[i] + part

    m = [jnp.full((1, qb), -1e30, F32) for _ in range(2)]
    scores(0)
    if last >= 1:
        scores(1)
    m, so = soft(0, m)
    pv(0, so)
    n_pairs = max(last - 1, 0) // 2
    loop_end = 1 + 2 * n_pairs

    def pair(p, carry):
        m = list(carry)
        r = 2 * p + (1 - new0)
        scores(2, r + 1)
        m, so1 = soft(1, m)
        scores(3, r + 2)
        pv(1, so1, r)
        m, so2 = soft(2, m)
        pv(2, so2, r + 1)
        return tuple(m)

    if n_pairs:
        m = list(lax.fori_loop(0, n_pairs, pair, tuple(m)))
    for idx in range(loop_end, last + 1):
        if idx + 1 <= last:
            scores(idx + 1)
        m, so = soft(idx, m)
        pv(idx, so)

    acc0 = acc_ref[0]
    acc1 = acc_ref[1]
    o = (acc0[0:DV] * (1.0 / acc0[DV:DV + 1]) - acc1[0:DV] * (lam / acc1[DV:DV + 1]))
    ms = jnp.mean(o * o, axis=0, keepdims=True)
    y = o * lax.rsqrt(ms + EPS) * (sg_ref[...] * (1.0 - lam_init))
    y_ref[...] = y.T.astype(BF16)


def _diff_lambda(lam_ref, lam_init):
    lp = lam_ref[...]
    return (jnp.exp(jnp.sum(lp[0:1] * lp[1:2], axis=-1, keepdims=True))
            - jnp.exp(jnp.sum(lp[2:3] * lp[3:4], axis=-1, keepdims=True)) + lam_init)


def _diff_short_kernel(lam_ref, q_ref, k_ref, v_ref, sg_ref, y_ref, *, lam_init):
    t = q_ref.shape[0]
    lam = _diff_lambda(lam_ref, lam_init)
    lane = lax.broadcasted_iota(jnp.int32, (1, LANES), 1)
    ones = jnp.ones((DIFF_ONES_ROWS, t), BF16)
    cols = [slice(h * LANES, (h + 1) * LANES) for h in range(N_HEAD)]
    scores = []
    for h in range(N_HEAD):
        q = q_ref[:, cols[h]]
        k = k_ref[:, cols[h]]
        for i in range(2):
            qi = jnp.where((lane < DK) if i == 0 else (lane >= DK), q, jnp.zeros_like(q))
            scores.append(_dot_nt(k, qi))
    for h in range(N_HEAD):
        vt = jnp.concatenate([v_ref[:, cols[h]].T, ones], axis=0)
        acc = []
        for i in range(2):
            s = scores[2 * h + i]
            e = jnp.exp2(s - jnp.max(s, axis=0, keepdims=True)).astype(BF16)
            acc.append(_dot(vt, e))
        o = (acc[0][0:DV] * (1.0 / acc[0][DV:DV + 1]) - acc[1][0:DV] * (lam / acc[1][DV:DV + 1]))
        ms = jnp.mean(o * o, axis=0, keepdims=True)
        y = o * lax.rsqrt(ms + EPS) * (sg_ref[...] * (1.0 - lam_init))
        y_ref[:, cols[h]] = y.T.astype(BF16)


def _diff_short_call(proj3, lam_p, subln_col, l):
    b, t, _ = proj3.shape
    lam_init = 0.8 - 0.6 * math.exp(-0.3 * l)
    width = N_HEAD * LANES
    col = lambda cb: pl.BlockSpec((None, t, width), lambda i: (i, 0, cb * LANES // width))
    return pl.pallas_call(
        functools.partial(_diff_short_kernel, lam_init=lam_init),
        grid=(b,),
        in_specs=[pl.BlockSpec((None, 4, DK), lambda i: (l, 0, 0)), col(CB_CQ), col(CB_CK), col(CB_CV),
                  pl.BlockSpec((None, DV, 1), lambda i: (l, 0, 0))],
        out_specs=pl.BlockSpec((None, t, N_HEAD * DV), lambda i: (i, 0, 0)),
        out_shape=jax.ShapeDtypeStruct((b, t, N_HEAD * DV), BF16),
        compiler_params=_cparams(("arbitrary",)),
    )(lam_p, proj3, proj3, proj3, subln_col)


def _diff_call(proj3, cache_k, cache_v, lam_p, subln_col, l):
    b, t, _ = proj3.shape
    if cache_k is None and t <= DIFF_KEY_TILE:
        return _diff_short_call(proj3, lam_p, subln_col, l)
    n_cache = 0 if cache_k is None else cache_k.shape[1]
    qb = min(Q_BLOCK, t)
    kt = min(t, DIFF_KEY_TILE)
    assert t % qb == 0 and t % kt == 0
    vrows = DV + DIFF_ONES_ROWS
    lam_init = 0.8 - 0.6 * math.exp(-0.3 * l)
    in_specs = [pl.BlockSpec((None, 4, DK), lambda i, h, j: (l, 0, 0)),
                pl.BlockSpec((None, qb, LANES), lambda i, h, j: (i, j, CB_CQ + h))]
    args = [lam_p, proj3]
    if n_cache:
        in_specs += [pl.BlockSpec((None, n_cache, LANES), lambda i, h, j: (i, 0, h)),
                     pl.BlockSpec((None, n_cache, DV), lambda i, h, j: (i, 0, h))]
        args += [cache_k, cache_v]
    in_specs += [pl.BlockSpec((None, t, LANES), lambda i, h, j: (i, 0, CB_CK + h)),
                 pl.BlockSpec((None, t, DV), lambda i, h, j: (i, 0, CB_CV + h)),
                 pl.BlockSpec((None, DV, 1), lambda i, h, j: (l, 0, 0))]
    args += [proj3, proj3, subln_col]
    return pl.pallas_call(
        functools.partial(_diff_kernel, lam_init=lam_init, n_cache=n_cache, n_new=t),
        grid=(b, N_HEAD, t // qb), in_specs=in_specs,
        out_specs=pl.BlockSpec((None, qb, DV), lambda i, h, j: (i, j, h)),
        out_shape=jax.ShapeDtypeStruct((b, t, N_HEAD * DV), BF16),
        scratch_shapes=([pltpu.VMEM((vrows, n_cache), BF16)] if n_cache else [])
        + [pltpu.VMEM((t // kt, vrows, kt), BF16), pltpu.VMEM((2, max(kt, n_cache), qb), F32),
           pltpu.VMEM((2, max(kt, n_cache), qb), F32), pltpu.VMEM((2, vrows, qb), F32)],
        compiler_params=_cparams(("arbitrary", "arbitrary", "arbitrary")),
    )(*args)


def _layer_norm(z, g, b):
    mu = jnp.mean(z, axis=-1, keepdims=True)
    zc = z - mu
    var = jnp.mean(zc * zc, axis=-1, keepdims=True)
    return zc * lax.rsqrt(var + EPS) * g + b


def _merge_kernel(x_ref, ya_ref, yb_ref, yc_ref, m_ref, g1_ref, wb_ref, wo_ref, lg_ref, lb_ref, o_ref):
    half = x_ref.shape[0] // 2
    merged = []
    for p in range(2):
        rs = slice(p * half, (p + 1) * half)
        acc = None
        for i, y_ref in enumerate((ya_ref, yb_ref, yc_ref)):
            gate = _sigmoid(m_ref[rs, i * D_MODEL:(i + 1) * D_MODEL].astype(F32))
            term = gate * _dot(y_ref[rs, :], wb_ref[i])
            acc = term if acc is None else acc + term
        merged.append(acc.astype(BF16))
    for p in range(2):
        rs = slice(p * half, (p + 1) * half)
        out = _dot(merged[p], wo_ref[...])
        z = ALPHA * x_ref[rs, :] + g1_ref[...] * out
        o_ref[rs, :] = _layer_norm(z, lg_ref[...], lb_ref[...])


def _merge_call(x, ya, yb, yc, proj, mod, l, row0, span, wts):
    n = x.shape[0]
    tb = TOKEN_BLOCK
    bw = N_HEAD * DV
    tok = lambda w: pl.BlockSpec((tb, w), lambda i: (i, 0))
    return pl.pallas_call(
        _merge_kernel,
        grid=(n // tb,),
        in_specs=[tok(D_MODEL), tok(bw), tok(bw), tok(bw),
                  pl.BlockSpec((tb, N_GATE_COLS), lambda i: (i, 0)),
                  _mod_spec(l, row0, span, 2),
                  _resident((3, bw, D_MODEL), l), _resident((D_MODEL, D_MODEL), l),
                  _resident((1, D_MODEL), l, 0), _resident((1, D_MODEL), l, 0)],
        out_specs=tok(D_MODEL),
        out_shape=jax.ShapeDtypeStruct((n, D_MODEL), F32),
        compiler_params=_cparams(("arbitrary",)),
    )(x, ya, yb, yc, proj, mod, wts['w_branch'], wts['w_out'], wts['ln_g'], wts['ln_b'])


def _regroup_kernel(w_ref, o_ref):
    for c in range(D_FF // FF_CHUNK):
        lo, hi = c * FF_CHUNK, (c + 1) * FF_CHUNK
        o_ref[:, 2 * lo:2 * lo + FF_CHUNK] = w_ref[:, lo:hi].astype(BF16)
        o_ref[:, 2 * lo + FF_CHUNK:2 * hi] = w_ref[:, D_FF + lo:D_FF + hi].astype(BF16)


def _regroup_w_up(w_up):
    rows = D_MODEL // 4
    return pl.pallas_call(
        _regroup_kernel,
        grid=(DEPTH, D_MODEL // rows),
        in_specs=[pl.BlockSpec((None, rows, 2 * D_FF), lambda l, j: (l, j, 0))],
        out_specs=pl.BlockSpec((None, rows, 2 * D_FF), lambda l, j: (l, j, 0)),
        out_shape=jax.ShapeDtypeStruct((DEPTH, D_MODEL, 2 * D_FF), BF16),
        compiler_params=_cparams(("arbitrary", "arbitrary")),
    )(w_up)


def _gelu_tanh(x):
    k = -2.0 * 0.7978845608028654
    w = x * (x * x * (k * 0.044715) + k)
    return x / (1.0 + jnp.exp(w))


def _ffn_kernel(xp_ref, x_ref, xn_ref, sh_ref, sc_ref, g2_ref, wu_ref, cw_ref, cb_ref, wd_ref,
                lg_ref, lb_ref, o_ref, acc_ref, *, seq_len):
    tb = x_ref.shape[0]
    halo = SUBLANES
    i = pl.program_id(0)
    xm = x_ref[...]
    scale = 1.0 + sc_ref[...]
    shift = sh_ref[...]
    starts_seq = (i * tb) % seq_len == 0
    ends_seq = ((i + 1) * tb) % seq_len == 0
    h_prev = jnp.where(starts_seq, 0.0, xp_ref[...] * scale + shift)
    h_next = jnp.where(ends_seq, 0.0, xn_ref[...] * scale + shift)
    h_mid = xm * scale + shift
    hm = h_mid.astype(BF16)
    h = jnp.concatenate([h_prev, h_mid, h_next], axis=0).astype(BF16)
    rows = tb + 2 * halo
    n_chunk = D_FF // FF_CHUNK

    cut = (rows // 2 + 15) // 16 * 16

    def up(c):
        w = wu_ref[:, 2 * c * FF_CHUNK:2 * (c + 1) * FF_CHUNK]
        u = jnp.concatenate([_dot(h[:cut], w), _dot(h[cut:], w)], axis=0)
        return u[:, :FF_CHUNK], u[halo:halo + tb, FF_CHUNK:]

    ab_next = up(0)
    for c in range(n_chunk):
        lo, hi = c * FF_CHUNK, (c + 1) * FF_CHUNK
        a, b = ab_next
        if c + 1 < n_chunk:
            ab_next = up(c + 1)
        a_prev = pltpu.roll(a, 1, 0)[halo:halo + tb]
        a_mid = a[halo:halo + tb]
        a_next = pltpu.roll(a, rows - 1, 0)[halo:halo + tb]
        w0, w1, w2 = cw_ref[0:1, lo:hi], cw_ref[1:2, lo:hi], cw_ref[2:3, lo:hi]
        cb = cb_ref[:, lo:hi]
        cv = a_prev * w0 + a_mid * w1 + a_next * w2 + cb
        for r in range(seq_len, tb, seq_len):
            sl = slice(r - SUBLANES, r + SUBLANES)
            rid = lax.broadcasted_iota(jnp.int32, (2 * SUBLANES, 1), 0)
            seam = (jnp.where(rid == SUBLANES, 0.0, a_prev[sl]) * w0 + a_mid[sl] * w1
                    + jnp.where(rid == SUBLANES - 1, 0.0, a_next[sl]) * w2 + cb)
            cv = jnp.concatenate([cv[:r - SUBLANES], seam, cv[r + SUBLANES:]], axis=0)
        act = (_gelu_tanh(cv) * b).astype(BF16)
        part = _dot(act, wd_ref[lo:hi, :])
        if c == 0:
            acc_ref[...] = part
        else:
            acc_ref[...] += part
    z = ALPHA * xm + g2_ref[...] * acc_ref[...]
    o_ref[...] = _layer_norm(z, lg_ref[...], lb_ref[...])


def _ffn_call(x, mod, l, row0, span, seq_len, wts):
    n = x.shape[0]
    tb = TOKEN_BLOCK
    hb = tb // SUBLANES
    last = n // SUBLANES - 1
    modspec = functools.partial(_mod_spec, l, row0, span)

    return pl.pallas_call(
        functools.partial(_ffn_kernel, seq_len=seq_len),
        grid=(n // tb,),
        in_specs=[pl.BlockSpec((SUBLANES, D_MODEL), lambda i: (jnp.maximum(i * hb - 1, 0), 0)),
                  pl.BlockSpec((tb, D_MODEL), lambda i: (i, 0)),
                  pl.BlockSpec((SUBLANES, D_MODEL), lambda i: (jnp.minimum((i + 1) * hb, last), 0)),
                  modspec(3), modspec(4), modspec(5),
                  _resident((D_MODEL, 2 * D_FF), l), _resident((3, D_FF), l),
                  _resident((1, D_FF), l), _resident((D_FF, D_MODEL), l),
                  _resident((1, D_MODEL), l, 1), _resident((1, D_MODEL), l, 1)],
        out_specs=pl.BlockSpec((tb, D_MODEL), lambda i: (i, 0)),
        out_shape=jax.ShapeDtypeStruct((n, D_MODEL), F32),
        scratch_shapes=[pltpu.VMEM((tb, D_MODEL), F32)],
        compiler_params=_cparams(("arbitrary",)),
    )(x, x, x, mod, mod, mod, wts['w_up'], wts['conv_w'], wts['conv_b'], wts['w_down'],
      wts['ln_g'], wts['ln_b'])


def _rope_tables(seq):
    half = DK // 4
    t = np.arange(seq)
    inv = ROPE_BASE ** (-np.arange(half, dtype=np.float64) / half)
    ang_row = (t // GRID_W)[:, None] * inv[None, :]
    ang_col = (t % GRID_W)[:, None] * inv[None, :]
    ang = np.concatenate([ang_row, ang_row, ang_col, ang_col], axis=1)
    first = (np.arange(DK) % (2 * half)) < half
    cos = np.cos(ang)
    sa = np.where(first[None, :], -np.sin(ang), 0.0)
    sb = np.where(first[None, :], 0.0, np.sin(ang))
    tile = lambda a: jnp.asarray(np.concatenate([a, a], axis=1), F32)
    return tile(cos), tile(sa), tile(sb)


def _trunk(x, mod, row0, seq_len, batch, l, wts, tables, ctx, carried):
    n = x.shape[0]
    span = n if ctx is None else seq_len
    kv_prev = None if carried is None else carried[:2]
    outs = _proj_call(x, mod, l, row0, span, seq_len, wts, tables, ctx is None, kv_prev)
    proj, la = outs[0], outs[1]
    proj3 = proj.reshape(batch, seq_len, D_IN)
    la3 = la.reshape(batch, seq_len, 4 * LANES)
    if ctx is None:
        ret_prev, gla_prev = carried[2:]
        ya, st_ret = _ret_call(proj3, wts['ret_decay'], l, None, ret_prev)
        yb, st_gla = _gla_call(proj3, la3, wts['gla_norm_g'], l, None, gla_prev)
        yc = _diff_call(proj3, None, None, wts['diff_lam'], wts['diff_subln_g'], l)
        carried = (outs[2], outs[3], st_ret, st_gla)
    else:
        ya, = _ret_call(proj3, wts['ret_decay'], l, ctx['ret'], None)
        yb, = _gla_call(proj3, la3, wts['gla_norm_g'], l, ctx['gla'], None)
        yc = _diff_call(proj3, ctx['dk'][l], ctx['dv'][l], wts['diff_lam'], wts['diff_subln_g'], l)
    bw = N_HEAD * DV
    x1 = _merge_call(x, ya.reshape(n, bw), yb.reshape(n, bw), yc.reshape(n, bw), proj, mod, l,
                     row0, span, wts)
    x2 = _ffn_call(x1, mod, l, row0, span, seq_len, wts)
    return x2, carried


def kernel(x_prompt, x_sample, cache_diff_k, cache_diff_v, state_ret, state_gla, c, c_ctx,
           ada_w, ada_b, w_in, ret_decay, gla_wa1, gla_wa2, gla_ba, gla_norm_g, diff_lam,
           diff_subln_g, w_branch, w_out, ln_g, ln_b, ffn_w_up, ffn_conv_w, ffn_conv_b, ffn_w_down):
    bp, tp, _ = x_prompt.shape
    bs, ts, _ = x_sample.shape
    past = cache_diff_k.shape[3]

    cvec = jnp.concatenate([c_ctx[None, :], c, jnp.zeros((SUBLANES - 1 - bs, D_MODEL), F32)], axis=0)
    mod = _ada_call(cvec, ada_w, ada_b).reshape(DEPTH, SUBLANES, 6, 1, D_MODEL)

    wa1 = jnp.concatenate([gla_wa1[:, 0], gla_wa1[:, 1]], axis=-1)
    wa1 = jnp.pad(wa1, ((0, 0), (0, 0), (0, LANES - 2 * GLA_RANK))).astype(BF16)
    wa2 = jnp.zeros((DEPTH, LANES, 4 * LANES), F32)
    wa2 = wa2.at[:, 0:GLA_RANK, 0:2 * LANES].set(gla_wa2[:, 0])
    wa2 = wa2.at[:, GLA_RANK:2 * GLA_RANK, 2 * LANES:].set(gla_wa2[:, 1]).astype(BF16)
    wts = {
        'w_in': w_in.astype(BF16), 'wa1': wa1, 'wa2': wa2,
        'ba': gla_ba.reshape(DEPTH, 1, 4 * LANES), 'ret_decay': ret_decay,
        'gla_norm_g': gla_norm_g.reshape(DEPTH, 1, DV), 'diff_lam': diff_lam,
        'diff_subln_g': diff_subln_g.reshape(DEPTH, DV, 1),
        'w_branch': w_branch.astype(BF16), 'w_out': w_out.astype(BF16),
        'ln_g': ln_g.reshape(DEPTH, 2, 1, D_MODEL), 'ln_b': ln_b.reshape(DEPTH, 2, 1, D_MODEL),
        'w_up': _regroup_w_up(ffn_w_up), 'conv_w': ffn_conv_w,
        'conv_b': ffn_conv_b.reshape(DEPTH, 1, D_FF), 'w_down': ffn_w_down.astype(BF16),
    }

    h = x_prompt.reshape(bp * tp, D_MODEL)
    carried = (jnp.zeros((bp, DEPTH, 2 * N_HEAD, tp, DK), F32), jnp.zeros((bp, DEPTH, N_HEAD, tp, DV), F32),
               jnp.zeros((bp, DEPTH, 2, N_HEAD, DK, DV), F32), jnp.zeros((bp, DEPTH, 2, N_HEAD, DK, DV), F32))
    for l in range(DEPTH):
        h, carried = _trunk(h, mod, 0, tp, bp, l, wts, None, None, carried)
    y_prompt = h.reshape(bp, tp, D_MODEL)

    tables = _rope_tables(ts)
    ctx = {
        'dk': cache_diff_k.transpose(1, 0, 3, 2, 4).reshape(DEPTH, bs, past, 2 * N_HEAD * DK).astype(BF16),
        'dv': cache_diff_v.transpose(1, 0, 3, 2, 4).reshape(DEPTH, bs, past, N_HEAD * DV).astype(BF16),
        'ret': state_ret, 'gla': state_gla,
    }
    z = x_sample.reshape(bs * ts, D_MODEL)
    for l in range(DEPTH):
        z, _ = _trunk(z, mod, 1, ts, bs, l, wts, tables, ctx, None)
    y_sample = z.reshape(bs, ts, D_MODEL)

    return (y_prompt, y_sample) + tuple(carried)
```

```python
import functools
import math

import numpy as np
import jax
import jax.numpy as jnp
from jax import lax
from jax.experimental import pallas as pl
from jax.experimental.pallas import tpu as pltpu

F32 = jnp.float32
BF16 = jnp.bfloat16

D_MODEL = 1024
DEPTH = 2
GRID_W = 64
N_HEAD = 4
DK = 64
DV = 128
GLA_RANK = 16
GLA_TAU = 16.0
D_FF = 2816
ROPE_BASE = 10000.0
ALPHA = (2 * DEPTH) ** 0.25
EPS = 1e-5
D_IN = 7680

LANES = 128
SUBLANES = 8
VMEM_LIMIT = 56 * 1024 * 1024

N_GATE_COLS = 3 * D_MODEL
_QK_BLOCKS = N_HEAD * DK // LANES
_V_BLOCKS = N_HEAD * DV // LANES
_segment_blocks = (_QK_BLOCKS, _QK_BLOCKS, _V_BLOCKS, _V_BLOCKS,
                   _QK_BLOCKS, _QK_BLOCKS, _V_BLOCKS, _V_BLOCKS,
                   2 * _QK_BLOCKS, 2 * _QK_BLOCKS, _V_BLOCKS)
(CB_AQ, CB_AK, CB_AV, CB_AG, CB_BQ, CB_BK, CB_BV, CB_BR, CB_CQ, CB_CK, CB_CV) = (
    N_GATE_COLS // LANES + sum(_segment_blocks[:i]) for i in range(len(_segment_blocks)))
ROPE_BLOCKS = (tuple(range(CB_AQ, CB_AK + _QK_BLOCKS))
               + tuple(range(CB_CQ, CB_CK + 2 * _QK_BLOCKS)))
SCALED_BLOCKS = (tuple(range(CB_AK, CB_AK + _QK_BLOCKS))
                 + tuple(range(CB_BQ, CB_BQ + _QK_BLOCKS)))
SOFTMAX_Q_BLOCKS = tuple(range(CB_CQ, CB_CQ + 2 * _QK_BLOCKS))
QK_SCALE = DK ** -0.5
LOG2E = math.log2(math.e)

TOKEN_BLOCK = 512
RET_CHUNK = 256
GLA_BLOCK = 128
Q_BLOCK = 1024
DIFF_KEY_TILE = 512
DIFF_ONES_ROWS = 16
FF_CHUNK = 256


def _cparams(sem):
    return pltpu.CompilerParams(dimension_semantics=sem, vmem_limit_bytes=VMEM_LIMIT)


def _resident(shape, *lead):
    idx = tuple(lead) + (0,) * len(shape)
    return pl.BlockSpec((None,) * len(lead) + tuple(shape), lambda *_: idx,
                        pipeline_mode=pl.Buffered(1))


def _mod_spec(l, row0, span, k):
    return pl.BlockSpec((None, None, None, 1, D_MODEL),
                        lambda i: (l, row0 + (i * TOKEN_BLOCK) // span, k, 0, 0))


def _sigmoid(x):
    return 1.0 / (1.0 + jnp.exp(-x))


def _log_sigmoid(x):
    return jnp.minimum(x, 0.0) - jnp.log(1.0 + jnp.exp(-jnp.abs(x)))


def _dot(a, b):
    return jnp.dot(a, b, preferred_element_type=F32)


def _dot_nt(a, b):
    return lax.dot_general(a, b, (((1,), (1,)), ((), ())), preferred_element_type=F32)


def _dot_tn(a, b):
    return lax.dot_general(a, b, (((0,), (0,)), ((), ())), preferred_element_type=F32)


def _ada_kernel(c_ref, w_ref, b_ref, o_ref):
    c = c_ref[...]
    s = (c * _sigmoid(c)).astype(BF16)
    o_ref[...] = _dot(s, w_ref[...].astype(BF16)) + b_ref[...]


def _ada_call(cvec, ada_w, ada_b):
    nt = 1536
    return pl.pallas_call(
        _ada_kernel,
        grid=(DEPTH, 6 * D_MODEL // nt),
        in_specs=[pl.BlockSpec((SUBLANES, D_MODEL), lambda l, j: (0, 0)),
                  pl.BlockSpec((None, D_MODEL, nt), lambda l, j: (l, 0, j)),
                  pl.BlockSpec((None, 1, nt), lambda l, j: (l, 0, j))],
        out_specs=pl.BlockSpec((None, SUBLANES, nt), lambda l, j: (l, 0, j)),
        out_shape=jax.ShapeDtypeStruct((DEPTH, SUBLANES, 6 * D_MODEL), F32),
        compiler_params=_cparams(("arbitrary", "arbitrary")),
    )(cvec, ada_w, ada_b.reshape(DEPTH, 1, 6 * D_MODEL))


def _proj_kernel(*refs, rope, emit_kv, n_alias, seq_len):
    x_ref, sh_ref, sc_ref, w_ref, wa1_ref, wa2_ref, ba_ref = refs[:7]
    pos = 7
    if rope:
        cos_ref, sa_ref, sb_ref = refs[pos:pos + 3]
        pos += 3
    pos += n_alias
    proj_ref, la_ref = refs[pos:pos + 2]
    pos += 2
    if emit_kv:
        ck_ref, cv_ref = refs[pos:pos + 2]

    h = (x_ref[...] * (1.0 + sc_ref[...]) + sh_ref[...]).astype(BF16)
    tile = 512
    per = tile // LANES
    n_blk = D_IN // LANES
    r = _dot(h, wa1_ref[...]).astype(BF16)
    for j in range(D_IN // tile):
        acc = _dot(h, w_ref[:, j * tile:(j + 1) * tile])
        if j == 0:
            z = _dot(r, wa2_ref[...]) + ba_ref[...]
        if j == 1:
            la_ref[...] = _log_sigmoid(z) * (1.0 / GLA_TAU)
        for i in range(per):
            blk = (j * per + i + N_GATE_COLS // LANES) % n_blk
            y = acc[:, i * LANES:(i + 1) * LANES]
            if emit_kv and CB_CK <= blk < CB_CK + 4:
                for s in range(x_ref.shape[0] // seq_len):
                    for half in range(2):
                        ck_ref[s, 2 * (blk - CB_CK) + half] = (
                            y[s * seq_len:(s + 1) * seq_len, half * DK:(half + 1) * DK])
            if emit_kv and CB_CV <= blk < CB_CV + 4:
                for s in range(x_ref.shape[0] // seq_len):
                    cv_ref[s, blk - CB_CV] = y[s * seq_len:(s + 1) * seq_len, :]
            if rope and blk in ROPE_BLOCKS:
                y = (y * cos_ref[...] + pltpu.roll(y, LANES - 16, 1) * sa_ref[...]
                     + pltpu.roll(y, 16, 1) * sb_ref[...])
            if blk in SCALED_BLOCKS:
                y = y * QK_SCALE
            if blk in SOFTMAX_Q_BLOCKS:
                y = y * (QK_SCALE * LOG2E)
            proj_ref[:, blk * LANES:(blk + 1) * LANES] = y.astype(BF16)


def _proj_call(x, mod, l, row0, span, seq_len, wts, tables, emit_kv, kv_prev):
    n = x.shape[0]
    tb = TOKEN_BLOCK
    assert n % tb == 0 and (seq_len % tb == 0 or tb % seq_len == 0)
    bps = max(seq_len // tb, 1)
    rope = tables is not None

    in_specs = [pl.BlockSpec((tb, D_MODEL), lambda i: (i, 0)),
                _mod_spec(l, row0, span, 0), _mod_spec(l, row0, span, 1),
                _resident((D_MODEL, D_IN), l), _resident((D_MODEL, LANES), l),
                _resident((LANES, 4 * LANES), l), _resident((1, 4 * LANES), l)]
    args = [x, mod, mod, wts['w_in'], wts['wa1'], wts['wa2'], wts['ba']]
    if rope:
        in_specs += [pl.BlockSpec((tb, LANES), lambda i: (i % bps, 0))] * 3
        args += list(tables)
    aliases = {}
    if kv_prev is not None:
        aliases = {len(args): 2, len(args) + 1: 3}
        in_specs += [pl.BlockSpec(memory_space=pl.ANY)] * 2
        args += list(kv_prev)
    out_specs = [pl.BlockSpec((tb, D_IN), lambda i: (i, 0)),
                 pl.BlockSpec((tb, 4 * LANES), lambda i: (i, 0))]
    out_shape = [jax.ShapeDtypeStruct((n, D_IN), BF16), jax.ShapeDtypeStruct((n, 4 * LANES), F32)]
    if emit_kv:
        spb = tb // seq_len
        nseq = n // seq_len
        out_specs += [pl.BlockSpec((spb, None, 2 * N_HEAD, seq_len, DK), lambda i: (i, l, 0, 0, 0)),
                      pl.BlockSpec((spb, None, N_HEAD, seq_len, DV), lambda i: (i, l, 0, 0, 0))]
        out_shape += [jax.ShapeDtypeStruct((nseq, DEPTH, 2 * N_HEAD, seq_len, DK), F32),
                      jax.ShapeDtypeStruct((nseq, DEPTH, N_HEAD, seq_len, DV), F32)]
    return pl.pallas_call(
        functools.partial(_proj_kernel, rope=rope, emit_kv=emit_kv, n_alias=len(aliases),
                          seq_len=seq_len),
        grid=(n // tb,), in_specs=in_specs, out_specs=out_specs, out_shape=out_shape,
        input_output_aliases=aliases,
        compiler_params=_cparams(("arbitrary",)),
    )(*args)


def _ret_kernel(*refs, layer, seq, chunk, has_ctx, n_alias):
    dec_ref, q_ref, k_ref, v_ref, g_ref = refs[:5]
    pos = 5
    if has_ctx:
        s0_ref = refs[pos]
        pos += 1
    pos += n_alias
    y_ref = refs[pos]
    pos += 1
    if not has_ctx:
        st_ref = refs[pos]
        pos += 1
    of_ref, ob_ref, s_ref, dm_ref, eq_ref, ek_ref, gc_ref = refs[pos:]
    p = pl.program_id(1)
    nc = seq // chunk
    c_f = float(chunk)
    lane = lax.broadcasted_iota(jnp.int32, (1, LANES), 1)
    hmask = [lane < DK, lane >= DK]
    rowp = lax.broadcasted_iota(jnp.int32, (chunk, 1), 0).astype(F32)
    colp = lax.broadcasted_iota(jnp.int32, (1, chunk), 1).astype(F32)
    diff = rowp - colp

    @pl.when((pl.program_id(0) == 0) & (p == 0))
    def _():
        for pair in range(2):
            for d in range(2):
                for hh in range(2):
                    ti = 4 * pair + 2 * d + hh
                    raw = dec_ref[layer, d, 2 * pair + hh]
                    lg_l = _log_sigmoid(jnp.full((1, LANES), raw, F32))
                    lg_c = _log_sigmoid(jnp.full((1, chunk), raw, F32))
                    if d == 0:
                        dm_ref[ti] = jnp.where(diff >= 0, jnp.exp(jnp.maximum(diff, 0.0) * lg_c), 0.0)
                        eq_ref[ti] = jnp.exp((rowp + 1.0) * lg_l)
                        ek_ref[ti] = jnp.exp((c_f - 1.0 - rowp) * lg_l)
                    else:
                        dm_ref[ti] = jnp.where(diff <= 0, jnp.exp(jnp.maximum(-diff, 0.0) * lg_c), 0.0)
                        eq_ref[ti] = jnp.exp((c_f - rowp) * lg_l)
                        ek_ref[ti] = jnp.exp(rowp * lg_l)
                    gc_ref[ti] = jnp.exp(c_f * lg_l)

    for d in range(2):
        for hh in range(2):
            ci = 2 * d + hh
            if has_ctx:
                s0 = s0_ref[d, hh]
                zero = jnp.zeros((DK, DV), F32)
                s_ref[ci] = jnp.concatenate([s0, zero] if hh == 0 else [zero, s0], axis=0)
            else:
                s_ref[ci] = jnp.zeros((LANES, DV), F32)

    def step(n, carry):
        pending = []
        for d in range(2):
            c = n if d == 0 else nc - 1 - n
            r0 = pl.multiple_of(c * chunk, chunk)
            qc = q_ref[pl.ds(r0, chunk), :]
            kc = k_ref[pl.ds(r0, chunk), :]
            for hh in range(2):
                ci = 2 * d + hh
                qh = jnp.where(hmask[hh], qc, jnp.zeros_like(qc))
                kh = jnp.where(hmask[hh], kc, jnp.zeros_like(kc))
                vh = v_ref[pl.ds(r0, chunk), hh * DV:(hh + 1) * DV]
                sc = _dot_nt(qh, kh)
                sb = s_ref[ci]
                ti = 4 * p + ci
                qd = (qh.astype(F32) * eq_ref[ti]).astype(BF16)
                inter = _dot(qd, sb.astype(BF16))
                kd = (kh.astype(F32) * ek_ref[ti]).astype(BF16)
                s_ref[ci] = gc_ref[ti] * sb + _dot_tn(kd, vh)
                pending.append((d, hh, r0, sc, inter, vh))
        for d, hh, r0, sc, inter, vh in pending:
            o_ref = of_ref if d == 0 else ob_ref
            intra = _dot((sc * dm_ref[4 * p + 2 * d + hh]).astype(BF16), vh)
            o_ref[hh, pl.ds(r0, chunk), :] = intra + inter
        return carry

    lax.fori_loop(0, nc, step, 0)

    if not has_ctx:
        for d in range(2):
            for hh in range(2):
                st_ref[d, hh] = s_ref[2 * d + hh][hh * DK:(hh + 1) * DK, :]

    def finish(n, carry):
        r0 = pl.multiple_of(n * chunk, chunk)
        for hh in range(2):
            o = of_ref[hh, pl.ds(r0, chunk), :] + ob_ref[hh, pl.ds(r0, chunk), :]
            mu = jnp.mean(o, axis=-1, keepdims=True)
            oc = o - mu
            var = jnp.mean(oc * oc, axis=-1, keepdims=True)
            g = g_ref[pl.ds(r0, chunk), hh * DV:(hh + 1) * DV].astype(F32)
            y = oc * lax.rsqrt(var + EPS) * (g * _sigmoid(g))
            y_ref[pl.ds(r0, chunk), hh * DV:(hh + 1) * DV] = y.astype(BF16)
        return carry

    lax.fori_loop(0, nc, finish, 0)


def _state_io(l, batch, s0_all, st_prev, n_in):
    spec = pl.BlockSpec((None, None, 2, 2, DK, DV), lambda i, p: (i, l, 0, p, 0, 0))
    if s0_all is not None:
        return [spec], [s0_all], [], [], {}
    shape = jax.ShapeDtypeStruct((batch, DEPTH, 2, N_HEAD, DK, DV), F32)
    return [pl.BlockSpec(memory_space=pl.ANY)], [st_prev], [spec], [shape], {n_in: 1}


def _ret_call(proj3, decay, l, s0_all, st_prev):
    b, t, _ = proj3.shape
    chunk = min(RET_CHUNK, t)
    assert t % chunk == 0
    has_ctx = s0_all is not None
    in_specs = [pl.BlockSpec(memory_space=pltpu.SMEM),
                pl.BlockSpec((None, t, LANES), lambda i, p: (i, 0, CB_AQ + p)),
                pl.BlockSpec((None, t, LANES), lambda i, p: (i, 0, CB_AK + p)),
                pl.BlockSpec((None, t, 2 * DV), lambda i, p: (i, 0, CB_AV // 2 + p)),
                pl.BlockSpec((None, t, 2 * DV), lambda i, p: (i, 0, CB_AG // 2 + p))]
    args = [decay, proj3, proj3, proj3, proj3]
    st_in, st_args, st_out, st_shape, aliases = _state_io(l, b, s0_all, st_prev, len(args))
    return pl.pallas_call(
        functools.partial(_ret_kernel, layer=l, seq=t, chunk=chunk, has_ctx=has_ctx,
                          n_alias=len(aliases)),
        grid=(b, 2), in_specs=in_specs + st_in,
        out_specs=[pl.BlockSpec((None, t, 2 * DV), lambda i, p: (i, 0, p))] + st_out,
        out_shape=[jax.ShapeDtypeStruct((b, t, N_HEAD * DV), BF16)] + st_shape,
        input_output_aliases=aliases,
        scratch_shapes=[pltpu.VMEM((2, t, DV), F32), pltpu.VMEM((2, t, DV), F32),
                        pltpu.VMEM((4, LANES, DV), F32), pltpu.VMEM((8, chunk, chunk), F32),
                        pltpu.VMEM((8, chunk, LANES), F32), pltpu.VMEM((8, chunk, LANES), F32),
                        pltpu.VMEM((8, 1, LANES), F32)],
        compiler_params=_cparams(("arbitrary", "arbitrary")),
    )(*(args + st_args))


def _gla_levels(blk, reverse):
    t = lax.broadcasted_iota(jnp.int32, (blk, blk), 0)
    s = lax.broadcasted_iota(jnp.int32, (blk, blk), 1)
    xr = t ^ s
    lvl = jnp.zeros((blk, blk), jnp.int32)
    h = 2
    while h < blk:
        lvl = lvl + jnp.where(xr >= h, 1, 0)
        h *= 2
    allowed = (s > t) if reverse else (s < t)
    return jnp.where(t == s, -1, jnp.where(allowed, lvl, -2))


def _gla_sweep(q, k, la, hmask, lvl, blk, reverse):
    row = lax.broadcasted_iota(jnp.int32, (blk, 1), 0)
    w = la
    tot = la
    def both_heads(qx):
        return jnp.concatenate([jnp.where(m, qx, jnp.zeros_like(qx)) for m in hmask], axis=0)

    lvl2 = jnp.concatenate([lvl, lvl], axis=0)
    amat = jnp.where(lvl2 == -1, _dot_nt(both_heads(q.astype(BF16)), k.astype(BF16)), 0.0)
    h = 1
    li = 0
    while h < blk:
        up = (row & h) != 0
        qside = jnp.logical_not(up) if reverse else up
        f = jnp.exp(jnp.where(qside, w, tot - w))
        ql = (q * f).astype(BF16)
        kl = (k * f).astype(BF16)
        amat = jnp.where(lvl2 == li, _dot_nt(both_heads(ql), kl), amat)
        partner = jnp.where(up, pltpu.roll(tot, h, 0), pltpu.roll(tot, blk - h, 0))
        w = w + jnp.where(qside, partner, 0.0)
        tot = tot + partner
        h *= 2
        li += 1
    return amat, w, tot


def _gla_state_dots(q, k, w, tot, v_pair, st_list, hmask):
    qd = (q * jnp.exp(w)).astype(BF16)
    kd = (k * jnp.exp(tot - w)).astype(BF16)
    dec = jnp.exp(tot[0:1, :])
    inter, new_st = [], []
    for hh in range(2):
        vh = v_pair[:, hh * DV:(hh + 1) * DV]
        st = st_list[hh]
        qh = jnp.where(hmask[hh], qd, jnp.zeros_like(qd))
        kh = jnp.where(hmask[hh], kd, jnp.zeros_like(kd))
        inter.append(_dot_nt(qh, st.astype(BF16)))
        new_st.append(dec * st + _dot_tn(vh, kh))
    return inter, new_st


def _gla_kernel(*refs, seq, blk, has_ctx, n_alias):
    q_ref, k_ref, v_ref, r_ref, laf_ref, lab_ref, ng_ref = refs[:7]
    pos = 7
    if has_ctx:
        s0_ref = refs[pos]
        pos += 1
    pos += n_alias
    y_ref = refs[pos]
    pos += 1
    if not has_ctx:
        st_ref = refs[pos]
        pos += 1
    of_ref, ob_ref, s_ref, lv_ref = refs[pos:]
    nb = seq // blk
    lane = lax.broadcasted_iota(jnp.int32, (1, LANES), 1)
    hmask = [lane < DK, lane >= DK]

    @pl.when((pl.program_id(0) == 0) & (pl.program_id(1) == 0))
    def _():
        for d in range(2):
            lv_ref[d] = _gla_levels(blk, d == 1)

    for d in range(2):
        for hh in range(2):
            if has_ctx:
                zero = jnp.zeros((DK, DV), F32)
                s0 = s0_ref[d, hh]
                full = jnp.concatenate([s0, zero] if hh == 0 else [zero, s0], axis=0)
                s_ref[2 * d + hh] = full.T
            else:
                s_ref[2 * d + hh] = jnp.zeros((DV, LANES), F32)

    def step(n, carry):
        work = []
        for d in range(2):
            c = n if d == 0 else nb - 1 - n
            r0 = pl.multiple_of(c * blk, blk)
            q = q_ref[pl.ds(r0, blk), :].astype(F32)
            k = k_ref[pl.ds(r0, blk), :].astype(F32)
            la_ref = laf_ref if d == 0 else lab_ref
            amat, w, tot = _gla_sweep(q, k, la_ref[pl.ds(r0, blk), :], hmask, lv_ref[d], blk, d == 1)
            work.append((r0, q, k, amat, w, tot, v_ref[pl.ds(r0, blk), :]))
        inters = []
        for d, (r0, q, k, amat, w, tot, vp) in enumerate(work):
            inter, new_st = _gla_state_dots(q, k, w, tot, vp, [s_ref[2 * d], s_ref[2 * d + 1]], hmask)
            inters.append(inter)
            for hh in range(2):
                s_ref[2 * d + hh] = new_st[hh]
        for d, (r0, q, k, amat, w, tot, vp) in enumerate(work):
            o_ref = of_ref if d == 0 else ob_ref
            for hh in range(2):
                intra = _dot(amat[hh * blk:(hh + 1) * blk].astype(BF16), vp[:, hh * DV:(hh + 1) * DV])
                o_ref[hh, pl.ds(r0, blk), :] = intra + inters[d][hh]
        return carry

    lax.fori_loop(0, nb, step, 0)

    if not has_ctx:
        for d in range(2):
            for hh in range(2):
                st_ref[d, hh] = s_ref[2 * d + hh][...].T[hh * DK:(hh + 1) * DK, :]

    fin = min(seq, 512)

    def finish(n, carry):
        r0 = pl.multiple_of(n * fin, fin)
        for hh in range(2):
            o = of_ref[hh, pl.ds(r0, fin), :] + ob_ref[hh, pl.ds(r0, fin), :]
            ms = jnp.mean(o * o, axis=-1, keepdims=True)
            g = r_ref[pl.ds(r0, fin), hh * DV:(hh + 1) * DV].astype(F32)
            y = o * lax.rsqrt(ms + EPS) * ng_ref[...] * (g * _sigmoid(g))
            y_ref[pl.ds(r0, fin), hh * DV:(hh + 1) * DV] = y.astype(BF16)
        return carry

    lax.fori_loop(0, seq // fin, finish, 0)


def _gla_call(proj3, la3, norm_g, l, s0_all, st_prev):
    b, t, _ = proj3.shape
    blk = GLA_BLOCK
    assert t % blk == 0
    has_ctx = s0_all is not None
    in_specs = [pl.BlockSpec((None, t, LANES), lambda i, p: (i, 0, CB_BQ + p)),
                pl.BlockSpec((None, t, LANES), lambda i, p: (i, 0, CB_BK + p)),
                pl.BlockSpec((None, t, 2 * DV), lambda i, p: (i, 0, CB_BV // 2 + p)),
                pl.BlockSpec((None, t, 2 * DV), lambda i, p: (i, 0, CB_BR // 2 + p)),
                pl.BlockSpec((None, t, LANES), lambda i, p: (i, 0, p)),
                pl.BlockSpec((None, t, LANES), lambda i, p: (i, 0, 2 + p)),
                pl.BlockSpec((None, 1, DV), lambda i, p: (l, 0, 0))]
    args = [proj3, proj3, proj3, proj3, la3, la3, norm_g]
    st_in, st_args, st_out, st_shape, aliases = _state_io(l, b, s0_all, st_prev, len(args))
    return pl.pallas_call(
        functools.partial(_gla_kernel, seq=t, blk=blk, has_ctx=has_ctx, n_alias=len(aliases)),
        grid=(b, 2), in_specs=in_specs + st_in,
        out_specs=[pl.BlockSpec((None, t, 2 * DV), lambda i, p: (i, 0, p))] + st_out,
        out_shape=[jax.ShapeDtypeStruct((b, t, N_HEAD * DV), BF16)] + st_shape,
        input_output_aliases=aliases,
        scratch_shapes=[pltpu.VMEM((2, t, DV), F32), pltpu.VMEM((2, t, DV), F32),
                        pltpu.VMEM((4, DV, LANES), F32), pltpu.VMEM((2, blk, blk), jnp.int32)],
        compiler_params=_cparams(("arbitrary", "arbitrary")),
    )(*(args + st_args))


def _diff_kernel(*refs, lam_init, n_cache, n_new):
    lam_ref, q_ref = refs[:2]
    pos = 2
    if n_cache:
        kc_ref, vc_ref = refs[pos:pos + 2]
        pos += 2
    kn_ref, vn_ref, sg_ref, y_ref = refs[pos:pos + 4]
    pos += 4
    if n_cache:
        vtc_ref = refs[pos]
        pos += 1
    vtn_ref, sa_ref, sb_ref, acc_ref = refs[pos:]
    qb = q_ref.shape[0]
    kt = min(n_new, DIFF_KEY_TILE)
    n_tiles = n_new // kt

    @pl.when(pl.program_id(2) == 0)
    def _():
        if n_cache:
            vtc_ref[0:DV, :] = vc_ref[...].T
            vtc_ref[DV:, :] = jnp.ones((DIFF_ONES_ROWS, n_cache), BF16)
        for r in range(n_tiles):
            vtn_ref[r, 0:DV, :] = vn_ref[r * kt:(r + 1) * kt, :].T
            vtn_ref[r, DV:, :] = jnp.ones((DIFF_ONES_ROWS, kt), BF16)

    lam = _diff_lambda(lam_ref, lam_init)
    lane = lax.broadcasted_iota(jnp.int32, (1, LANES), 1)
    q = q_ref[...]
    qs =[jnp.where((lane < DK) if i == 0 else (lane >= DK), q, jnp.zeros_like(q)) for i in range(2)]

    s_bufs = (sa_ref, sb_ref)
    new0 = 1 if n_cache else 0
    last = new0 + n_tiles - 1

    def tile_rows(idx):
        return n_cache if idx < new0 else kt

    def scores(idx, r=None):
        if idx < new0:
            k_tile = kc_ref[...]
        elif r is None:
            k_tile = kn_ref[(idx - new0) * kt:(idx - new0 + 1) * kt, :]
        else:
            k_tile = kn_ref[pl.ds(pl.multiple_of(r * kt, kt), kt), :]
        for i in range(2):
            s_bufs[idx % 2][i, 0:tile_rows(idx), :] = _dot_nt(k_tile, qs[i])

    def soft(idx, m):
        n = tile_rows(idx)
        m_out, es, alphas = [], [], []
        for i in range(2):
            s = s_bufs[idx % 2][i, 0:n, :]
            m_new = jnp.maximum(m[i], jnp.max(s, axis=0, keepdims=True))
            es.append(jnp.exp2(s - m_new).astype(BF16))
            alphas.append(jnp.exp2(m[i] - m_new))
            m_out.append(m_new)
        return m_out, (es, alphas)

    def pv(idx, soft_out, r=None):
        es, alphas = soft_out
        if idx < new0:
            vt_tile = vtc_ref[...]
        else:
            vt_tile = vtn_ref[idx - new0 if r is None else r]
        for i in range(2):
            part = _dot(vt_tile, es[i])
            acc_ref[i] = part if idx == 0 else acc_ref[i] * alphas[i] + part

    m = [jnp.full((1, qb), -1e30, F32) for _ in range(2)]
    scores(0)
    if last >= 1:
        scores(1)
    m, so = soft(0, m)
    pv(0, so)
    n_pairs = max(last - 1, 0) // 2
    loop_end = 1 + 2 * n_pairs

    def pair(p, carry):
        m = list(carry)
        r = 2 * p + (1 - new0)
        scores(2, r + 1)
        m, so1 = soft(1, m)
        scores(3, r + 2)
        pv(1, so1, r)
        m, so2 = soft(2, m)
        pv(2, so2, r + 1)
        return tuple(m)

    if n_pairs:
        m = list(lax.fori_loop(0, n_pairs, pair, tuple(m)))
    for idx in range(loop_end, last + 1):
        if idx + 1 <= last:
            scores(idx + 1)
        m, so = soft(idx, m)
        pv(idx, so)

    acc0 = acc_ref[0]
    acc1 = acc_ref[1]
    o = (acc0[0:DV] * (1.0 / acc0[DV:DV + 1]) - acc1[0:DV] * (lam / acc1[DV:DV + 1]))
    ms = jnp.mean(o * o, axis=0, keepdims=True)
    y = o * lax.rsqrt(ms + EPS) * (sg_ref[...] * (1.0 - lam_init))
    y_ref[...] = y.T.astype(BF16)


def _diff_lambda(lam_ref, lam_init):
    lp = lam_ref[...]
    return (jnp.exp(jnp.sum(lp[0:1] * lp[1:2], axis=-1, keepdims=True))
            - jnp.exp(jnp.sum(lp[2:3] * lp[3:4], axis=-1, keepdims=True)) + lam_init)


def _diff_short_kernel(lam_ref, q_ref, k_ref, v_ref, sg_ref, y_ref, *, lam_init):
    t = q_ref.shape[0]
    lam = _diff_lambda(lam_ref, lam_init)
    lane = lax.broadcasted_iota(jnp.int32, (1, LANES), 1)
    ones = jnp.ones((DIFF_ONES_ROWS, t), BF16)
    cols = [slice(h * LANES, (h + 1) * LANES) for h in range(N_HEAD)]
    scores = []
    for h in range(N_HEAD):
        q = q_ref[:, cols[h]]
        k = k_ref[:, cols[h]]
        for i in range(2):
            qi = jnp.where((lane < DK) if i == 0 else (lane >= DK), q, jnp.zeros_like(q))
            scores.append(_dot_nt(k, qi))
    for h in range(N_HEAD):
        vt = jnp.concatenate([v_ref[:, cols[h]].T, ones], axis=0)
        acc = []
        for i in range(2):
            s = scores[2 * h + i]
            e = jnp.exp2(s - jnp.max(s, axis=0, keepdims=True)).astype(BF16)
            acc.append(_dot(vt, e))
        o = (acc[0][0:DV] * (1.0 / acc[0][DV:DV + 1]) - acc[1][0:DV] * (lam / acc[1][DV:DV + 1]))
        ms = jnp.mean(o * o, axis=0, keepdims=True)
        y = o * lax.rsqrt(ms + EPS) * (sg_ref[...] * (1.0 - lam_init))
        y_ref[:, cols[h]] = y.T.astype(BF16)


def _diff_short_call(proj3, lam_p, subln_col, l):
    b, t, _ = proj3.shape
    lam_init = 0.8 - 0.6 * math.exp(-0.3 * l)
    width = N_HEAD * LANES
    col = lambda cb: pl.BlockSpec((None, t, width), lambda i: (i, 0, cb * LANES // width))
    return pl.pallas_call(
        functools.partial(_diff_short_kernel, lam_init=lam_init),
        grid=(b,),
        in_specs=[pl.BlockSpec((None, 4, DK), lambda i: (l, 0, 0)), col(CB_CQ), col(CB_CK), col(CB_CV),
                  pl.BlockSpec((None, DV, 1), lambda i: (l, 0, 0))],
        out_specs=pl.BlockSpec((None, t, N_HEAD * DV), lambda i: (i, 0, 0)),
        out_shape=jax.ShapeDtypeStruct((b, t, N_HEAD * DV), BF16),
        compiler_params=_cparams(("arbitrary",)),
    )(lam_p, proj3, proj3, proj3, subln_col)


def _diff_call(proj3, cache_k, cache_v, lam_p, subln_col, l):
    b, t, _ = proj3.shape
    if cache_k is None and t <= DIFF_KEY_TILE:
        return _diff_short_call(proj3, lam_p, subln_col, l)
    n_cache = 0 if cache_k is None else cache_k.shape[1]
    qb = min(Q_BLOCK, t)
    kt = min(t, DIFF_KEY_TILE)
    assert t % qb == 0 and t % kt == 0
    vrows = DV + DIFF_ONES_ROWS
    lam_init = 0.8 - 0.6 * math.exp(-0.3 * l)
    in_specs = [pl.BlockSpec((None, 4, DK), lambda i, h, j: (l, 0, 0)),
                pl.BlockSpec((None, qb, LANES), lambda i, h, j: (i, j, CB_CQ + h))]
    args = [lam_p, proj3]
    if n_cache:
        in_specs += [pl.BlockSpec((None, n_cache, LANES), lambda i, h, j: (i, 0, h)),
                     pl.BlockSpec((None, n_cache, DV), lambda i, h, j: (i, 0, h))]
        args += [cache_k, cache_v]
    in_specs += [pl.BlockSpec((None, t, LANES), lambda i, h, j: (i, 0, CB_CK + h)),
                 pl.BlockSpec((None, t, DV), lambda i, h, j: (i, 0, CB_CV + h)),
                 pl.BlockSpec((None, DV, 1), lambda i, h, j: (l, 0, 0))]
    args += [proj3, proj3, subln_col]
    return pl.pallas_call(
        functools.partial(_diff_kernel, lam_init=lam_init, n_cache=n_cache, n_new=t),
        grid=(b, N_HEAD, t // qb), in_specs=in_specs,
        out_specs=pl.BlockSpec((None, qb, DV), lambda i, h, j: (i, j, h)),
        out_shape=jax.ShapeDtypeStruct((b, t, N_HEAD * DV), BF16),
        scratch_shapes=([pltpu.VMEM((vrows, n_cache), BF16)] if n_cache else [])
        + [pltpu.VMEM((t // kt, vrows, kt), BF16), pltpu.VMEM((2, max(kt, n_cache), qb), F32),
           pltpu.VMEM((2, max(kt, n_cache), qb), F32), pltpu.VMEM((2, vrows, qb), F32)],
        compiler_params=_cparams(("arbitrary", "arbitrary", "arbitrary")),
    )(*args)


def _layer_norm(z, g, b):
    mu = jnp.mean(z, axis=-1, keepdims=True)
    zc = z - mu
    var = jnp.mean(zc * zc, axis=-1, keepdims=True)
    return zc * lax.rsqrt(var + EPS) * g + b


def _merge_kernel(x_ref, ya_ref, yb_ref, yc_ref, m_ref, g1_ref, wb_ref, wo_ref, lg_ref, lb_ref, o_ref):
    half = x_ref.shape[0] // 2
    merged = []
    for p in range(2):
        rs = slice(p * half, (p + 1) * half)
        acc = None
        for i, y_ref in enumerate((ya_ref, yb_ref, yc_ref)):
            gate = _sigmoid(m_ref[rs, i * D_MODEL:(i + 1) * D_MODEL].astype(F32))
            term = gate * _dot(y_ref[rs, :], wb_ref[i])
            acc = term if acc is None else acc + term
        merged.append(acc.astype(BF16))
    for p in range(2):
        rs = slice(p * half, (p + 1) * half)
        out = _dot(merged[p], wo_ref[...])
        z = ALPHA * x_ref[rs, :] + g1_ref[...] * out
        o_ref[rs, :] = _layer_norm(z, lg_ref[...], lb_ref[...])


def _merge_call(x, ya, yb, yc, proj, mod, l, row0, span, wts):
    n = x.shape[0]
    tb = TOKEN_BLOCK
    bw = N_HEAD * DV
    tok = lambda w: pl.BlockSpec((tb, w), lambda i: (i, 0))
    return pl.pallas_call(
        _merge_kernel,
        grid=(n // tb,),
        in_specs=[tok(D_MODEL), tok(bw), tok(bw), tok(bw),
                  pl.BlockSpec((tb, N_GATE_COLS), lambda i: (i, 0)),
                  _mod_spec(l, row0, span, 2),
                  _resident((3, bw, D_MODEL), l), _resident((D_MODEL, D_MODEL), l),
                  _resident((1, D_MODEL), l, 0), _resident((1, D_MODEL), l, 0)],
        out_specs=tok(D_MODEL),
        out_shape=jax.ShapeDtypeStruct((n, D_MODEL), F32),
        compiler_params=_cparams(("arbitrary",)),
    )(x, ya, yb, yc, proj, mod, wts['w_branch'], wts['w_out'], wts['ln_g'], wts['ln_b'])


PREP_ROW_SPLIT = 4


def _regroup_kernel(w_ref, o_ref, *zero_refs):
    for c in range(D_FF // FF_CHUNK):
        lo, hi = c * FF_CHUNK, (c + 1) * FF_CHUNK
        o_ref[:, 2 * lo:2 * lo + FF_CHUNK] = w_ref[:, lo:hi].astype(BF16)
        o_ref[:, 2 * lo + FF_CHUNK:2 * hi] = w_ref[:, D_FF + lo:D_FF + hi].astype(BF16)
    for z_ref in zero_refs:
        z_ref[...] = jnp.zeros(z_ref.shape, z_ref.dtype)


def _regroup_w_up(w_up, zero_shapes):
    steps = DEPTH * PREP_ROW_SPLIT
    rows = D_MODEL // PREP_ROW_SPLIT
    assert all(s[0] % steps == 0 for s in zero_shapes)

    def slab(shape):
        tail = (0,) * (len(shape) - 1)
        return pl.BlockSpec((shape[0] // steps,) + tuple(shape[1:]),
                            lambda l, j: (l * PREP_ROW_SPLIT + j,) + tail)

    return pl.pallas_call(
        _regroup_kernel,
        grid=(DEPTH, PREP_ROW_SPLIT),
        in_specs=[pl.BlockSpec((None, rows, 2 * D_FF), lambda l, j: (l, j, 0))],
        out_specs=[pl.BlockSpec((None, rows, 2 * D_FF), lambda l, j: (l, j, 0))]
        + [slab(s) for s in zero_shapes],
        out_shape=[jax.ShapeDtypeStruct((DEPTH, D_MODEL, 2 * D_FF), BF16)]
        + [jax.ShapeDtypeStruct(s, F32) for s in zero_shapes],
        compiler_params=_cparams(("arbitrary", "arbitrary")),
    )(w_up)


def _cast_kernel(*refs):
    n = len(refs) // 2
    for w_ref, o_ref in zip(refs[:n], refs[n:]):
        o_ref[...] = w_ref[...].astype(BF16)


def _cast_weights(*ws):
    def spec(w):
        return pl.BlockSpec((None, w.shape[1] // PREP_ROW_SPLIT, w.shape[2]), lambda l, j: (l, j, 0))

    return pl.pallas_call(
        _cast_kernel,
        grid=(DEPTH, PREP_ROW_SPLIT),
        in_specs=[spec(w) for w in ws], out_specs=[spec(w) for w in ws],
        out_shape=[jax.ShapeDtypeStruct(w.shape, BF16) for w in ws],
        compiler_params=_cparams(("arbitrary", "arbitrary")),
    )(*ws)


def _gelu_tanh(x):
    k = -2.0 * 0.7978845608028654
    w = x * (x * x * (k * 0.044715) + k)
    return x / (1.0 + jnp.exp(w))


def _ffn_kernel(xp_ref, x_ref, xn_ref, sh_ref, sc_ref, g2_ref, wu_ref, cw_ref, cb_ref, wd_ref,
                lg_ref, lb_ref, o_ref, acc_ref, *, seq_len):
    tb = x_ref.shape[0]
    halo = SUBLANES
    i = pl.program_id(0)
    xm = x_ref[...]
    scale = 1.0 + sc_ref[...]
    shift = sh_ref[...]
    starts_seq = (i * tb) % seq_len == 0
    ends_seq = ((i + 1) * tb) % seq_len == 0
    h_prev = jnp.where(starts_seq, 0.0, xp_ref[...] * scale + shift)
    h_next = jnp.where(ends_seq, 0.0, xn_ref[...] * scale + shift)
    h_mid = xm * scale + shift
    hm = h_mid.astype(BF16)
    h = jnp.concatenate([h_prev, h_mid, h_next], axis=0).astype(BF16)
    rows = tb + 2 * halo
    n_chunk = D_FF // FF_CHUNK

    cut = (rows // 2 + 15) // 16 * 16

    def up(c):
        w = wu_ref[:, 2 * c * FF_CHUNK:2 * (c + 1) * FF_CHUNK]
        u = jnp.concatenate([_dot(h[:cut], w), _dot(h[cut:], w)], axis=0)
        return u[:, :FF_CHUNK], u[halo:halo + tb, FF_CHUNK:]

    ab_next = up(0)
    for c in range(n_chunk):
        lo, hi = c * FF_CHUNK, (c + 1) * FF_CHUNK
        a, b = ab_next
        if c + 1 < n_chunk:
            ab_next = up(c + 1)
        a_prev = pltpu.roll(a, 1, 0)[halo:halo + tb]
        a_mid = a[halo:halo + tb]
        a_next = pltpu.roll(a, rows - 1, 0)[halo:halo + tb]
        w0, w1, w2 = cw_ref[0:1, lo:hi], cw_ref[1:2, lo:hi], cw_ref[2:3, lo:hi]
        cb = cb_ref[:, lo:hi]
        cv = a_prev * w0 + a_mid * w1 + a_next * w2 + cb
        for r in range(seq_len, tb, seq_len):
            sl = slice(r - SUBLANES, r + SUBLANES)
            rid = lax.broadcasted_iota(jnp.int32, (2 * SUBLANES, 1), 0)
            seam = (jnp.where(rid == SUBLANES, 0.0, a_prev[sl]) * w0 + a_mid[sl] * w1
                    + jnp.where(rid == SUBLANES - 1, 0.0, a_next[sl]) * w2 + cb)
            cv = jnp.concatenate([cv[:r - SUBLANES], seam, cv[r + SUBLANES:]], axis=0)
        act = (_gelu_tanh(cv) * b).astype(BF16)
        part = _dot(act, wd_ref[lo:hi, :])
        if c == 0:
            acc_ref[...] = part
        else:
            acc_ref[...] += part
    z = ALPHA * xm + g2_ref[...] * acc_ref[...]
    o_ref[...] = _layer_norm(z, lg_ref[...], lb_ref[...])


def _ffn_call(x, mod, l, row0, span, seq_len, wts):
    n = x.shape[0]
    tb = TOKEN_BLOCK
    hb = tb // SUBLANES
    last = n // SUBLANES - 1
    modspec = functools.partial(_mod_spec, l, row0, span)

    return pl.pallas_call(
        functools.partial(_ffn_kernel, seq_len=seq_len),
        grid=(n // tb,),
        in_specs=[pl.BlockSpec((SUBLANES, D_MODEL), lambda i: (jnp.maximum(i * hb - 1, 0), 0)),
                  pl.BlockSpec((tb, D_MODEL), lambda i: (i, 0)),
                  pl.BlockSpec((SUBLANES, D_MODEL), lambda i: (jnp.minimum((i + 1) * hb, last), 0)),
                  modspec(3), modspec(4), modspec(5),
                  _resident((D_MODEL, 2 * D_FF), l), _resident((3, D_FF), l),
                  _resident((1, D_FF), l), _resident((D_FF, D_MODEL), l),
                  _resident((1, D_MODEL), l, 1), _resident((1, D_MODEL), l, 1)],
        out_specs=pl.BlockSpec((tb, D_MODEL), lambda i: (i, 0)),
        out_shape=jax.ShapeDtypeStruct((n, D_MODEL), F32),
        scratch_shapes=[pltpu.VMEM((tb, D_MODEL), F32)],
        compiler_params=_cparams(("arbitrary",)),
    )(x, x, x, mod, mod, mod, wts['w_up'], wts['conv_w'], wts['conv_b'], wts['w_down'],
      wts['ln_g'], wts['ln_b'])


def _rope_tables(seq):
    half = DK // 4
    t = np.arange(seq)
    inv = ROPE_BASE ** (-np.arange(half, dtype=np.float64) / half)
    ang_row = (t // GRID_W)[:, None] * inv[None, :]
    ang_col = (t % GRID_W)[:, None] * inv[None, :]
    ang = np.concatenate([ang_row, ang_row, ang_col, ang_col], axis=1)
    first = (np.arange(DK) % (2 * half)) < half
    cos = np.cos(ang)
    sa = np.where(first[None, :], -np.sin(ang), 0.0)
    sb = np.where(first[None, :], 0.0, np.sin(ang))
    tile = lambda a: jnp.asarray(np.concatenate([a, a], axis=1), F32)
    return tile(cos), tile(sa), tile(sb)


def _trunk(x, mod, row0, seq_len, batch, l, wts, tables, ctx, carried):
    n = x.shape[0]
    span = n if ctx is None else seq_len
    kv_prev = None if carried is None else carried[:2]
    outs = _proj_call(x, mod, l, row0, span, seq_len, wts, tables, ctx is None, kv_prev)
    proj, la = outs[0], outs[1]
    proj3 = proj.reshape(batch, seq_len, D_IN)
    la3 = la.reshape(batch, seq_len, 4 * LANES)
    if ctx is None:
        ret_prev, gla_prev = carried[2:]
        ya, st_ret = _ret_call(proj3, wts['ret_decay'], l, None, ret_prev)
        yb, st_gla = _gla_call(proj3, la3, wts['gla_norm_g'], l, None, gla_prev)
        yc = _diff_call(proj3, None, None, wts['diff_lam'], wts['diff_subln_g'], l)
        carried = (outs[2], outs[3], st_ret, st_gla)
    else:
        ya, = _ret_call(proj3, wts['ret_decay'], l, ctx['ret'], None)
        yb, = _gla_call(proj3, la3, wts['gla_norm_g'], l, ctx['gla'], None)
        yc = _diff_call(proj3, ctx['dk'][l], ctx['dv'][l], wts['diff_lam'], wts['diff_subln_g'], l)
    bw = N_HEAD * DV
    x1 = _merge_call(x, ya.reshape(n, bw), yb.reshape(n, bw), yc.reshape(n, bw), proj, mod, l,
                     row0, span, wts)
    x2 = _ffn_call(x1, mod, l, row0, span, seq_len, wts)
    return x2, carried


def kernel(x_prompt, x_sample, cache_diff_k, cache_diff_v, state_ret, state_gla, c, c_ctx,
           ada_w, ada_b, w_in, ret_decay, gla_wa1, gla_wa2, gla_ba, gla_norm_g, diff_lam,
           diff_subln_g, w_branch, w_out, ln_g, ln_b, ffn_w_up, ffn_conv_w, ffn_conv_b, ffn_w_down):
    bp, tp, _ = x_prompt.shape
    bs, ts, _ = x_sample.shape
    past = cache_diff_k.shape[3]

    cvec = jnp.concatenate([c_ctx[None, :], c, jnp.zeros((SUBLANES - 1 - bs, D_MODEL), F32)], axis=0)
    mod = _ada_call(cvec, ada_w, ada_b).reshape(DEPTH, SUBLANES, 6, 1, D_MODEL)

    wa1 = jnp.concatenate([gla_wa1[:, 0], gla_wa1[:, 1]], axis=-1)
    wa1 = jnp.pad(wa1, ((0, 0), (0, 0), (0, LANES - 2 * GLA_RANK))).astype(BF16)
    wa2 = jnp.zeros((DEPTH, LANES, 4 * LANES), F32)
    wa2 = wa2.at[:, 0:GLA_RANK, 0:2 * LANES].set(gla_wa2[:, 0])
    wa2 = wa2.at[:, GLA_RANK:2 * GLA_RANK, 2 * LANES:].set(gla_wa2[:, 1]).astype(BF16)
    bw = N_HEAD * DV
    w_in_b, w_branch_b, w_out_b, w_down_b = _cast_weights(
        w_in, w_branch.reshape(DEPTH, 3 * bw, D_MODEL), w_out, ffn_w_down)
    state_shape = (bp, DEPTH, 2, N_HEAD, DK, DV)
    w_up_b, *carried = _regroup_w_up(
        ffn_w_up, [(bp, DEPTH, 2 * N_HEAD, tp, DK), (bp, DEPTH, N_HEAD, tp, DV), state_shape, state_shape])
    wts = {
        'w_in': w_in_b, 'wa1': wa1, 'wa2': wa2,
        'ba': gla_ba.reshape(DEPTH, 1, 4 * LANES), 'ret_decay': ret_decay,
        'gla_norm_g': gla_norm_g.reshape(DEPTH, 1, DV), 'diff_lam': diff_lam,
        'diff_subln_g': diff_subln_g.reshape(DEPTH, DV, 1),
        'w_branch': w_branch_b.reshape(DEPTH, 3, bw, D_MODEL), 'w_out': w_out_b,
        'ln_g': ln_g.reshape(DEPTH, 2, 1, D_MODEL), 'ln_b': ln_b.reshape(DEPTH, 2, 1, D_MODEL),
        'w_up': w_up_b, 'conv_w': ffn_conv_w,
        'conv_b': ffn_conv_b.reshape(DEPTH, 1, D_FF), 'w_down': w_down_b,
    }

    h = x_prompt.reshape(bp * tp, D_MODEL)
    carried = tuple(carried)
    for l in range(DEPTH):
        h, carried = _trunk(h, mod, 0, tp, bp, l, wts, None, None, carried)
    y_prompt = h.reshape(bp, tp, D_MODEL)

    tables = _rope_tables(ts)
    ctx = {
        'dk': cache_diff_k.transpose(1, 0, 3, 2, 4).reshape(DEPTH, bs, past, 2 * N_HEAD * DK).astype(BF16),
        'dv': cache_diff_v.transpose(1, 0, 3, 2, 4).reshape(DEPTH, bs, past, N_HEAD * DV).astype(BF16),
        'ret': state_ret, 'gla': state_gla,
    }
    z = x_sample.reshape(bs * ts, D_MODEL)
    for l in range(DEPTH):
        z, _ = _trunk(z, mod, 1, ts, bs, l, wts, tables, ctx, None)
    y_sample = z.reshape(bs, ts, D_MODEL)

    return (y_prompt, y_sample) + tuple(carried)
```

```python
import functools
import math

import numpy as np
import jax
import jax.numpy as jnp
from jax import lax
from jax.experimental import pallas as pl
from jax.experimental.pallas import tpu as pltpu

F32 = jnp.float32
BF16 = jnp.bfloat16

D_MODEL = 1024
DEPTH = 2
GRID_W = 64
N_HEAD = 4
DK = 64
DV = 128
GLA_RANK = 16
GLA_TAU = 16.0
D_FF = 2816
ROPE_BASE = 10000.0
ALPHA = (2 * DEPTH) ** 0.25
EPS = 1e-5
D_IN = 7680

LANES = 128
SUBLANES = 8
VMEM_LIMIT = 56 * 1024 * 1024

N_GATE_COLS = 3 * D_MODEL
_QK_BLOCKS = N_HEAD * DK // LANES
_V_BLOCKS = N_HEAD * DV // LANES
_segment_blocks = (_QK_BLOCKS, _QK_BLOCKS, _V_BLOCKS, _V_BLOCKS,
                   _QK_BLOCKS, _QK_BLOCKS, _V_BLOCKS, _V_BLOCKS,
                   2 * _QK_BLOCKS, 2 * _QK_BLOCKS, _V_BLOCKS)
(CB_AQ, CB_AK, CB_AV, CB_AG, CB_BQ, CB_BK, CB_BV, CB_BR, CB_CQ, CB_CK, CB_CV) = (
    N_GATE_COLS // LANES + sum(_segment_blocks[:i]) for i in range(len(_segment_blocks)))
ROPE_BLOCKS = (tuple(range(CB_AQ, CB_AK + _QK_BLOCKS))
               + tuple(range(CB_CQ, CB_CK + 2 * _QK_BLOCKS)))
SCALED_BLOCKS = (tuple(range(CB_AK, CB_AK + _QK_BLOCKS))
                 + tuple(range(CB_BQ, CB_BQ + _QK_BLOCKS)))
SOFTMAX_Q_BLOCKS = tuple(range(CB_CQ, CB_CQ + 2 * _QK_BLOCKS))
QK_SCALE = DK ** -0.5
LOG2E = math.log2(math.e)

TOKEN_BLOCK = 512
RET_CHUNK = 256
GLA_BLOCK = 128
Q_BLOCK = 1024
DIFF_KEY_TILE = 512
DIFF_ONES_ROWS = 16
FF_CHUNK = 256


def _cparams(sem):
    return pltpu.CompilerParams(dimension_semantics=sem, vmem_limit_bytes=VMEM_LIMIT)


def _resident(shape, *lead):
    idx = tuple(lead) + (0,) * len(shape)
    return pl.BlockSpec((None,) * len(lead) + tuple(shape), lambda *_: idx,
                        pipeline_mode=pl.Buffered(1))


def _mod_spec(l, row0, span, k):
    return pl.BlockSpec((None, None, None, 1, D_MODEL),
                        lambda i: (l, row0 + (i * TOKEN_BLOCK) // span, k, 0, 0))


def _sigmoid(x):
    return 1.0 / (1.0 + jnp.exp(-x))


def _log_sigmoid(x):
    return jnp.minimum(x, 0.0) - jnp.log(1.0 + jnp.exp(-jnp.abs(x)))


def _dot(a, b):
    return jnp.dot(a, b, preferred_element_type=F32)


def _dot_nt(a, b):
    return lax.dot_general(a, b, (((1,), (1,)), ((), ())), preferred_element_type=F32)


def _dot_tn(a, b):
    return lax.dot_general(a, b, (((0,), (0,)), ((), ())), preferred_element_type=F32)


def _ada_kernel(c_ref, w_ref, b_ref, o_ref):
    c = c_ref[...]
    s = (c * _sigmoid(c)).astype(BF16)
    o_ref[...] = _dot(s, w_ref[...].astype(BF16)) + b_ref[...]


def _ada_call(cvec, ada_w, ada_b):
    nt = 1536
    return pl.pallas_call(
        _ada_kernel,
        grid=(DEPTH, 6 * D_MODEL // nt),
        in_specs=[pl.BlockSpec((SUBLANES, D_MODEL), lambda l, j: (0, 0)),
                  pl.BlockSpec((None, D_MODEL, nt), lambda l, j: (l, 0, j)),
                  pl.BlockSpec((None, 1, nt), lambda l, j: (l, 0, j))],
        out_specs=pl.BlockSpec((None, SUBLANES, nt), lambda l, j: (l, 0, j)),
        out_shape=jax.ShapeDtypeStruct((DEPTH, SUBLANES, 6 * D_MODEL), F32),
        compiler_params=_cparams(("arbitrary", "arbitrary")),
    )(cvec, ada_w, ada_b.reshape(DEPTH, 1, 6 * D_MODEL))


def _proj_kernel(*refs, rope, emit_kv, n_alias, seq_len):
    x_ref, sh_ref, sc_ref, w_ref, wa1_ref, wa2_ref, ba_ref = refs[:7]
    pos = 7
    if rope:
        cos_ref, sa_ref, sb_ref = refs[pos:pos + 3]
        pos += 3
    pos += n_alias
    proj_ref, la_ref = refs[pos:pos + 2]
    pos += 2
    if emit_kv:
        ck_ref, cv_ref = refs[pos:pos + 2]

    h = (x_ref[...] * (1.0 + sc_ref[...]) + sh_ref[...]).astype(BF16)
    tile = 512
    per = tile // LANES
    n_blk = D_IN // LANES
    r = _dot(h, wa1_ref[...]).astype(BF16)
    for j in range(D_IN // tile):
        acc = _dot(h, w_ref[:, j * tile:(j + 1) * tile])
        if j == 0:
            z = _dot(r, wa2_ref[...]) + ba_ref[...]
        if j == 1:
            la_ref[...] = _log_sigmoid(z) * (1.0 / GLA_TAU)
        for i in range(per):
            blk = (j * per + i + N_GATE_COLS // LANES) % n_blk
            y = acc[:, i * LANES:(i + 1) * LANES]
            if emit_kv and CB_CK <= blk < CB_CK + 4:
                for s in range(x_ref.shape[0] // seq_len):
                    for half in range(2):
                        ck_ref[s, 2 * (blk - CB_CK) + half] = (
                            y[s * seq_len:(s + 1) * seq_len, half * DK:(half + 1) * DK])
            if emit_kv and CB_CV <= blk < CB_CV + 4:
                for s in range(x_ref.shape[0] // seq_len):
                    cv_ref[s, blk - CB_CV] = y[s * seq_len:(s + 1) * seq_len, :]
            if rope and blk in ROPE_BLOCKS:
                y = (y * cos_ref[...] + pltpu.roll(y, LANES - 16, 1) * sa_ref[...]
                     + pltpu.roll(y, 16, 1) * sb_ref[...])
            if blk in SCALED_BLOCKS:
                y = y * QK_SCALE
            if blk in SOFTMAX_Q_BLOCKS:
                y = y * (QK_SCALE * LOG2E)
            proj_ref[:, blk * LANES:(blk + 1) * LANES] = y.astype(BF16)


def _proj_call(x, mod, l, row0, span, seq_len, wts, tables, emit_kv, kv_prev):
    n = x.shape[0]
    tb = TOKEN_BLOCK
    assert n % tb == 0 and (seq_len % tb == 0 or tb % seq_len == 0)
    bps = max(seq_len // tb, 1)
    rope = tables is not None

    in_specs = [pl.BlockSpec((tb, D_MODEL), lambda i: (i, 0)),
                _mod_spec(l, row0, span, 0), _mod_spec(l, row0, span, 1),
                _resident((D_MODEL, D_IN), l), _resident((D_MODEL, LANES), l),
                _resident((LANES, 4 * LANES), l), _resident((1, 4 * LANES), l)]
    args = [x, mod, mod, wts['w_in'], wts['wa1'], wts['wa2'], wts['ba']]
    if rope:
        in_specs += [pl.BlockSpec((tb, LANES), lambda i: (i % bps, 0))] * 3
        args += list(tables)
    aliases = {}
    if kv_prev is not None:
        aliases = {len(args): 2, len(args) + 1: 3}
        in_specs += [pl.BlockSpec(memory_space=pl.ANY)] * 2
        args += list(kv_prev)
    out_specs = [pl.BlockSpec((tb, D_IN), lambda i: (i, 0)),
                 pl.BlockSpec((tb, 4 * LANES), lambda i: (i, 0))]
    out_shape = [jax.ShapeDtypeStruct((n, D_IN), BF16), jax.ShapeDtypeStruct((n, 4 * LANES), F32)]
    if emit_kv:
        spb = tb // seq_len
        nseq = n // seq_len
        out_specs += [pl.BlockSpec((spb, None, 2 * N_HEAD, seq_len, DK), lambda i: (i, l, 0, 0, 0)),
                      pl.BlockSpec((spb, None, N_HEAD, seq_len, DV), lambda i: (i, l, 0, 0, 0))]
        out_shape += [jax.ShapeDtypeStruct((nseq, DEPTH, 2 * N_HEAD, seq_len, DK), F32),
                      jax.ShapeDtypeStruct((nseq, DEPTH, N_HEAD, seq_len, DV), F32)]
    return pl.pallas_call(
        functools.partial(_proj_kernel, rope=rope, emit_kv=emit_kv, n_alias=len(aliases),
                          seq_len=seq_len),
        grid=(n // tb,), in_specs=in_specs, out_specs=out_specs, out_shape=out_shape,
        input_output_aliases=aliases,
        compiler_params=_cparams(("arbitrary",)),
    )(*args)


def _ret_kernel(*refs, layer, seq, chunk, has_ctx, n_alias):
    dec_ref, q_ref, k_ref, v_ref, g_ref = refs[:5]
    pos = 5
    if has_ctx:
        s0_ref = refs[pos]
        pos += 1
    pos += n_alias
    y_ref = refs[pos]
    pos += 1
    if not has_ctx:
        st_ref = refs[pos]
        pos += 1
    of_ref, ob_ref, s_ref, dm_ref, eq_ref, ek_ref, gc_ref = refs[pos:]
    p = pl.program_id(1)
    nc = seq // chunk
    c_f = float(chunk)
    lane = lax.broadcasted_iota(jnp.int32, (1, LANES), 1)
    hmask = [lane < DK, lane >= DK]
    rowp = lax.broadcasted_iota(jnp.int32, (chunk, 1), 0).astype(F32)
    colp = lax.broadcasted_iota(jnp.int32, (1, chunk), 1).astype(F32)
    diff = rowp - colp

    @pl.when((pl.program_id(0) == 0) & (p == 0))
    def _():
        for pair in range(2):
            for d in range(2):
                for hh in range(2):
                    ti = 4 * pair + 2 * d + hh
                    raw = dec_ref[layer, d, 2 * pair + hh]
                    lg_l = _log_sigmoid(jnp.full((1, LANES), raw, F32))
                    lg_c = _log_sigmoid(jnp.full((1, chunk), raw, F32))
                    if d == 0:
                        dm_ref[ti] = jnp.where(diff >= 0, jnp.exp(jnp.maximum(diff, 0.0) * lg_c), 0.0)
                        eq_ref[ti] = jnp.exp((rowp + 1.0) * lg_l)
                        ek_ref[ti] = jnp.exp((c_f - 1.0 - rowp) * lg_l)
                    else:
                        dm_ref[ti] = jnp.where(diff <= 0, jnp.exp(jnp.maximum(-diff, 0.0) * lg_c), 0.0)
                        eq_ref[ti] = jnp.exp((c_f - rowp) * lg_l)
                        ek_ref[ti] = jnp.exp(rowp * lg_l)
                    gc_ref[ti] = jnp.exp(c_f * lg_l)

    for d in range(2):
        for hh in range(2):
            ci = 2 * d + hh
            if has_ctx:
                s0 = s0_ref[d, hh]
                zero = jnp.zeros((DK, DV), F32)
                s_ref[ci] = jnp.concatenate([s0, zero] if hh == 0 else [zero, s0], axis=0)
            else:
                s_ref[ci] = jnp.zeros((LANES, DV), F32)

    def step(n, carry):
        pending = []
        for d in range(2):
            c = n if d == 0 else nc - 1 - n
            r0 = pl.multiple_of(c * chunk, chunk)
            qc = q_ref[pl.ds(r0, chunk), :]
            kc = k_ref[pl.ds(r0, chunk), :]
            for hh in range(2):
                ci = 2 * d + hh
                qh = jnp.where(hmask[hh], qc, jnp.zeros_like(qc))
                kh = jnp.where(hmask[hh], kc, jnp.zeros_like(kc))
                vh = v_ref[pl.ds(r0, chunk), hh * DV:(hh + 1) * DV]
                sc = _dot_nt(qh, kh)
                sb = s_ref[ci]
                ti = 4 * p + ci
                qd = (qh.astype(F32) * eq_ref[ti]).astype(BF16)
                inter = _dot(qd, sb.astype(BF16))
                kd = (kh.astype(F32) * ek_ref[ti]).astype(BF16)
                s_ref[ci] = gc_ref[ti] * sb + _dot_tn(kd, vh)
                pending.append((d, hh, r0, sc, inter, vh))
        for d, hh, r0, sc, inter, vh in pending:
            o_ref = of_ref if d == 0 else ob_ref
            intra = _dot((sc * dm_ref[4 * p + 2 * d + hh]).astype(BF16), vh)
            o_ref[hh, pl.ds(r0, chunk), :] = intra + inter
        return carry

    lax.fori_loop(0, nc, step, 0)

    if not has_ctx:
        for d in range(2):
            for hh in range(2):
                st_ref[d, hh] = s_ref[2 * d + hh][hh * DK:(hh + 1) * DK, :]

    def finish(n, carry):
        r0 = pl.multiple_of(n * chunk, chunk)
        for hh in range(2):
            o = of_ref[hh, pl.ds(r0, chunk), :] + ob_ref[hh, pl.ds(r0, chunk), :]
            mu = jnp.mean(o, axis=-1, keepdims=True)
            oc = o - mu
            var = jnp.mean(oc * oc, axis=-1, keepdims=True)
            g = g_ref[pl.ds(r0, chunk), hh * DV:(hh + 1) * DV].astype(F32)
            y = oc * lax.rsqrt(var + EPS) * (g * _sigmoid(g))
            y_ref[pl.ds(r0, chunk), hh * DV:(hh + 1) * DV] = y.astype(BF16)
        return carry

    lax.fori_loop(0, nc, finish, 0)


def _state_io(l, batch, s0_all, st_prev, n_in, pairs=1):
    spec = pl.BlockSpec((None, None, 2, 2 * pairs, DK, DV), lambda i, p: (i, l, 0, p, 0, 0))
    if s0_all is not None:
        return [spec], [s0_all], [], [], {}
    shape = jax.ShapeDtypeStruct((batch, DEPTH, 2, N_HEAD, DK, DV), F32)
    return [pl.BlockSpec(memory_space=pl.ANY)], [st_prev], [spec], [shape], {n_in: 1}


def _ret_call(proj3, decay, l, s0_all, st_prev):
    b, t, _ = proj3.shape
    chunk = min(RET_CHUNK, t)
    assert t % chunk == 0
    has_ctx = s0_all is not None
    in_specs = [pl.BlockSpec(memory_space=pltpu.SMEM),
                pl.BlockSpec((None, t, LANES), lambda i, p: (i, 0, CB_AQ + p)),
                pl.BlockSpec((None, t, LANES), lambda i, p: (i, 0, CB_AK + p)),
                pl.BlockSpec((None, t, 2 * DV), lambda i, p: (i, 0, CB_AV // 2 + p)),
                pl.BlockSpec((None, t, 2 * DV), lambda i, p: (i, 0, CB_AG // 2 + p))]
    args = [decay, proj3, proj3, proj3, proj3]
    st_in, st_args, st_out, st_shape, aliases = _state_io(l, b, s0_all, st_prev, len(args))
    return pl.pallas_call(
        functools.partial(_ret_kernel, layer=l, seq=t, chunk=chunk, has_ctx=has_ctx,
                          n_alias=len(aliases)),
        grid=(b, 2), in_specs=in_specs + st_in,
        out_specs=[pl.BlockSpec((None, t, 2 * DV), lambda i, p: (i, 0, p))] + st_out,
        out_shape=[jax.ShapeDtypeStruct((b, t, N_HEAD * DV), BF16)] + st_shape,
        input_output_aliases=aliases,
        scratch_shapes=[pltpu.VMEM((2, t, DV), F32), pltpu.VMEM((2, t, DV), F32),
                        pltpu.VMEM((4, LANES, DV), F32), pltpu.VMEM((8, chunk, chunk), F32),
                        pltpu.VMEM((8, chunk, LANES), F32), pltpu.VMEM((8, chunk, LANES), F32),
                        pltpu.VMEM((8, 1, LANES), F32)],
        compiler_params=_cparams(("arbitrary", "arbitrary")),
    )(*(args + st_args))


def _gla_levels(blk, reverse):
    t = lax.broadcasted_iota(jnp.int32, (blk, blk), 0)
    s = lax.broadcasted_iota(jnp.int32, (blk, blk), 1)
    xr = t ^ s
    lvl = jnp.zeros((blk, blk), jnp.int32)
    h = 2
    while h < blk:
        lvl = lvl + jnp.where(xr >= h, 1, 0)
        h *= 2
    allowed = (s > t) if reverse else (s < t)
    return jnp.where(t == s, -1, jnp.where(allowed, lvl, -2))


def _gla_sweep(q, k, la, hmask, lvl, blk, reverse):
    row = lax.broadcasted_iota(jnp.int32, (blk, 1), 0)
    w = la
    tot = la
    def both_heads(qx):
        return jnp.concatenate([jnp.where(m, qx, jnp.zeros_like(qx)) for m in hmask], axis=0)

    lvl2 = jnp.concatenate([lvl, lvl], axis=0)
    amat = jnp.where(lvl2 == -1, _dot_nt(both_heads(q.astype(BF16)), k.astype(BF16)), 0.0)
    h = 1
    li = 0
    while h < blk:
        up = (row & h) != 0
        qside = jnp.logical_not(up) if reverse else up
        f = jnp.exp(jnp.where(qside, w, tot - w))
        ql = (q * f).astype(BF16)
        kl = (k * f).astype(BF16)
        amat = jnp.where(lvl2 == li, _dot_nt(both_heads(ql), kl), amat)
        partner = jnp.where(up, pltpu.roll(tot, h, 0), pltpu.roll(tot, blk - h, 0))
        w = w + jnp.where(qside, partner, 0.0)
        tot = tot + partner
        h *= 2
        li += 1
    return amat, w, tot


def _gla_state_dots(q, k, w, tot, v_pair, st_list, hmask):
    qd = (q * jnp.exp(w)).astype(BF16)
    kd = (k * jnp.exp(tot - w)).astype(BF16)
    dec = jnp.exp(tot[0:1, :])
    inter, new_st = [], []
    for hh in range(2):
        vh = v_pair[:, hh * DV:(hh + 1) * DV]
        st = st_list[hh]
        qh = jnp.where(hmask[hh], qd, jnp.zeros_like(qd))
        kh = jnp.where(hmask[hh], kd, jnp.zeros_like(kd))
        inter.append(_dot_nt(qh, st.astype(BF16)))
        new_st.append(dec * st + _dot_tn(vh, kh))
    return inter, new_st


def _gla_kernel(*refs, seq, blk, has_ctx, n_alias):
    q_ref, k_ref, v_ref, r_ref, laf_ref, lab_ref, ng_ref = refs[:7]
    pos = 7
    if has_ctx:
        s0_ref = refs[pos]
        pos += 1
    pos += n_alias
    y_ref = refs[pos]
    pos += 1
    if not has_ctx:
        st_ref = refs[pos]
        pos += 1
    of_ref, ob_ref, s_ref, lv_ref = refs[pos:]
    nb = seq // blk
    n_pair = q_ref.shape[1] // LANES
    heads = 2 * n_pair
    lane = lax.broadcasted_iota(jnp.int32, (1, LANES), 1)
    hmask = [lane < DK, lane >= DK]

    @pl.when((pl.program_id(0) == 0) & (pl.program_id(1) == 0))
    def _():
        for d in range(2):
            lv_ref[d] = _gla_levels(blk, d == 1)

    for d in range(2):
        for head in range(heads):
            hh = head % 2
            if has_ctx:
                zero = jnp.zeros((DK, DV), F32)
                s0 = s0_ref[d, head]
                full = jnp.concatenate([s0, zero] if hh == 0 else [zero, s0], axis=0)
                s_ref[heads * d + head] = full.T
            else:
                s_ref[heads * d + head] = jnp.zeros((DV, LANES), F32)

    def step(n, carry):
        work = []
        for pair in range(n_pair):
            pc = slice(pair * LANES, (pair + 1) * LANES)
            vc = slice(pair * 2 * DV, (pair + 1) * 2 * DV)
            for d in range(2):
                c = n if d == 0 else nb - 1 - n
                r0 = pl.multiple_of(c * blk, blk)
                q = q_ref[pl.ds(r0, blk), pc].astype(F32)
                k = k_ref[pl.ds(r0, blk), pc].astype(F32)
                la_ref = laf_ref if d == 0 else lab_ref
                amat, w, tot = _gla_sweep(q, k, la_ref[pl.ds(r0, blk), pc], hmask, lv_ref[d], blk, d == 1)
                work.append((pair, d, r0, q, k, amat, w, tot, v_ref[pl.ds(r0, blk), vc]))
        inters = []
        for pair, d, r0, q, k, amat, w, tot, vp in work:
            base = heads * d + 2 * pair
            inter, new_st = _gla_state_dots(q, k, w, tot, vp, [s_ref[base], s_ref[base + 1]], hmask)
            inters.append(inter)
            for hh in range(2):
                s_ref[base + hh] = new_st[hh]
        for (pair, d, r0, q, k, amat, w, tot, vp), inter in zip(work, inters):
            o_ref = of_ref if d == 0 else ob_ref
            for hh in range(2):
                intra = _dot(amat[hh * blk:(hh + 1) * blk].astype(BF16), vp[:, hh * DV:(hh + 1) * DV])
                o_ref[2 * pair + hh, pl.ds(r0, blk), :] = intra + inter[hh]
        return carry

    lax.fori_loop(0, nb, step, 0)

    if not has_ctx:
        for d in range(2):
            for head in range(heads):
                hh = head % 2
                st_ref[d, head] = s_ref[heads * d + head][...].T[hh * DK:(hh + 1) * DK, :]

    fin = min(seq, 512)

    def finish(n, carry):
        r0 = pl.multiple_of(n * fin, fin)
        for head in range(heads):
            o = of_ref[head, pl.ds(r0, fin), :] + ob_ref[head, pl.ds(r0, fin), :]
            ms = jnp.mean(o * o, axis=-1, keepdims=True)
            g = r_ref[pl.ds(r0, fin), head * DV:(head + 1) * DV].astype(F32)
            y = o * lax.rsqrt(ms + EPS) * ng_ref[...] * (g * _sigmoid(g))
            y_ref[pl.ds(r0, fin), head * DV:(head + 1) * DV] = y.astype(BF16)
        return carry

    lax.fori_loop(0, seq // fin, finish, 0)


def _gla_call(proj3, la3, norm_g, l, s0_all, st_prev):
    b, t, _ = proj3.shape
    blk = GLA_BLOCK
    assert t % blk == 0
    has_ctx = s0_all is not None
    pairs = 2 if 2 * 8192 * t <= VMEM_LIMIT // 2 else 1
    qk_w = pairs * LANES
    v_w = pairs * 2 * DV
    col = lambda w, cb: pl.BlockSpec((None, t, w), lambda i, p: (i, 0, cb * LANES // w + p))
    in_specs = [col(qk_w, CB_BQ), col(qk_w, CB_BK), col(v_w, CB_BV), col(v_w, CB_BR),
                pl.BlockSpec((None, t, qk_w), lambda i, p: (i, 0, p)),
                pl.BlockSpec((None, t, qk_w), lambda i, p: (i, 0, 2 // pairs + p)),
                pl.BlockSpec((None, 1, DV), lambda i, p: (l, 0, 0))]
    args = [proj3, proj3, proj3, proj3, la3, la3, norm_g]
    st_in, st_args, st_out, st_shape, aliases = _state_io(l, b, s0_all, st_prev, len(args), pairs)
    return pl.pallas_call(
        functools.partial(_gla_kernel, seq=t, blk=blk, has_ctx=has_ctx, n_alias=len(aliases)),
        grid=(b, 2 // pairs), in_specs=in_specs + st_in,
        out_specs=[pl.BlockSpec((None, t, v_w), lambda i, p: (i, 0, p))] + st_out,
        out_shape=[jax.ShapeDtypeStruct((b, t, N_HEAD * DV), BF16)] + st_shape,
        input_output_aliases=aliases,
        scratch_shapes=[pltpu.VMEM((2 * pairs, t, DV), F32), pltpu.VMEM((2 * pairs, t, DV), F32),
                        pltpu.VMEM((4 * pairs, DV, LANES), F32), pltpu.VMEM((2, blk, blk), jnp.int32)],
        compiler_params=_cparams(("arbitrary", "arbitrary")),
    )(*(args + st_args))


def _diff_kernel(*refs, lam_init, n_cache, n_new):
    lam_ref, q_ref = refs[:2]
    pos = 2
    if n_cache:
        kc_ref, vc_ref = refs[pos:pos + 2]
        pos += 2
    kn_ref, vn_ref, sg_ref, y_ref = refs[pos:pos + 4]
    pos += 4
    if n_cache:
        vtc_ref = refs[pos]
        pos += 1
    vtn_ref, sa_ref, sb_ref, acc_ref = refs[pos:]
    qb = q_ref.shape[0]
    kt = min(n_new, DIFF_KEY_TILE)
    n_tiles = n_new // kt

    @pl.when(pl.program_id(2) == 0)
    def _():
        if n_cache:
            vtc_ref[0:DV, :] = vc_ref[...].T
            vtc_ref[DV:, :] = jnp.ones((DIFF_ONES_ROWS, n_cache), BF16)
        for r in range(n_tiles):
            vtn_ref[r, 0:DV, :] = vn_ref[r * kt:(r + 1) * kt, :].T
            vtn_ref[r, DV:, :] = jnp.ones((DIFF_ONES_ROWS, kt), BF16)

    lam = _diff_lambda(lam_ref, lam_init)
    lane = lax.broadcasted_iota(jnp.int32, (1, LANES), 1)
    q = q_ref[...]
    qs =[jnp.where((lane < DK) if i == 0 else (lane >= DK), q, jnp.zeros_like(q)) for i in range(2)]

    s_bufs = (sa_ref, sb_ref)
    new0 = 1 if n_cache else 0
    last = new0 + n_tiles - 1

    def tile_rows(idx):
        return n_cache if idx < new0 else kt

    def scores(idx, r=None):
        if idx < new0:
            k_tile = kc_ref[...]
        elif r is None:
            k_tile = kn_ref[(idx - new0) * kt:(idx - new0 + 1) * kt, :]
        else:
            k_tile = kn_ref[pl.ds(pl.multiple_of(r * kt, kt), kt), :]
        for i in range(2):
            s_bufs[idx % 2][i, 0:tile_rows(idx), :] = _dot_nt(k_tile, qs[i])

    def soft(idx, m):
        n = tile_rows(idx)
        m_out, es, alphas = [], [], []
        for i in range(2):
            s = s_bufs[idx % 2][i, 0:n, :]
            m_new = jnp.maximum(m[i], jnp.max(s, axis=0, keepdims=True))
            es.append(jnp.exp2(s - m_new).astype(BF16))
            alphas.append(jnp.exp2(m[i] - m_new))
            m_out.append(m_new)
        return m_out, (es, alphas)

    def pv(idx, soft_out, r=None):
        es, alphas = soft_out
        if idx < new0:
            vt_tile = vtc_ref[...]
        else:
            vt_tile = vtn_ref[idx - new0 if r is None else r]
        for i in range(2):
            part = _dot(vt_tile, es[i])
            acc_ref[i] = part if idx == 0 else acc_ref[i] * alphas[i] + part

    m = [jnp.full((1, qb), -1e30, F32) for _ in range(2)]
    scores(0)
    if last >= 1:
        scores(1)
    m, so = soft(0, m)
    pv(0, so)
    n_pairs = max(last - 1, 0) // 2
    loop_end = 1 + 2 * n_pairs

    def pair(p, carry):
        m = list(carry)
        r = 2 * p + (1 - new0)
        scores(2, r + 1)
        m, so1 = soft(1, m)
        scores(3, r + 2)
        pv(1, so1, r)
        m, so2 = soft(2, m)
        pv(2, so2, r + 1)
        return tuple(m)

    if n_pairs:
        m = list(lax.fori_loop(0, n_pairs, pair, tuple(m)))
    for idx in range(loop_end, last + 1):
        if idx + 1 <= last:
            scores(idx + 1)
        m, so = soft(idx, m)
        pv(idx, so)

    acc0 = acc_ref[0]
    acc1 = acc_ref[1]
    o = (acc0[0:DV] * (1.0 / acc0[DV:DV + 1]) - acc1[0:DV] * (lam / acc1[DV:DV + 1]))
    ms = jnp.mean(o * o, axis=0, keepdims=True)
    y = o * lax.rsqrt(ms + EPS) * (sg_ref[...] * (1.0 - lam_init))
    y_ref[...] = y.T.astype(BF16)


def _diff_lambda(lam_ref, lam_init):
    lp = lam_ref[...]
    return (jnp.exp(jnp.sum(lp[0:1] * lp[1:2], axis=-1, keepdims=True))
            - jnp.exp(jnp.sum(lp[2:3] * lp[3:4], axis=-1, keepdims=True)) + lam_init)


def _diff_short_kernel(lam_ref, q_ref, k_ref, v_ref, sg_ref, y_ref, *, lam_init):
    t = q_ref.shape[0]
    lam = _diff_lambda(lam_ref, lam_init)
    lane = lax.broadcasted_iota(jnp.int32, (1, LANES), 1)
    ones = jnp.ones((DIFF_ONES_ROWS, t), BF16)
    cols = [slice(h * LANES, (h + 1) * LANES) for h in range(N_HEAD)]
    scores = []
    for h in range(N_HEAD):
        q = q_ref[:, cols[h]]
        k = k_ref[:, cols[h]]
        for i in range(2):
            qi = jnp.where((lane < DK) if i == 0 else (lane >= DK), q, jnp.zeros_like(q))
            scores.append(_dot_nt(k, qi))
    for h in range(N_HEAD):
        vt = jnp.concatenate([v_ref[:, cols[h]].T, ones], axis=0)
        acc = []
        for i in range(2):
            s = scores[2 * h + i]
            e = jnp.exp2(s - jnp.max(s, axis=0, keepdims=True)).astype(BF16)
            acc.append(_dot(vt, e))
        o = (acc[0][0:DV] * (1.0 / acc[0][DV:DV + 1]) - acc[1][0:DV] * (lam / acc[1][DV:DV + 1]))
        ms = jnp.mean(o * o, axis=0, keepdims=True)
        y = o * lax.rsqrt(ms + EPS) * (sg_ref[...] * (1.0 - lam_init))
        y_ref[:, cols[h]] = y.T.astype(BF16)


def _diff_short_call(proj3, lam_p, subln_col, l):
    b, t, _ = proj3.shape
    lam_init = 0.8 - 0.6 * math.exp(-0.3 * l)
    width = N_HEAD * LANES
    col = lambda cb: pl.BlockSpec((None, t, width), lambda i: (i, 0, cb * LANES // width))
    return pl.pallas_call(
        functools.partial(_diff_short_kernel, lam_init=lam_init),
        grid=(b,),
        in_specs=[pl.BlockSpec((None, 4, DK), lambda i: (l, 0, 0)), col(CB_CQ), col(CB_CK), col(CB_CV),
                  pl.BlockSpec((None, DV, 1), lambda i: (l, 0, 0))],
        out_specs=pl.BlockSpec((None, t, N_HEAD * DV), lambda i: (i, 0, 0)),
        out_shape=jax.ShapeDtypeStruct((b, t, N_HEAD * DV), BF16),
        compiler_params=_cparams(("arbitrary",)),
    )(lam_p, proj3, proj3, proj3, subln_col)


def _diff_call(proj3, cache_k, cache_v, lam_p, subln_col, l):
    b, t, _ = proj3.shape
    if cache_k is None and t <= DIFF_KEY_TILE:
        return _diff_short_call(proj3, lam_p, subln_col, l)
    n_cache = 0 if cache_k is None else cache_k.shape[1]
    qb = min(Q_BLOCK, t)
    kt = min(t, DIFF_KEY_TILE)
    assert t % qb == 0 and t % kt == 0
    vrows = DV + DIFF_ONES_ROWS
    lam_init = 0.8 - 0.6 * math.exp(-0.3 * l)
    in_specs = [pl.BlockSpec((None, 4, DK), lambda i, h, j: (l, 0, 0)),
                pl.BlockSpec((None, qb, LANES), lambda i, h, j: (i, j, CB_CQ + h))]
    args = [lam_p, proj3]
    if n_cache:
        in_specs += [pl.BlockSpec((None, n_cache, LANES), lambda i, h, j: (i, 0, h)),
                     pl.BlockSpec((None, n_cache, DV), lambda i, h, j: (i, 0, h))]
        args += [cache_k, cache_v]
    in_specs += [pl.BlockSpec((None, t, LANES), lambda i, h, j: (i, 0, CB_CK + h)),
                 pl.BlockSpec((None, t, DV), lambda i, h, j: (i, 0, CB_CV + h)),
                 pl.BlockSpec((None, DV, 1), lambda i, h, j: (l, 0, 0))]
    args += [proj3, proj3, subln_col]
    return pl.pallas_call(
        functools.partial(_diff_kernel, lam_init=lam_init, n_cache=n_cache, n_new=t),
        grid=(b, N_HEAD, t // qb), in_specs=in_specs,
        out_specs=pl.BlockSpec((None, qb, DV), lambda i, h, j: (i, j, h)),
        out_shape=jax.ShapeDtypeStruct((b, t, N_HEAD * DV), BF16),
        scratch_shapes=([pltpu.VMEM((vrows, n_cache), BF16)] if n_cache else [])
        + [pltpu.VMEM((t // kt, vrows, kt), BF16), pltpu.VMEM((2, max(kt, n_cache), qb), F32),
           pltpu.VMEM((2, max(kt, n_cache), qb), F32), pltpu.VMEM((2, vrows, qb), F32)],
        compiler_params=_cparams(("arbitrary", "arbitrary", "arbitrary")),
    )(*args)


def _layer_norm(z, g, b):
    mu = jnp.mean(z, axis=-1, keepdims=True)
    zc = z - mu
    var = jnp.mean(zc * zc, axis=-1, keepdims=True)
    return zc * lax.rsqrt(var + EPS) * g + b


def _merge_kernel(x_ref, ya_ref, yb_ref, yc_ref, m_ref, g1_ref, wb_ref, wo_ref, lg_ref, lb_ref, o_ref):
    half = x_ref.shape[0] // 2
    merged = []
    for p in range(2):
        rs = slice(p * half, (p + 1) * half)
        acc = None
        for i, y_ref in enumerate((ya_ref, yb_ref, yc_ref)):
            gate = _sigmoid(m_ref[rs, i * D_MODEL:(i + 1) * D_MODEL].astype(F32))
            term = gate * _dot(y_ref[rs, :], wb_ref[i])
            acc = term if acc is None else acc + term
        merged.append(acc.astype(BF16))
    for p in range(2):
        rs = slice(p * half, (p + 1) * half)
        out = _dot(merged[p], wo_ref[...])
        z = ALPHA * x_ref[rs, :] + g1_ref[...] * out
        o_ref[rs, :] = _layer_norm(z, lg_ref[...], lb_ref[...])


def _merge_call(x, ya, yb, yc, proj, mod, l, row0, span, wts):
    n = x.shape[0]
    tb = TOKEN_BLOCK
    bw = N_HEAD * DV
    tok = lambda w: pl.BlockSpec((tb, w), lambda i: (i, 0))
    return pl.pallas_call(
        _merge_kernel,
        grid=(n // tb,),
        in_specs=[tok(D_MODEL), tok(bw), tok(bw), tok(bw),
                  pl.BlockSpec((tb, N_GATE_COLS), lambda i: (i, 0)),
                  _mod_spec(l, row0, span, 2),
                  _resident((3, bw, D_MODEL), l), _resident((D_MODEL, D_MODEL), l),
                  _resident((1, D_MODEL), l, 0), _resident((1, D_MODEL), l, 0)],
        out_specs=tok(D_MODEL),
        out_shape=jax.ShapeDtypeStruct((n, D_MODEL), F32),
        compiler_params=_cparams(("arbitrary",)),
    )(x, ya, yb, yc, proj, mod, wts['w_branch'], wts['w_out'], wts['ln_g'], wts['ln_b'])


def _regroup_kernel(w_ref, o_ref):
    for c in range(D_FF // FF_CHUNK):
        lo, hi = c * FF_CHUNK, (c + 1) * FF_CHUNK
        o_ref[:, 2 * lo:2 * lo + FF_CHUNK] = w_ref[:, lo:hi].astype(BF16)
        o_ref[:, 2 * lo + FF_CHUNK:2 * hi] = w_ref[:, D_FF + lo:D_FF + hi].astype(BF16)


def _regroup_w_up(w_up):
    rows = D_MODEL // 4
    return pl.pallas_call(
        _regroup_kernel,
        grid=(DEPTH, D_MODEL // rows),
        in_specs=[pl.BlockSpec((None, rows, 2 * D_FF), lambda l, j: (l, j, 0))],
        out_specs=pl.BlockSpec((None, rows, 2 * D_FF), lambda l, j: (l, j, 0)),
        out_shape=jax.ShapeDtypeStruct((DEPTH, D_MODEL, 2 * D_FF), BF16),
        compiler_params=_cparams(("arbitrary", "arbitrary")),
    )(w_up)


def _gelu_tanh(x):
    k = -2.0 * 0.7978845608028654
    w = x * (x * x * (k * 0.044715) + k)
    return x / (1.0 + jnp.exp(w))


def _ffn_kernel(xp_ref, x_ref, xn_ref, sh_ref, sc_ref, g2_ref, wu_ref, cw_ref, cb_ref, wd_ref,
                lg_ref, lb_ref, o_ref, acc_ref, *, seq_len):
    tb = x_ref.shape[0]
    halo = SUBLANES
    i = pl.program_id(0)
    xm = x_ref[...]
    scale = 1.0 + sc_ref[...]
    shift = sh_ref[...]
    starts_seq = (i * tb) % seq_len == 0
    ends_seq = ((i + 1) * tb) % seq_len == 0
    h_prev = jnp.where(starts_seq, 0.0, xp_ref[...] * scale + shift)
    h_next = jnp.where(ends_seq, 0.0, xn_ref[...] * scale + shift)
    h_mid = xm * scale + shift
    hm = h_mid.astype(BF16)
    h = jnp.concatenate([h_prev, h_mid, h_next], axis=0).astype(BF16)
    rows = tb + 2 * halo
    n_chunk = D_FF // FF_CHUNK

    cut = (rows // 2 + 15) // 16 * 16

    def up(c):
        w = wu_ref[:, 2 * c * FF_CHUNK:2 * (c + 1) * FF_CHUNK]
        u = jnp.concatenate([_dot(h[:cut], w), _dot(h[cut:], w)], axis=0)
        return u[:, :FF_CHUNK], u[halo:halo + tb, FF_CHUNK:]

    ab_next = up(0)
    for c in range(n_chunk):
        lo, hi = c * FF_CHUNK, (c + 1) * FF_CHUNK
        a, b = ab_next
        if c + 1 < n_chunk:
            ab_next = up(c + 1)
        a_prev = pltpu.roll(a, 1, 0)[halo:halo + tb]
        a_mid = a[halo:halo + tb]
        a_next = pltpu.roll(a, rows - 1, 0)[halo:halo + tb]
        w0, w1, w2 = cw_ref[0:1, lo:hi], cw_ref[1:2, lo:hi], cw_ref[2:3, lo:hi]
        cb = cb_ref[:, lo:hi]
        cv = a_prev * w0 + a_mid * w1 + a_next * w2 + cb
        for r in range(seq_len, tb, seq_len):
            sl = slice(r - SUBLANES, r + SUBLANES)
            rid = lax.broadcasted_iota(jnp.int32, (2 * SUBLANES, 1), 0)
            seam = (jnp.where(rid == SUBLANES, 0.0, a_prev[sl]) * w0 + a_mid[sl] * w1
                    + jnp.where(rid == SUBLANES - 1, 0.0, a_next[sl]) * w2 + cb)
            cv = jnp.concatenate([cv[:r - SUBLANES], seam, cv[r + SUBLANES:]], axis=0)
        act = (_gelu_tanh(cv) * b).astype(BF16)
        part = _dot(act, wd_ref[lo:hi, :])
        if c == 0:
            acc_ref[...] = part
        else:
            acc_ref[...] += part
    z = ALPHA * xm + g2_ref[...] * acc_ref[...]
    o_ref[...] = _layer_norm(z, lg_ref[...], lb_ref[...])


def _ffn_call(x, mod, l, row0, span, seq_len, wts):
    n = x.shape[0]
    tb = TOKEN_BLOCK
    hb = tb // SUBLANES
    last = n // SUBLANES - 1
    modspec = functools.partial(_mod_spec, l, row0, span)

    return pl.pallas_call(
        functools.partial(_ffn_kernel, seq_len=seq_len),
        grid=(n // tb,),
        in_specs=[pl.BlockSpec((SUBLANES, D_MODEL), lambda i: (jnp.maximum(i * hb - 1, 0), 0)),
                  pl.BlockSpec((tb, D_MODEL), lambda i: (i, 0)),
                  pl.BlockSpec((SUBLANES, D_MODEL), lambda i: (jnp.minimum((i + 1) * hb, last), 0)),
                  modspec(3), modspec(4), modspec(5),
                  _resident((D_MODEL, 2 * D_FF), l), _resident((3, D_FF), l),
                  _resident((1, D_FF), l), _resident((D_FF, D_MODEL), l),
                  _resident((1, D_MODEL), l, 1), _resident((1, D_MODEL), l, 1)],
        out_specs=pl.BlockSpec((tb, D_MODEL), lambda i: (i, 0)),
        out_shape=jax.ShapeDtypeStruct((n, D_MODEL), F32),
        scratch_shapes=[pltpu.VMEM((tb, D_MODEL), F32)],
        compiler_params=_cparams(("arbitrary",)),
    )(x, x, x, mod, mod, mod, wts['w_up'], wts['conv_w'], wts['conv_b'], wts['w_down'],
      wts['ln_g'], wts['ln_b'])


def _rope_tables(seq):
    half = DK // 4
    t = np.arange(seq)
    inv = ROPE_BASE ** (-np.arange(half, dtype=np.float64) / half)
    ang_row = (t // GRID_W)[:, None] * inv[None, :]
    ang_col = (t % GRID_W)[:, None] * inv[None, :]
    ang = np.concatenate([ang_row, ang_row, ang_col, ang_col], axis=1)
    first = (np.arange(DK) % (2 * half)) < half
    cos = np.cos(ang)
    sa = np.where(first[None, :], -np.sin(ang), 0.0)
    sb = np.where(first[None, :], 0.0, np.sin(ang))
    tile = lambda a: jnp.asarray(np.concatenate([a, a], axis=1), F32)
    return tile(cos), tile(sa), tile(sb)


def _trunk(x, mod, row0, seq_len, batch, l, wts, tables, ctx, carried):
    n = x.shape[0]
    span = n if ctx is None else seq_len
    kv_prev = None if carried is None else carried[:2]
    outs = _proj_call(x, mod, l, row0, span, seq_len, wts, tables, ctx is None, kv_prev)
    proj, la = outs[0], outs[1]
    proj3 = proj.reshape(batch, seq_len, D_IN)
    la3 = la.reshape(batch, seq_len, 4 * LANES)
    if ctx is None:
        ret_prev, gla_prev = carried[2:]
        ya, st_ret = _ret_call(proj3, wts['ret_decay'], l, None, ret_prev)
        yb, st_gla = _gla_call(proj3, la3, wts['gla_norm_g'], l, None, gla_prev)
        yc = _diff_call(proj3, None, None, wts['diff_lam'], wts['diff_subln_g'], l)
        carried = (outs[2], outs[3], st_ret, st_gla)
    else:
        ya, = _ret_call(proj3, wts['ret_decay'], l, ctx['ret'], None)
        yb, = _gla_call(proj3, la3, wts['gla_norm_g'], l, ctx['gla'], None)
        yc = _diff_call(proj3, ctx['dk'][l], ctx['dv'][l], wts['diff_lam'], wts['diff_subln_g'], l)
    bw = N_HEAD * DV
    x1 = _merge_call(x, ya.reshape(n, bw), yb.reshape(n, bw), yc.reshape(n, bw), proj, mod, l,
                     row0, span, wts)
    x2 = _ffn_call(x1, mod, l, row0, span, seq_len, wts)
    return x2, carried


def kernel(x_prompt, x_sample, cache_diff_k, cache_diff_v, state_ret, state_gla, c, c_ctx,
           ada_w, ada_b, w_in, ret_decay, gla_wa1, gla_wa2, gla_ba, gla_norm_g, diff_lam,
           diff_subln_g, w_branch, w_out, ln_g, ln_b, ffn_w_up, ffn_conv_w, ffn_conv_b, ffn_w_down):
    bp, tp, _ = x_prompt.shape
    bs, ts, _ = x_sample.shape
    past = cache_diff_k.shape[3]

    cvec = jnp.concatenate([c_ctx[None, :], c, jnp.zeros((SUBLANES - 1 - bs, D_MODEL), F32)], axis=0)
    mod = _ada_call(cvec, ada_w, ada_b).reshape(DEPTH, SUBLANES, 6, 1, D_MODEL)

    wa1 = jnp.concatenate([gla_wa1[:, 0], gla_wa1[:, 1]], axis=-1)
    wa1 = jnp.pad(wa1, ((0, 0), (0, 0), (0, LANES - 2 * GLA_RANK))).astype(BF16)
    wa2 = jnp.zeros((DEPTH, LANES, 4 * LANES), F32)
    wa2 = wa2.at[:, 0:GLA_RANK, 0:2 * LANES].set(gla_wa2[:, 0])
    wa2 = wa2.at[:, GLA_RANK:2 * GLA_RANK, 2 * LANES:].set(gla_wa2[:, 1]).astype(BF16)
    wts = {
        'w_in': w_in.astype(BF16), 'wa1': wa1, 'wa2': wa2,
        'ba': gla_ba.reshape(DEPTH, 1, 4 * LANES), 'ret_decay': ret_decay,
        'gla_norm_g': gla_norm_g.reshape(DEPTH, 1, DV), 'diff_lam': diff_lam,
        'diff_subln_g': diff_subln_g.reshape(DEPTH, DV, 1),
        'w_branch': w_branch.astype(BF16), 'w_out': w_out.astype(BF16),
        'ln_g': ln_g.reshape(DEPTH, 2, 1, D_MODEL), 'ln_b': ln_b.reshape(DEPTH, 2, 1, D_MODEL),
        'w_up': _regroup_w_up(ffn_w_up), 'conv_w': ffn_conv_w,
        'conv_b': ffn_conv_b.reshape(DEPTH, 1, D_FF), 'w_down': ffn_w_down.astype(BF16),
    }

    h = x_prompt.reshape(bp * tp, D_MODEL)
    carried = (jnp.zeros((bp, DEPTH, 2 * N_HEAD, tp, DK), F32), jnp.zeros((bp, DEPTH, N_HEAD, tp, DV), F32),
               jnp.zeros((bp, DEPTH, 2, N_HEAD, DK, DV), F32), jnp.zeros((bp, DEPTH, 2, N_HEAD, DK, DV), F32))
    for l in range(DEPTH):
        h, carried = _trunk(h, mod, 0, tp, bp, l, wts, None, None, carried)
    y_prompt = h.reshape(bp, tp, D_MODEL)

    tables = _rope_tables(ts)
    ctx = {
        'dk': cache_diff_k.transpose(1, 0, 3, 2, 4).reshape(DEPTH, bs, past, 2 * N_HEAD * DK).astype(BF16),
        'dv': cache_diff_v.transpose(1, 0, 3, 2, 4).reshape(DEPTH, bs, past, N_HEAD * DV).astype(BF16),
        'ret': state_ret, 'gla': state_gla,
    }
    z = x_sample.reshape(bs * ts, D_MODEL)
    for l in range(DEPTH):
        z, _ = _trunk(z, mod, 1, ts, bs, l, wts, tables, ctx, None)
    y_sample = z.reshape(bs, ts, D_MODEL)

    return (y_prompt, y_sample) + tuple(carried)
```

```python
import functools
import math

import numpy as np
import jax
import jax.numpy as jnp
from jax import lax
from jax.experimental import pallas as pl
from jax.experimental.pallas import tpu as pltpu

F32 = jnp.float32
BF16 = jnp.bfloat16

D_MODEL = 1024
DEPTH = 2
GRID_W = 64
N_HEAD = 4
DK = 64
DV = 128
GLA_RANK = 16
GLA_TAU = 16.0
D_FF = 2816
ROPE_BASE = 10000.0
ALPHA = (2 * DEPTH) ** 0.25
EPS = 1e-5
D_IN = 7680

LANES = 128
SUBLANES = 8
VMEM_LIMIT = 56 * 1024 * 1024

N_GATE_COLS = 3 * D_MODEL
_QK_BLOCKS = N_HEAD * DK // LANES
_V_BLOCKS = N_HEAD * DV // LANES
_segment_blocks = (_QK_BLOCKS, _QK_BLOCKS, _V_BLOCKS, _V_BLOCKS,
                   _QK_BLOCKS, _QK_BLOCKS, _V_BLOCKS, _V_BLOCKS,
                   2 * _QK_BLOCKS, 2 * _QK_BLOCKS, _V_BLOCKS)
(CB_AQ, CB_AK, CB_AV, CB_AG, CB_BQ, CB_BK, CB_BV, CB_BR, CB_CQ, CB_CK, CB_CV) = (
    N_GATE_COLS // LANES + sum(_segment_blocks[:i]) for i in range(len(_segment_blocks)))
ROPE_BLOCKS = (tuple(range(CB_AQ, CB_AK + _QK_BLOCKS))
               + tuple(range(CB_CQ, CB_CK + 2 * _QK_BLOCKS)))
SCALED_BLOCKS = (tuple(range(CB_AK, CB_AK + _QK_BLOCKS))
                 + tuple(range(CB_BQ, CB_BQ + _QK_BLOCKS)))
SOFTMAX_Q_BLOCKS = tuple(range(CB_CQ, CB_CQ + 2 * _QK_BLOCKS))
QK_SCALE = DK ** -0.5
LOG2E = math.log2(math.e)

TOKEN_BLOCK = 512
RET_CHUNK = 256
GLA_BLOCK = 128
Q_BLOCK = 1024
DIFF_KEY_TILE = 512
DIFF_ONES_ROWS = 16
FF_CHUNK = 256


def _cparams(sem):
    return pltpu.CompilerParams(dimension_semantics=sem, vmem_limit_bytes=VMEM_LIMIT)


def _resident(shape, *lead):
    idx = tuple(lead) + (0,) * len(shape)
    return pl.BlockSpec((None,) * len(lead) + tuple(shape), lambda *_: idx,
                        pipeline_mode=pl.Buffered(1))


def _mod_spec(l, row0, span, k):
    return pl.BlockSpec((None, None, None, 1, D_MODEL),
                        lambda i: (l, row0 + (i * TOKEN_BLOCK) // span, k, 0, 0))


def _sigmoid(x):
    return 1.0 / (1.0 + jnp.exp(-x))


def _log_sigmoid(x):
    return jnp.minimum(x, 0.0) - jnp.log(1.0 + jnp.exp(-jnp.abs(x)))


def _dot(a, b):
    return jnp.dot(a, b, preferred_element_type=F32)


def _dot_nt(a, b):
    return lax.dot_general(a, b, (((1,), (1,)), ((), ())), preferred_element_type=F32)


def _dot_tn(a, b):
    return lax.dot_general(a, b, (((0,), (0,)), ((), ())), preferred_element_type=F32)


def _ada_kernel(c_ref, w_ref, b_ref, o_ref):
    c = c_ref[...]
    s = (c * _sigmoid(c)).astype(BF16)
    o_ref[...] = _dot(s, w_ref[...].astype(BF16)) + b_ref[...]


def _ada_call(cvec, ada_w, ada_b):
    nt = 1536
    return pl.pallas_call(
        _ada_kernel,
        grid=(DEPTH, 6 * D_MODEL // nt),
        in_specs=[pl.BlockSpec((SUBLANES, D_MODEL), lambda l, j: (0, 0)),
                  pl.BlockSpec((None, D_MODEL, nt), lambda l, j: (l, 0, j)),
                  pl.BlockSpec((None, 1, nt), lambda l, j: (l, 0, j))],
        out_specs=pl.BlockSpec((None, SUBLANES, nt), lambda l, j: (l, 0, j)),
        out_shape=jax.ShapeDtypeStruct((DEPTH, SUBLANES, 6 * D_MODEL), F32),
        compiler_params=_cparams(("arbitrary", "arbitrary")),
    )(cvec, ada_w, ada_b.reshape(DEPTH, 1, 6 * D_MODEL))


def _proj_kernel(*refs, rope, emit_kv, n_alias, seq_len):
    x_ref, sh_ref, sc_ref, w_ref, wa1_ref, wa2_ref, ba_ref = refs[:7]
    pos = 7
    if rope:
        cos_ref, sa_ref, sb_ref = refs[pos:pos + 3]
        pos += 3
    pos += n_alias
    proj_ref, la_ref = refs[pos:pos + 2]
    pos += 2
    if emit_kv:
        ck_ref, cv_ref = refs[pos:pos + 2]

    h = (x_ref[...] * (1.0 + sc_ref[...]) + sh_ref[...]).astype(BF16)
    tile = 512
    per = tile // LANES
    n_blk = D_IN // LANES
    r = _dot(h, wa1_ref[...]).astype(BF16)
    for j in range(D_IN // tile):
        acc = _dot(h, w_ref[:, j * tile:(j + 1) * tile])
        if j == 0:
            z = _dot(r, wa2_ref[...]) + ba_ref[...]
        if j == 1:
            la_ref[...] = _log_sigmoid(z) * (1.0 / GLA_TAU)
        for i in range(per):
            blk = (j * per + i + N_GATE_COLS // LANES) % n_blk
            y = acc[:, i * LANES:(i + 1) * LANES]
            if emit_kv and CB_CK <= blk < CB_CK + 4:
                for s in range(x_ref.shape[0] // seq_len):
                    for half in range(2):
                        ck_ref[s, 2 * (blk - CB_CK) + half] = (
                            y[s * seq_len:(s + 1) * seq_len, half * DK:(half + 1) * DK])
            if emit_kv and CB_CV <= blk < CB_CV + 4:
                for s in range(x_ref.shape[0] // seq_len):
                    cv_ref[s, blk - CB_CV] = y[s * seq_len:(s + 1) * seq_len, :]
            if rope and blk in ROPE_BLOCKS:
                y = (y * cos_ref[...] + pltpu.roll(y, LANES - 16, 1) * sa_ref[...]
                     + pltpu.roll(y, 16, 1) * sb_ref[...])
            if blk in SCALED_BLOCKS:
                y = y * QK_SCALE
            if blk in SOFTMAX_Q_BLOCKS:
                y = y * (QK_SCALE * LOG2E)
            proj_ref[:, blk * LANES:(blk + 1) * LANES] = y.astype(BF16)


def _proj_call(x, mod, l, row0, span, seq_len, wts, tables, emit_kv, kv_prev):
    n = x.shape[0]
    tb = TOKEN_BLOCK
    assert n % tb == 0 and (seq_len % tb == 0 or tb % seq_len == 0)
    bps = max(seq_len // tb, 1)
    rope = tables is not None

    in_specs = [pl.BlockSpec((tb, D_MODEL), lambda i: (i, 0)),
                _mod_spec(l, row0, span, 0), _mod_spec(l, row0, span, 1),
                _resident((D_MODEL, D_IN), l), _resident((D_MODEL, LANES), l),
                _resident((LANES, 4 * LANES), l), _resident((1, 4 * LANES), l)]
    args = [x, mod, mod, wts['w_in'], wts['wa1'], wts['wa2'], wts['ba']]
    if rope:
        in_specs += [pl.BlockSpec((tb, LANES), lambda i: (i % bps, 0))] * 3
        args += list(tables)
    aliases = {}
    if kv_prev is not None:
        aliases = {len(args): 2, len(args) + 1: 3}
        in_specs += [pl.BlockSpec(memory_space=pl.ANY)] * 2
        args += list(kv_prev)
    out_specs = [pl.BlockSpec((tb, D_IN), lambda i: (i, 0)),
                 pl.BlockSpec((tb, 4 * LANES), lambda i: (i, 0))]
    out_shape = [jax.ShapeDtypeStruct((n, D_IN), BF16), jax.ShapeDtypeStruct((n, 4 * LANES), F32)]
    if emit_kv:
        spb = tb // seq_len
        nseq = n // seq_len
        out_specs += [pl.BlockSpec((spb, None, 2 * N_HEAD, seq_len, DK), lambda i: (i, l, 0, 0, 0)),
                      pl.BlockSpec((spb, None, N_HEAD, seq_len, DV), lambda i: (i, l, 0, 0, 0))]
        out_shape += [jax.ShapeDtypeStruct((nseq, DEPTH, 2 * N_HEAD, seq_len, DK), F32),
                      jax.ShapeDtypeStruct((nseq, DEPTH, N_HEAD, seq_len, DV), F32)]
    return pl.pallas_call(
        functools.partial(_proj_kernel, rope=rope, emit_kv=emit_kv, n_alias=len(aliases),
                          seq_len=seq_len),
        grid=(n // tb,), in_specs=in_specs, out_specs=out_specs, out_shape=out_shape,
        input_output_aliases=aliases,
        compiler_params=_cparams(("arbitrary",)),
    )(*args)


def _ret_kernel(*refs, layer, seq, chunk, has_ctx, n_alias):
    dec_ref, q_ref, k_ref, v_ref, g_ref = refs[:5]
    pos = 5
    if has_ctx:
        s0_ref = refs[pos]
        pos += 1
    pos += n_alias
    y_ref = refs[pos]
    pos += 1
    if not has_ctx:
        st_ref = refs[pos]
        pos += 1
    of_ref, ob_ref, s_ref, dm_ref, eq_ref, ek_ref, gc_ref, vt_ref = refs[pos:]
    p = pl.program_id(1)
    nc = seq // chunk
    c_f = float(chunk)
    lane = lax.broadcasted_iota(jnp.int32, (1, LANES), 1)
    hmask = [lane < DK, lane >= DK]
    rowp = lax.broadcasted_iota(jnp.int32, (chunk, 1), 0).astype(F32)
    colp = lax.broadcasted_iota(jnp.int32, (1, chunk), 1).astype(F32)
    diff = rowp - colp

    @pl.when((pl.program_id(0) == 0) & (p == 0))
    def _():
        for pair in range(2):
            for d in range(2):
                for hh in range(2):
                    ti = 4 * pair + 2 * d + hh
                    raw = dec_ref[layer, d, 2 * pair + hh]
                    lg_l = _log_sigmoid(jnp.full((1, LANES), raw, F32))
                    lg_c = _log_sigmoid(jnp.full((1, chunk), raw, F32))
                    if d == 0:
                        dm_ref[ti] = jnp.where(diff <= 0, jnp.exp(jnp.maximum(-diff, 0.0) * lg_c), 0.0)
                        eq_ref[ti] = jnp.exp((rowp + 1.0) * lg_l)
                        ek_ref[ti] = jnp.exp((c_f - 1.0 - rowp) * lg_l)
                    else:
                        dm_ref[ti] = jnp.where(diff >= 0, jnp.exp(jnp.maximum(diff, 0.0) * lg_c), 0.0)
                        eq_ref[ti] = jnp.exp((c_f - rowp) * lg_l)
                        ek_ref[ti] = jnp.exp(rowp * lg_l)
                    gc_ref[ti] = jnp.exp(c_f * lg_l)

    for d in range(2):
        for hh in range(2):
            ci = 2 * d + hh
            if has_ctx:
                s0 = s0_ref[d, hh]
                zero = jnp.zeros((DK, DV), F32)
                s_ref[ci] = jnp.concatenate([s0, zero] if hh == 0 else [zero, s0], axis=0).T
            else:
                s_ref[ci] = jnp.zeros((DV, LANES), F32)

    for hh in range(2):
        for c in range(nc):
            vt_ref[hh, c] = v_ref[c * chunk:(c + 1) * chunk, hh * DV:(hh + 1) * DV].T

    def step(n, carry):
        pending = []
        for d in range(2):
            c = n if d == 0 else nc - 1 - n
            r0 = pl.multiple_of(c * chunk, chunk)
            qc = q_ref[pl.ds(r0, chunk), :]
            kc = k_ref[pl.ds(r0, chunk), :]
            for hh in range(2):
                ci = 2 * d + hh
                qh = jnp.where(hmask[hh], qc, jnp.zeros_like(qc))
                kh = jnp.where(hmask[hh], kc, jnp.zeros_like(kc))
                vt = vt_ref[hh, c]
                sc_t = _dot_nt(kh, qh)
                st = s_ref[ci]
                ti = 4 * p + ci
                qd = (qh.astype(F32) * eq_ref[ti]).astype(BF16)
                inter_t = _dot_nt(st.astype(BF16), qd)
                kd = (kh.astype(F32) * ek_ref[ti]).astype(BF16)
                s_ref[ci] = gc_ref[ti] * st + _dot(vt, kd)
                pending.append((d, hh, c, sc_t, inter_t, vt))
        for d, hh, c, sc_t, inter_t, vt in pending:
            o_ref = of_ref if d == 0 else ob_ref
            intra_t = _dot(vt, (sc_t * dm_ref[4 * p + 2 * d + hh]).astype(BF16))
            o_ref[hh, c] = intra_t + inter_t
        return carry

    lax.fori_loop(0, nc, step, 0, unroll=2 if nc % 2 == 0 else 1)

    if not has_ctx:
        for d in range(2):
            for hh in range(2):
                st_ref[d, hh] = s_ref[2 * d + hh][...].T[hh * DK:(hh + 1) * DK, :]

    def finish(n, carry):
        r0 = pl.multiple_of(n * chunk, chunk)
        for hh in range(2):
            o = of_ref[hh, n] + ob_ref[hh, n]
            mu = jnp.mean(o, axis=0, keepdims=True)
            oc = o - mu
            var = jnp.mean(oc * oc, axis=0, keepdims=True)
            g = g_ref[pl.ds(r0, chunk), hh * DV:(hh + 1) * DV].astype(F32)
            y = (oc * lax.rsqrt(var + EPS)).T * (g * _sigmoid(g))
            y_ref[pl.ds(r0, chunk), hh * DV:(hh + 1) * DV] = y.astype(BF16)
        return carry

    lax.fori_loop(0, nc, finish, 0)


def _state_io(l, batch, s0_all, st_prev, n_in, pairs=1):
    spec = pl.BlockSpec((None, None, 2, 2 * pairs, DK, DV), lambda i, p: (i, l, 0, p, 0, 0))
    if s0_all is not None:
        return [spec], [s0_all], [], [], {}
    shape = jax.ShapeDtypeStruct((batch, DEPTH, 2, N_HEAD, DK, DV), F32)
    return [pl.BlockSpec(memory_space=pl.ANY)], [st_prev], [spec], [shape], {n_in: 1}


def _ret_call(proj3, decay, l, s0_all, st_prev):
    b, t, _ = proj3.shape
    chunk = min(RET_CHUNK, t)
    assert t % chunk == 0
    has_ctx = s0_all is not None
    in_specs = [pl.BlockSpec(memory_space=pltpu.SMEM),
                pl.BlockSpec((None, t, LANES), lambda i, p: (i, 0, CB_AQ + p)),
                pl.BlockSpec((None, t, LANES), lambda i, p: (i, 0, CB_AK + p)),
                pl.BlockSpec((None, t, 2 * DV), lambda i, p: (i, 0, CB_AV // 2 + p)),
                pl.BlockSpec((None, t, 2 * DV), lambda i, p: (i, 0, CB_AG // 2 + p))]
    args = [decay, proj3, proj3, proj3, proj3]
    st_in, st_args, st_out, st_shape, aliases = _state_io(l, b, s0_all, st_prev, len(args))
    return pl.pallas_call(
        functools.partial(_ret_kernel, layer=l, seq=t, chunk=chunk, has_ctx=has_ctx,
                          n_alias=len(aliases)),
        grid=(b, 2), in_specs=in_specs + st_in,
        out_specs=[pl.BlockSpec((None, t, 2 * DV), lambda i, p: (i, 0, p))] + st_out,
        out_shape=[jax.ShapeDtypeStruct((b, t, N_HEAD * DV), BF16)] + st_shape,
        input_output_aliases=aliases,
        scratch_shapes=[pltpu.VMEM((2, t // chunk, DV, chunk), F32),
                        pltpu.VMEM((2, t // chunk, DV, chunk), F32),
                        pltpu.VMEM((4, DV, LANES), F32), pltpu.VMEM((8, chunk, chunk), F32),
                        pltpu.VMEM((8, chunk, LANES), F32), pltpu.VMEM((8, chunk, LANES), F32),
                        pltpu.VMEM((8, 1, LANES), F32), pltpu.VMEM((2, t // chunk, DV, chunk), BF16)],
        compiler_params=_cparams(("arbitrary", "arbitrary")),
    )(*(args + st_args))


def _gla_levels(blk, reverse):
    t = lax.broadcasted_iota(jnp.int32, (blk, blk), 0)
    s = lax.broadcasted_iota(jnp.int32, (blk, blk), 1)
    xr = t ^ s
    lvl = jnp.zeros((blk, blk), jnp.int32)
    h = 2
    while h < blk:
        lvl = lvl + jnp.where(xr >= h, 1, 0)
        h *= 2
    allowed = (s > t) if reverse else (s < t)
    return jnp.where(t == s, -1, jnp.where(allowed, lvl, -2))


def _gla_sweep(q, k, la, hmask, lvl, blk, reverse):
    row = lax.broadcasted_iota(jnp.int32, (blk, 1), 0)
    w = la
    tot = la
    def both_heads(qx):
        return jnp.concatenate([jnp.where(m, qx, jnp.zeros_like(qx)) for m in hmask], axis=0)

    lvl2 = jnp.concatenate([lvl, lvl], axis=0)
    amat = jnp.where(lvl2 == -1, _dot_nt(both_heads(q.astype(BF16)), k.astype(BF16)), 0.0)
    h = 1
    li = 0
    while h < blk:
        up = (row & h) != 0
        qside = jnp.logical_not(up) if reverse else up
        f = jnp.exp(jnp.where(qside, w, tot - w))
        ql = (q * f).astype(BF16)
        kl = (k * f).astype(BF16)
        amat = jnp.where(lvl2 == li, _dot_nt(both_heads(ql), kl), amat)
        partner = jnp.where(up, pltpu.roll(tot, h, 0), pltpu.roll(tot, blk - h, 0))
        w = w + jnp.where(qside, partner, 0.0)
        tot = tot + partner
        h *= 2
        li += 1
    return amat, w, tot


def _gla_state_dots(q, k, w, tot, v_pair, st_list, hmask):
    qd = (q * jnp.exp(w)).astype(BF16)
    kd = (k * jnp.exp(tot - w)).astype(BF16)
    dec = jnp.exp(tot[0:1, :])
    inter, new_st = [], []
    for hh in range(2):
        vh = v_pair[:, hh * DV:(hh + 1) * DV]
        st = st_list[hh]
        qh = jnp.where(hmask[hh], qd, jnp.zeros_like(qd))
        kh = jnp.where(hmask[hh], kd, jnp.zeros_like(kd))
        inter.append(_dot_nt(qh, st.astype(BF16)))
        new_st.append(dec * st + _dot_tn(vh, kh))
    return inter, new_st


def _gla_kernel(*refs, seq, blk, has_ctx, n_alias):
    q_ref, k_ref, v_ref, r_ref, laf_ref, lab_ref, ng_ref = refs[:7]
    pos = 7
    if has_ctx:
        s0_ref = refs[pos]
        pos += 1
    pos += n_alias
    y_ref = refs[pos]
    pos += 1
    if not has_ctx:
        st_ref = refs[pos]
        pos += 1
    of_ref, ob_ref, s_ref, lv_ref = refs[pos:]
    nb = seq // blk
    n_pair = q_ref.shape[1] // LANES
    heads = 2 * n_pair
    lane = lax.broadcasted_iota(jnp.int32, (1, LANES), 1)
    hmask = [lane < DK, lane >= DK]

    @pl.when((pl.program_id(0) == 0) & (pl.program_id(1) == 0))
    def _():
        for d in range(2):
            lv_ref[d] = _gla_levels(blk, d == 1)

    for d in range(2):
        for head in range(heads):
            hh = head % 2
            if has_ctx:
                zero = jnp.zeros((DK, DV), F32)
                s0 = s0_ref[d, head]
                full = jnp.concatenate([s0, zero] if hh == 0 else [zero, s0], axis=0)
                s_ref[heads * d + head] = full.T
            else:
                s_ref[heads * d + head] = jnp.zeros((DV, LANES), F32)

    def step(n, carry):
        work = []
        for pair in range(n_pair):
            pc = slice(pair * LANES, (pair + 1) * LANES)
            vc = slice(pair * 2 * DV, (pair + 1) * 2 * DV)
            for d in range(2):
                c = n if d == 0 else nb - 1 - n
                r0 = pl.multiple_of(c * blk, blk)
                q = q_ref[pl.ds(r0, blk), pc].astype(F32)
                k = k_ref[pl.ds(r0, blk), pc].astype(F32)
                la_ref = laf_ref if d == 0 else lab_ref
                amat, w, tot = _gla_sweep(q, k, la_ref[pl.ds(r0, blk), pc], hmask, lv_ref[d], blk, d == 1)
                work.append((pair, d, r0, q, k, amat, w, tot, v_ref[pl.ds(r0, blk), vc]))
        inters = []
        for pair, d, r0, q, k, amat, w, tot, vp in work:
            base = heads * d + 2 * pair
            inter, new_st = _gla_state_dots(q, k, w, tot, vp, [s_ref[base], s_ref[base + 1]], hmask)
            inters.append(inter)
            for hh in range(2):
                s_ref[base + hh] = new_st[hh]
        for (pair, d, r0, q, k, amat, w, tot, vp), inter in zip(work, inters):
            o_ref = of_ref if d == 0 else ob_ref
            for hh in range(2):
                intra = _dot(amat[hh * blk:(hh + 1) * blk].astype(BF16), vp[:, hh * DV:(hh + 1) * DV])
                o_ref[2 * pair + hh, pl.ds(r0, blk), :] = intra + inter[hh]
        return carry

    lax.fori_loop(0, nb, step, 0)

    if not has_ctx:
        for d in range(2):
            for head in range(heads):
                hh = head % 2
                st_ref[d, head] = s_ref[heads * d + head][...].T[hh * DK:(hh + 1) * DK, :]

    fin = min(seq, 512)

    def finish(n, carry):
        r0 = pl.multiple_of(n * fin, fin)
        for head in range(heads):
            o = of_ref[head, pl.ds(r0, fin), :] + ob_ref[head, pl.ds(r0, fin), :]
            ms = jnp.mean(o * o, axis=-1, keepdims=True)
            g = r_ref[pl.ds(r0, fin), head * DV:(head + 1) * DV].astype(F32)
            y = o * lax.rsqrt(ms + EPS) * ng_ref[...] * (g * _sigmoid(g))
            y_ref[pl.ds(r0, fin), head * DV:(head + 1) * DV] = y.astype(BF16)
        return carry

    lax.fori_loop(0, seq // fin, finish, 0)


def _gla_call(proj3, la3, norm_g, l, s0_all, st_prev):
    b, t, _ = proj3.shape
    blk = GLA_BLOCK
    assert t % blk == 0
    has_ctx = s0_all is not None
    pairs = 2 if 2 * 8192 * t <= VMEM_LIMIT // 2 else 1
    qk_w = pairs * LANES
    v_w = pairs * 2 * DV
    col = lambda w, cb: pl.BlockSpec((None, t, w), lambda i, p: (i, 0, cb * LANES // w + p))
    in_specs = [col(qk_w, CB_BQ), col(qk_w, CB_BK), col(v_w, CB_BV), col(v_w, CB_BR),
                pl.BlockSpec((None, t, qk_w), lambda i, p: (i, 0, p)),
                pl.BlockSpec((None, t, qk_w), lambda i, p: (i, 0, 2 // pairs + p)),
                pl.BlockSpec((None, 1, DV), lambda i, p: (l, 0, 0))]
    args = [proj3, proj3, proj3, proj3, la3, la3, norm_g]
    st_in, st_args, st_out, st_shape, aliases = _state_io(l, b, s0_all, st_prev, len(args), pairs)
    return pl.pallas_call(
        functools.partial(_gla_kernel, seq=t, blk=blk, has_ctx=has_ctx, n_alias=len(aliases)),
        grid=(b, 2 // pairs), in_specs=in_specs + st_in,
        out_specs=[pl.BlockSpec((None, t, v_w), lambda i, p: (i, 0, p))] + st_out,
        out_shape=[jax.ShapeDtypeStruct((b, t, N_HEAD * DV), BF16)] + st_shape,
        input_output_aliases=aliases,
        scratch_shapes=[pltpu.VMEM((2 * pairs, t, DV), F32), pltpu.VMEM((2 * pairs, t, DV), F32),
                        pltpu.VMEM((4 * pairs, DV, LANES), F32), pltpu.VMEM((2, blk, blk), jnp.int32)],
        compiler_params=_cparams(("arbitrary", "arbitrary")),
    )(*(args + st_args))


def _diff_kernel(*refs, lam_init, n_cache, n_new):
    lam_ref, q_ref = refs[:2]
    pos = 2
    if n_cache:
        kc_ref, vc_ref = refs[pos:pos + 2]
        pos += 2
    kn_ref, vn_ref, sg_ref, y_ref = refs[pos:pos + 4]
    pos += 4
    if n_cache:
        vtc_ref = refs[pos]
        pos += 1
    vtn_ref, sa_ref, sb_ref, acc_ref = refs[pos:]
    qb = q_ref.shape[0]
    kt = min(n_new, DIFF_KEY_TILE)
    n_tiles = n_new // kt

    @pl.when(pl.program_id(2) == 0)
    def _():
        if n_cache:
            vtc_ref[0:DV, :] = vc_ref[...].T
            vtc_ref[DV:, :] = jnp.ones((DIFF_ONES_ROWS, n_cache), BF16)
        for r in range(n_tiles):
            vtn_ref[r, 0:DV, :] = vn_ref[r * kt:(r + 1) * kt, :].T
            vtn_ref[r, DV:, :] = jnp.ones((DIFF_ONES_ROWS, kt), BF16)

    lam = _diff_lambda(lam_ref, lam_init)
    lane = lax.broadcasted_iota(jnp.int32, (1, LANES), 1)
    q = q_ref[...]
    qs =[jnp.where((lane < DK) if i == 0 else (lane >= DK), q, jnp.zeros_like(q)) for i in range(2)]

    s_bufs = (sa_ref, sb_ref)
    new0 = 1 if n_cache else 0
    last = new0 + n_tiles - 1

    def tile_rows(idx):
        return n_cache if idx < new0 else kt

    def scores(idx, r=None):
        if idx < new0:
            k_tile = kc_ref[...]
        elif r is None:
            k_tile = kn_ref[(idx - new0) * kt:(idx - new0 + 1) * kt, :]
        else:
            k_tile = kn_ref[pl.ds(pl.multiple_of(r * kt, kt), kt), :]
        for i in range(2):
            s_bufs[idx % 2][i, 0:tile_rows(idx), :] = _dot_nt(k_tile, qs[i])

    def soft(idx, m):
        n = tile_rows(idx)
        m_out, es, alphas = [], [], []
        for i in range(2):
            s = s_bufs[idx % 2][i, 0:n, :]
            m_new = jnp.maximum(m[i], jnp.max(s, axis=0, keepdims=True))
            es.append(jnp.exp2(s - m_new).astype(BF16))
            alphas.append(jnp.exp2(m[i] - m_new))
            m_out.append(m_new)
        return m_out, (es, alphas)

    def pv(idx, soft_out, r=None):
        es, alphas = soft_out
        if idx < new0:
            vt_tile = vtc_ref[...]
        else:
            vt_tile = vtn_ref[idx - new0 if r is None else r]
        for i in range(2):
            part = _dot(vt_tile, es[i])
            acc_ref[i] = part if idx == 0 else acc_ref[i] * alphas[i] + part

    m = [jnp.full((1, qb), -1e30, F32) for _ in range(2)]
    scores(0)
    if last >= 1:
        scores(1)
    m, so = soft(0, m)
    pv(0, so)
    n_pairs = max(last - 1, 0) // 2
    loop_end = 1 + 2 * n_pairs

    def pair(p, carry):
        m = list(carry)
        r = 2 * p + (1 - new0)
        scores(2, r + 1)
        m, so1 = soft(1, m)
        scores(3, r + 2)
        pv(1, so1, r)
        m, so2 = soft(2, m)
        pv(2, so2, r + 1)
        return tuple(m)

    if n_pairs:
        m = list(lax.fori_loop(0, n_pairs, pair, tuple(m)))
    for idx in range(loop_end, last + 1):
        if idx + 1 <= last:
            scores(idx + 1)
        m, so = soft(idx, m)
        pv(idx, so)

    acc0 = acc_ref[0]
    acc1 = acc_ref[1]
    o = (acc0[0:DV] * (1.0 / acc0[DV:DV + 1]) - acc1[0:DV] * (lam / acc1[DV:DV + 1]))
    ms = jnp.mean(o * o, axis=0, keepdims=True)
    y = o * lax.rsqrt(ms + EPS) * (sg_ref[...] * (1.0 - lam_init))
    y_ref[...] = y.T.astype(BF16)


def _diff_lambda(lam_ref, lam_init):
    lp = lam_ref[...]
    return (jnp.exp(jnp.sum(lp[0:1] * lp[1:2], axis=-1, keepdims=True))
            - jnp.exp(jnp.sum(lp[2:3] * lp[3:4], axis=-1, keepdims=True)) + lam_init)


def _diff_short_kernel(lam_ref, q_ref, k_ref, v_ref, sg_ref, y_ref, *, lam_init):
    t = q_ref.shape[0]
    lam = _diff_lambda(lam_ref, lam_init)
    lane = lax.broadcasted_iota(jnp.int32, (1, LANES), 1)
    ones = jnp.ones((DIFF_ONES_ROWS, t), BF16)
    cols = [slice(h * LANES, (h + 1) * LANES) for h in range(N_HEAD)]
    scores = []
    for h in range(N_HEAD):
        q = q_ref[:, cols[h]]
        k = k_ref[:, cols[h]]
        for i in range(2):
            qi = jnp.where((lane < DK) if i == 0 else (lane >= DK), q, jnp.zeros_like(q))
            scores.append(_dot_nt(k, qi))
    for h in range(N_HEAD):
        vt = jnp.concatenate([v_ref[:, cols[h]].T, ones], axis=0)
        acc = []
        for i in range(2):
            s = scores[2 * h + i]
            e = jnp.exp2(s - jnp.max(s, axis=0, keepdims=True)).astype(BF16)
            acc.append(_dot(vt, e))
        o = (acc[0][0:DV] * (1.0 / acc[0][DV:DV + 1]) - acc[1][0:DV] * (lam / acc[1][DV:DV + 1]))
        ms = jnp.mean(o * o, axis=0, keepdims=True)
        y = o * lax.rsqrt(ms + EPS) * (sg_ref[...] * (1.0 - lam_init))
        y_ref[:, cols[h]] = y.T.astype(BF16)


def _diff_short_call(proj3, lam_p, subln_col, l):
    b, t, _ = proj3.shape
    lam_init = 0.8 - 0.6 * math.exp(-0.3 * l)
    width = N_HEAD * LANES
    col = lambda cb: pl.BlockSpec((None, t, width), lambda i: (i, 0, cb * LANES // width))
    return pl.pallas_call(
        functools.partial(_diff_short_kernel, lam_init=lam_init),
        grid=(b,),
        in_specs=[pl.BlockSpec((None, 4, DK), lambda i: (l, 0, 0)), col(CB_CQ), col(CB_CK), col(CB_CV),
                  pl.BlockSpec((None, DV, 1), lambda i: (l, 0, 0))],
        out_specs=pl.BlockSpec((None, t, N_HEAD * DV), lambda i: (i, 0, 0)),
        out_shape=jax.ShapeDtypeStruct((b, t, N_HEAD * DV), BF16),
        compiler_params=_cparams(("arbitrary",)),
    )(lam_p, proj3, proj3, proj3, subln_col)


def _diff_call(proj3, cache_k, cache_v, lam_p, subln_col, l):
    b, t, _ = proj3.shape
    if cache_k is None and t <= DIFF_KEY_TILE:
        return _diff_short_call(proj3, lam_p, subln_col, l)
    n_cache = 0 if cache_k is None else cache_k.shape[1]
    qb = min(Q_BLOCK, t)
    kt = min(t, DIFF_KEY_TILE)
    assert t % qb == 0 and t % kt == 0
    vrows = DV + DIFF_ONES_ROWS
    lam_init = 0.8 - 0.6 * math.exp(-0.3 * l)
    in_specs = [pl.BlockSpec((None, 4, DK), lambda i, h, j: (l, 0, 0)),
                pl.BlockSpec((None, qb, LANES), lambda i, h, j: (i, j, CB_CQ + h))]
    args = [lam_p, proj3]
    if n_cache:
        in_specs += [pl.BlockSpec((None, n_cache, LANES), lambda i, h, j: (i, 0, h)),
                     pl.BlockSpec((None, n_cache, DV), lambda i, h, j: (i, 0, h))]
        args += [cache_k, cache_v]
    in_specs += [pl.BlockSpec((None, t, LANES), lambda i, h, j: (i, 0, CB_CK + h)),
                 pl.BlockSpec((None, t, DV), lambda i, h, j: (i, 0, CB_CV + h)),
                 pl.BlockSpec((None, DV, 1), lambda i, h, j: (l, 0, 0))]
    args += [proj3, proj3, subln_col]
    return pl.pallas_call(
        functools.partial(_diff_kernel, lam_init=lam_init, n_cache=n_cache, n_new=t),
        grid=(b, N_HEAD, t // qb), in_specs=in_specs,
        out_specs=pl.BlockSpec((None, qb, DV), lambda i, h, j: (i, j, h)),
        out_shape=jax.ShapeDtypeStruct((b, t, N_HEAD * DV), BF16),
        scratch_shapes=([pltpu.VMEM((vrows, n_cache), BF16)] if n_cache else [])
        + [pltpu.VMEM((t // kt, vrows, kt), BF16), pltpu.VMEM((2, max(kt, n_cache), qb), F32),
           pltpu.VMEM((2, max(kt, n_cache), qb), F32), pltpu.VMEM((2, vrows, qb), F32)],
        compiler_params=_cparams(("arbitrary", "arbitrary", "arbitrary")),
    )(*args)


def _layer_norm(z, g, b):
    mu = jnp.mean(z, axis=-1, keepdims=True)
    zc = z - mu
    var = jnp.mean(zc * zc, axis=-1, keepdims=True)
    return zc * lax.rsqrt(var + EPS) * g + b


def _merge_kernel(x_ref, ya_ref, yb_ref, yc_ref, m_ref, g1_ref, wb_ref, wo_ref, lg_ref, lb_ref, o_ref):
    half = x_ref.shape[0] // 2
    merged = []
    for p in range(2):
        rs = slice(p * half, (p + 1) * half)
        acc = None
        for i, y_ref in enumerate((ya_ref, yb_ref, yc_ref)):
            gate = _sigmoid(m_ref[rs, i * D_MODEL:(i + 1) * D_MODEL].astype(F32))
            term = gate * _dot(y_ref[rs, :], wb_ref[i])
            acc = term if acc is None else acc + term
        merged.append(acc.astype(BF16))
    for p in range(2):
        rs = slice(p * half, (p + 1) * half)
        out = _dot(merged[p], wo_ref[...])
        z = ALPHA * x_ref[rs, :] + g1_ref[...] * out
        o_ref[rs, :] = _layer_norm(z, lg_ref[...], lb_ref[...])


def _merge_call(x, ya, yb, yc, proj, mod, l, row0, span, wts):
    n = x.shape[0]
    tb = TOKEN_BLOCK
    bw = N_HEAD * DV
    tok = lambda w: pl.BlockSpec((tb, w), lambda i: (i, 0))
    return pl.pallas_call(
        _merge_kernel,
        grid=(n // tb,),
        in_specs=[tok(D_MODEL), tok(bw), tok(bw), tok(bw),
                  pl.BlockSpec((tb, N_GATE_COLS), lambda i: (i, 0)),
                  _mod_spec(l, row0, span, 2),
                  _resident((3, bw, D_MODEL), l), _resident((D_MODEL, D_MODEL), l),
                  _resident((1, D_MODEL), l, 0), _resident((1, D_MODEL), l, 0)],
        out_specs=tok(D_MODEL),
        out_shape=jax.ShapeDtypeStruct((n, D_MODEL), F32),
        compiler_params=_cparams(("arbitrary",)),
    )(x, ya, yb, yc, proj, mod, wts['w_branch'], wts['w_out'], wts['ln_g'], wts['ln_b'])


def _regroup_kernel(w_ref, o_ref):
    for c in range(D_FF // FF_CHUNK):
        lo, hi = c * FF_CHUNK, (c + 1) * FF_CHUNK
        o_ref[:, 2 * lo:2 * lo + FF_CHUNK] = w_ref[:, lo:hi].astype(BF16)
        o_ref[:, 2 * lo + FF_CHUNK:2 * hi] = w_ref[:, D_FF + lo:D_FF + hi].astype(BF16)


def _regroup_w_up(w_up):
    rows = D_MODEL // 4
    return pl.pallas_call(
        _regroup_kernel,
        grid=(DEPTH, D_MODEL // rows),
        in_specs=[pl.BlockSpec((None, rows, 2 * D_FF), lambda l, j: (l, j, 0))],
        out_specs=pl.BlockSpec((None, rows, 2 * D_FF), lambda l, j: (l, j, 0)),
        out_shape=jax.ShapeDtypeStruct((DEPTH, D_MODEL, 2 * D_FF), BF16),
        compiler_params=_cparams(("arbitrary", "arbitrary")),
    )(w_up)


def _gelu_tanh(x):
    k = -2.0 * 0.7978845608028654
    w = x * (x * x * (k * 0.044715) + k)
    return x / (1.0 + jnp.exp(w))


def _ffn_kernel(xp_ref, x_ref, xn_ref, sh_ref, sc_ref, g2_ref, wu_ref, cw_ref, cb_ref, wd_ref,
                lg_ref, lb_ref, o_ref, acc_ref, *, seq_len):
    tb = x_ref.shape[0]
    halo = SUBLANES
    i = pl.program_id(0)
    xm = x_ref[...]
    scale = 1.0 + sc_ref[...]
    shift = sh_ref[...]
    starts_seq = (i * tb) % seq_len == 0
    ends_seq = ((i + 1) * tb) % seq_len == 0
    h_prev = jnp.where(starts_seq, 0.0, xp_ref[...] * scale + shift)
    h_next = jnp.where(ends_seq, 0.0, xn_ref[...] * scale + shift)
    h_mid = xm * scale + shift
    hm = h_mid.astype(BF16)
    h = jnp.concatenate([h_prev, h_mid, h_next], axis=0).astype(BF16)
    rows = tb + 2 * halo
    n_chunk = D_FF // FF_CHUNK

    cut = (rows // 2 + 15) // 16 * 16

    def up(c):
        w = wu_ref[:, 2 * c * FF_CHUNK:2 * (c + 1) * FF_CHUNK]
        u = jnp.concatenate([_dot(h[:cut], w), _dot(h[cut:], w)], axis=0)
        return u[:, :FF_CHUNK], u[halo:halo + tb, FF_CHUNK:]

    ab_next = up(0)
    for c in range(n_chunk):
        lo, hi = c * FF_CHUNK, (c + 1) * FF_CHUNK
        a, b = ab_next
        if c + 1 < n_chunk:
            ab_next = up(c + 1)
        a_prev = pltpu.roll(a, 1, 0)[halo:halo + tb]
        a_mid = a[halo:halo + tb]
        a_next = pltpu.roll(a, rows - 1, 0)[halo:halo + tb]
        w0, w1, w2 = cw_ref[0:1, lo:hi], cw_ref[1:2, lo:hi], cw_ref[2:3, lo:hi]
        cb = cb_ref[:, lo:hi]
        cv = a_prev * w0 + a_mid * w1 + a_next * w2 + cb
        for r in range(seq_len, tb, seq_len):
            sl = slice(r - SUBLANES, r + SUBLANES)
            rid = lax.broadcasted_iota(jnp.int32, (2 * SUBLANES, 1), 0)
            seam = (jnp.where(rid == SUBLANES, 0.0, a_prev[sl]) * w0 + a_mid[sl] * w1
                    + jnp.where(rid == SUBLANES - 1, 0.0, a_next[sl]) * w2 + cb)
            cv = jnp.concatenate([cv[:r - SUBLANES], seam, cv[r + SUBLANES:]], axis=0)
        act = (_gelu_tanh(cv) * b).astype(BF16)
        part = _dot(act, wd_ref[lo:hi, :])
        if c == 0:
            acc_ref[...] = part
        else:
            acc_ref[...] += part
    z = ALPHA * xm + g2_ref[...] * acc_ref[...]
    o_ref[...] = _layer_norm(z, lg_ref[...], lb_ref[...])


def _ffn_call(x, mod, l, row0, span, seq_len, wts):
    n = x.shape[0]
    tb = TOKEN_BLOCK
    hb = tb // SUBLANES
    last = n // SUBLANES - 1
    modspec = functools.partial(_mod_spec, l, row0, span)

    return pl.pallas_call(
        functools.partial(_ffn_kernel, seq_len=seq_len),
        grid=(n // tb,),
        in_specs=[pl.BlockSpec((SUBLANES, D_MODEL), lambda i: (jnp.maximum(i * hb - 1, 0), 0)),
                  pl.BlockSpec((tb, D_MODEL), lambda i: (i, 0)),
                  pl.BlockSpec((SUBLANES, D_MODEL), lambda i: (jnp.minimum((i + 1) * hb, last), 0)),
                  modspec(3), modspec(4), modspec(5),
                  _resident((D_MODEL, 2 * D_FF), l), _resident((3, D_FF), l),
                  _resident((1, D_FF), l), _resident((D_FF, D_MODEL), l),
                  _resident((1, D_MODEL), l, 1), _resident((1, D_MODEL), l, 1)],
        out_specs=pl.BlockSpec((tb, D_MODEL), lambda i: (i, 0)),
        out_shape=jax.ShapeDtypeStruct((n, D_MODEL), F32),
        scratch_shapes=[pltpu.VMEM((tb, D_MODEL), F32)],
        compiler_params=_cparams(("arbitrary",)),
    )(x, x, x, mod, mod, mod, wts['w_up'], wts['conv_w'], wts['conv_b'], wts['w_down'],
      wts['ln_g'], wts['ln_b'])


def _rope_tables(seq):
    half = DK // 4
    t = np.arange(seq)
    inv = ROPE_BASE ** (-np.arange(half, dtype=np.float64) / half)
    ang_row = (t // GRID_W)[:, None] * inv[None, :]
    ang_col = (t % GRID_W)[:, None] * inv[None, :]
    ang = np.concatenate([ang_row, ang_row, ang_col, ang_col], axis=1)
    first = (np.arange(DK) % (2 * half)) < half
    cos = np.cos(ang)
    sa = np.where(first[None, :], -np.sin(ang), 0.0)
    sb = np.where(first[None, :], 0.0, np.sin(ang))
    tile = lambda a: jnp.asarray(np.concatenate([a, a], axis=1), F32)
    return tile(cos), tile(sa), tile(sb)


def _trunk(x, mod, row0, seq_len, batch, l, wts, tables, ctx, carried):
    n = x.shape[0]
    span = n if ctx is None else seq_len
    kv_prev = None if carried is None else carried[:2]
    outs = _proj_call(x, mod, l, row0, span, seq_len, wts, tables, ctx is None, kv_prev)
    proj, la = outs[0], outs[1]
    proj3 = proj.reshape(batch, seq_len, D_IN)
    la3 = la.reshape(batch, seq_len, 4 * LANES)
    if ctx is None:
        ret_prev, gla_prev = carried[2:]
        ya, st_ret = _ret_call(proj3, wts['ret_decay'], l, None, ret_prev)
        yb, st_gla = _gla_call(proj3, la3, wts['gla_norm_g'], l, None, gla_prev)
        yc = _diff_call(proj3, None, None, wts['diff_lam'], wts['diff_subln_g'], l)
        carried = (outs[2], outs[3], st_ret, st_gla)
    else:
        ya, = _ret_call(proj3, wts['ret_decay'], l, ctx['ret'], None)
        yb, = _gla_call(proj3, la3, wts['gla_norm_g'], l, ctx['gla'], None)
        yc = _diff_call(proj3, ctx['dk'][l], ctx['dv'][l], wts['diff_lam'], wts['diff_subln_g'], l)
    bw = N_HEAD * DV
    x1 = _merge_call(x, ya.reshape(n, bw), yb.reshape(n, bw), yc.reshape(n, bw), proj, mod, l,
                     row0, span, wts)
    x2 = _ffn_call(x1, mod, l, row0, span, seq_len, wts)
    return x2, carried


def kernel(x_prompt, x_sample, cache_diff_k, cache_diff_v, state_ret, state_gla, c, c_ctx,
           ada_w, ada_b, w_in, ret_decay, gla_wa1, gla_wa2, gla_ba, gla_norm_g, diff_lam,
           diff_subln_g, w_branch, w_out, ln_g, ln_b, ffn_w_up, ffn_conv_w, ffn_conv_b, ffn_w_down):
    bp, tp, _ = x_prompt.shape
    bs, ts, _ = x_sample.shape
    past = cache_diff_k.shape[3]

    cvec = jnp.concatenate([c_ctx[None, :], c, jnp.zeros((SUBLANES - 1 - bs, D_MODEL), F32)], axis=0)
    mod = _ada_call(cvec, ada_w, ada_b).reshape(DEPTH, SUBLANES, 6, 1, D_MODEL)

    wa1 = jnp.concatenate([gla_wa1[:, 0], gla_wa1[:, 1]], axis=-1)
    wa1 = jnp.pad(wa1, ((0, 0), (0, 0), (0, LANES - 2 * GLA_RANK))).astype(BF16)
    wa2 = jnp.zeros((DEPTH, LANES, 4 * LANES), F32)
    wa2 = wa2.at[:, 0:GLA_RANK, 0:2 * LANES].set(gla_wa2[:, 0])
    wa2 = wa2.at[:, GLA_RANK:2 * GLA_RANK, 2 * LANES:].set(gla_wa2[:, 1]).astype(BF16)
    wts = {
        'w_in': w_in.astype(BF16), 'wa1': wa1, 'wa2': wa2,
        'ba': gla_ba.reshape(DEPTH, 1, 4 * LANES), 'ret_decay': ret_decay,
        'gla_norm_g': gla_norm_g.reshape(DEPTH, 1, DV), 'diff_lam': diff_lam,
        'diff_subln_g': diff_subln_g.reshape(DEPTH, DV, 1),
        'w_branch': w_branch.astype(BF16), 'w_out': w_out.astype(BF16),
        'ln_g': ln_g.reshape(DEPTH, 2, 1, D_MODEL), 'ln_b': ln_b.reshape(DEPTH, 2, 1, D_MODEL),
        'w_up': _regroup_w_up(ffn_w_up), 'conv_w': ffn_conv_w,
        'conv_b': ffn_conv_b.reshape(DEPTH, 1, D_FF), 'w_down': ffn_w_down.astype(BF16),
    }

    h = x_prompt.reshape(bp * tp, D_MODEL)
    carried = (jnp.zeros((bp, DEPTH, 2 * N_HEAD, tp, DK), F32), jnp.zeros((bp, DEPTH, N_HEAD, tp, DV), F32),
               jnp.zeros((bp, DEPTH, 2, N_HEAD, DK, DV), F32), jnp.zeros((bp, DEPTH, 2, N_HEAD, DK, DV), F32))
    for l in range(DEPTH):
        h, carried = _trunk(h, mod, 0, tp, bp, l, wts, None, None, carried)
    y_prompt = h.reshape(bp, tp, D_MODEL)

    tables = _rope_tables(ts)
    ctx = {
        'dk': cache_diff_k.transpose(1, 0, 3, 2, 4).reshape(DEPTH, bs, past, 2 * N_HEAD * DK).astype(BF16),
        'dv': cache_diff_v.transpose(1, 0, 3, 2, 4).reshape(DEPTH, bs, past, N_HEAD * DV).astype(BF16),
        'ret': state_ret, 'gla': state_gla,
    }
    z = x_sample.reshape(bs * ts, D_MODEL)
    for l in range(DEPTH):
        z, _ = _trunk(z, mod, 1, ts, bs, l, wts, tables, ctx, None)
    y_sample = z.reshape(bs, ts, D_MODEL)

    return (y_prompt, y_sample) + tuple(carried)
```
